```python
import jax, jax.numpy as jnp
from jax import lax
import numpy as np

D_MODEL = 2048
BATCH = 1
SEQ = 8192
DEPTH = 2

GDN_HEADS = 6
GDN_HEAD_DIM = 128
GDN_WIDTH = GDN_HEADS * GDN_HEAD_DIM
GDN_CONV = 4
GDN_CHUNK = 64
CONF_WIDTH = 768
CONF_CONV = 31
ATT_PATTERNS = ((128, 1), (512, 4), (2048, 16))
ATT_N_GROUPS = 3
ATT_HEADS = 6
ATT_HEAD_DIM = 128
ATT_WIDTH = ATT_HEADS * ATT_HEAD_DIM
ATT_BLOCK = 128
ROPE_THETA = 10000.0
POOL_WINDOWS = (2, 4, 8, 16)
POOL_GROUP_WIDTH = 192
POOL_WIDTH = 4 * POOL_GROUP_WIDTH
N_BRANCHES = 4
BRANCH_WIDTH = 768
IN_SPLIT_SIZES = (3 * GDN_WIDTH, GDN_WIDTH, GDN_HEADS, GDN_HEADS, 2 * CONF_WIDTH,
                  3 * ATT_N_GROUPS * ATT_WIDTH, POOL_WIDTH, N_BRANCHES * D_MODEL)
N_IN = 3 * GDN_WIDTH + GDN_WIDTH + 2 * GDN_HEADS + 2 * CONF_WIDTH + 3 * ATT_N_GROUPS * ATT_WIDTH + POOL_WIDTH + N_BRANCHES * D_MODEL
N_EXPERT_GROUPS = 8
EXPERTS_PER_GROUP = 8
N_EXPERTS = N_EXPERT_GROUPS * EXPERTS_PER_GROUP
TOP_K = 2
EXPERT_HIDDEN = 512
MOE_BLOCK = 128
NORM_EPS = 1e-6
LN_EPS = 1e-5

kernel_name = 'hybrid_gated_branch_hier_moe'


def rms_norm(x, g):
    x32 = x.astype(jnp.float32)
    y = x32 * lax.rsqrt(jnp.mean(x32 * x32, axis=-1, keepdims=True) + NORM_EPS)
    return (y * g.astype(jnp.float32)).astype(x.dtype)


def l2_normalize(t):
    return t * lax.rsqrt(jnp.sum(t * t, axis=-1, keepdims=True) + 1e-6)


def causal_depthwise_conv(x, w):
    kw, c = w.shape
    xp = jnp.pad(x, ((0, 0), (kw - 1, 0), (0, 0)))
    return lax.conv_general_dilated(xp, w[:, None, :].astype(x.dtype), (1,), 'VALID',
                                    dimension_numbers=('NWC', 'WIO', 'NWC'), feature_group_count=c)


def rotary(x, pos):
    hd = x.shape[-1]
    half = hd // 2
    inv = ROPE_THETA ** (-jnp.arange(half, dtype=jnp.float32) * 2.0 / hd)
    ang = pos.astype(jnp.float32)[..., None] * inv
    ang = ang.reshape(ang.shape[:2] + (1,) * (x.ndim - 3) + (half,))
    cos, sin = jnp.cos(ang), jnp.sin(ang)
    x32 = x.astype(jnp.float32)
    x1, x2 = x32[..., :half], x32[..., half:]
    return jnp.concatenate([x1 * cos - x2 * sin, x2 * cos + x1 * sin], axis=-1).astype(x.dtype)


def gated_delta_rule_chunked(q, k, v, g, beta):
    b_, s_, h_, dk = q.shape
    dv = v.shape[-1]
    c = GDN_CHUNK
    n = s_ // c

    def chunks(t):
        t = jnp.moveaxis(t, 2, 1)
        return t.reshape((b_, h_, n, c) + t.shape[3:])

    q, k, v, g, beta = (chunks(t) for t in (q, k, v, g, beta))
    gc = jnp.cumsum(g, axis=-1)
    idx = jnp.arange(c)
    incl = idx[:, None] >= idx[None, :]
    strict = idx[:, None] > idx[None, :]
    decay = jnp.exp(jnp.where(incl, gc[..., :, None] - gc[..., None, :], -jnp.inf))
    kk = jnp.einsum('bhncd,bhnmd->bhncm', k, k)
    a_mat = jnp.where(strict, beta[..., :, None] * kk * decay, 0.0)
    t_mat = a_mat + jnp.eye(c, dtype=jnp.float32)
    rhs = jnp.concatenate([v * beta[..., None], k * (beta * jnp.exp(gc))[..., None]], axis=-1)
    sol = lax.linalg.triangular_solve(t_mat, rhs, left_side=True, lower=True, unit_diagonal=True)
    u_c, w_c = sol[..., :dv], sol[..., dv:]
    qk = jnp.einsum('bhncd,bhnmd->bhncm', q, k) * decay
    q_dec = q * jnp.exp(gc)[..., None]
    k_dec = k * jnp.exp(gc[..., -1:] - gc)[..., None]
    g_last = jnp.exp(gc[..., -1])
    xs = tuple(jnp.moveaxis(t, 2, 0) for t in (qk, q_dec, k_dec, u_c, w_c, g_last))

    def step(state, inp):
        qk_c, qd_c, kd_c, uu, ww, gl = inp
        v_new = uu - jnp.einsum('bhcd,bhde->bhce', ww, state)
        o = jnp.einsum('bhcd,bhde->bhce', qd_c, state) + jnp.einsum('bhcm,bhme->bhce', qk_c, v_new)
        state = state * gl[..., None, None] + jnp.einsum('bhcd,bhce->bhde', kd_c, v_new)
        return state, o

    s0 = jnp.zeros((b_, h_, dk, dv), jnp.float32)
    _, o = lax.scan(step, s0, xs)
    o = jnp.moveaxis(o, 0, 2).reshape(b_, h_, s_, dv)
    return jnp.moveaxis(o, 1, 2)


def dilated_window_attention(q, k, v, window, dilation):
    b_, s_, h_, hd = q.shape
    length = s_ // dilation
    steps = window // dilation
    blk = ATT_BLOCK
    nb = -(-length // blk)
    lp = nb * blk

    def to_blocks(t):
        t = t.reshape(b_, length, dilation, h_, hd).transpose(0, 2, 1, 3, 4)
        t = jnp.pad(t, ((0, 0), (0, 0), (0, lp - length), (0, 0), (0, 0)))
        return t.reshape(b_, dilation, nb, blk, h_, hd).astype(jnp.float32)

    def with_prev(t):
        prev = jnp.pad(t, ((0, 0), (0, 0), (1, 0), (0, 0), (0, 0), (0, 0)))[:, :, :-1]
        return jnp.concatenate([prev, t], axis=3)

    qb = to_blocks(q)
    kb = with_prev(to_blocks(k))
    vb = with_prev(to_blocks(v))
    qpos = jnp.arange(nb)[:, None, None] * blk + jnp.arange(blk)[None, :, None]
    kpos = jnp.arange(nb)[:, None, None] * blk - blk + jnp.arange(2 * blk)[None, None, :]
    dist = qpos - kpos
    mask = (dist >= 0) & (dist <= steps) & (kpos >= 0)
    s = jnp.einsum('bdnqhe,bdnkhe->bdnhqk', qb, kb) * (hd ** -0.5)
    s = jnp.where(mask[:, None], s, -jnp.inf)
    m = jnp.max(s, axis=-1, keepdims=True)
    p = jnp.exp(s - m)
    l = jnp.sum(p, axis=-1)
    o = jnp.einsum('bdnhqk,bdnkhe->bdnqhe', p, vb) / jnp.moveaxis(l, 3, 4)[..., None]
    lse = jnp.moveaxis(m[..., 0] + jnp.log(l), 3, 4)

    def from_blocks(t):
        t = t.reshape((b_, dilation, lp) + t.shape[4:])[:, :, :length]
        t = jnp.moveaxis(t, 1, 2)
        return t.reshape((b_, s_) + t.shape[3:])

    return from_blocks(o), from_blocks(lse)


def trailing_mean_minus_self(x, w):
    s_ = x.shape[1]
    cs = jnp.cumsum(x, axis=1)
    lower = jnp.pad(cs, ((0, 0), (w, 0), (0, 0)))[:, :s_]
    count = jnp.minimum(jnp.arange(1, s_ + 1), w).astype(jnp.float32)[None, :, None]
    return (cs - lower) / count - x


def hybrid_mixer(h, pos, w_in, conv_a_w, a_log, dt_bias, gdn_norm_g, conv_b_w, conv_b_b,
                 ln_b_g, ln_b_b, pool_w, pool_scale, w_branch, w_out):
    b_, s_, _ = h.shape
    f32 = jnp.float32
    u = h @ w_in
    points = [int(p) for p in np.cumsum(IN_SPLIT_SIZES)[:-1]]
    u_qkv_a, u_z_a, u_beta_a, u_decay_a, u_b, u_c, u_d, u_gate = jnp.split(u, points, axis=-1)

    qkv = jax.nn.silu(causal_depthwise_conv(u_qkv_a, conv_a_w))
    qa, ka, va = (t.reshape(b_, s_, GDN_HEADS, GDN_HEAD_DIM).astype(f32) for t in jnp.split(qkv, 3, axis=-1))
    qa = l2_normalize(qa) * (GDN_HEAD_DIM ** -0.5)
    ka = l2_normalize(ka)
    beta = jax.nn.sigmoid(u_beta_a.astype(f32))
    g = -jnp.exp(a_log) * jax.nn.softplus(u_decay_a.astype(f32) + dt_bias)
    oa = gated_delta_rule_chunked(qa, ka, va, g, beta)
    oa = oa * lax.rsqrt(jnp.mean(oa * oa, axis=-1, keepdims=True) + NORM_EPS) * gdn_norm_g
    oa = oa * jax.nn.silu(u_z_a.astype(f32).reshape(b_, s_, GDN_HEADS, GDN_HEAD_DIM))
    y_a = oa.reshape(b_, s_, GDN_WIDTH)

    ga, gb = jnp.split(u_b, 2, axis=-1)
    glu = ga * jax.nn.sigmoid(gb)
    cb = (causal_depthwise_conv(glu, conv_b_w) + conv_b_b).astype(f32)
    mu = jnp.mean(cb, axis=-1, keepdims=True)
    var = jnp.mean(jnp.square(cb - mu), axis=-1, keepdims=True)
    y_b = jax.nn.silu((cb - mu) * lax.rsqrt(var + LN_EPS) * ln_b_g + ln_b_b)

    qc, kc, vc = (t.reshape(b_, s_, ATT_N_GROUPS, ATT_HEADS, ATT_HEAD_DIM) for t in jnp.split(u_c, 3, axis=-1))
    qc = rotary(qc, pos)
    kc = rotary(kc, pos)
    outs, lses = [], []
    for gi, (window, dilation) in enumerate(ATT_PATTERNS):
        o_g, lse_g = dilated_window_attention(qc[:, :, gi], kc[:, :, gi], vc[:, :, gi], window, dilation)
        outs.append(o_g)
        lses.append(lse_g)
    alpha = jax.nn.softmax(jnp.stack(lses, axis=2), axis=2)
    y_c = jnp.sum(alpha[..., None] * jnp.stack(outs, axis=2), axis=2).reshape(b_, s_, ATT_WIDTH)

    xd = u_d.astype(f32).reshape(b_, s_, len(POOL_WINDOWS), POOL_GROUP_WIDTH)
    pooled = jnp.stack([trailing_mean_minus_self(xd[:, :, i], w) for i, w in enumerate(POOL_WINDOWS)], axis=2)
    y_d = jnp.einsum('bsgc,gce->bsge', pooled, pool_w).reshape(b_, s_, POOL_WIDTH) * pool_scale

    branches = jnp.stack([y_a, y_b, y_c, y_d], axis=2).astype(h.dtype)
    proj = jnp.einsum('bskc,kcd->bskd', branches, w_branch)
    gates = jax.nn.sigmoid(u_gate.reshape(b_, s_, N_BRANCHES, D_MODEL))
    merged = jnp.sum(gates * proj, axis=2)
    return merged @ w_out


def hierarchical_moe(h, wg, bg, we, be, w_up, w_down):
    b_, s_, d_ = h.shape
    t_ = b_ * s_
    ht = h.reshape(t_, d_)
    gprob = jax.nn.softmax((ht @ wg).astype(jnp.float32) + bg, axis=-1)
    gp, gi = lax.top_k(gprob, 1)
    elog = ((ht @ we).astype(jnp.float32) + be).reshape(t_, N_EXPERT_GROUPS, EXPERTS_PER_GROUP)
    sel = jnp.broadcast_to(gi[:, :, None], (t_, 1, EXPERTS_PER_GROUP))
    eprob = jax.nn.softmax(jnp.take_along_axis(elog, sel, axis=1)[:, 0], axis=-1)
    ep, ei = lax.top_k(eprob, TOP_K)
    wts = gp * ep / jnp.sum(ep, axis=-1, keepdims=True)
    eid = gi * EXPERTS_PER_GROUP + ei

    n_assign = t_ * TOP_K
    flat_e = eid.reshape(n_assign)
    flat_w = wts.reshape(n_assign)
    flat_t = jnp.repeat(jnp.arange(t_, dtype=jnp.int32), TOP_K)
    order = jnp.argsort(flat_e)
    se, st, sw = flat_e[order], flat_t[order], flat_w[order]
    counts = jnp.zeros((N_EXPERTS,), jnp.int32).at[flat_e].add(1)
    padded = (counts + MOE_BLOCK - 1) // MOE_BLOCK * MOE_BLOCK
    pad_end = jnp.cumsum(padded)
    pad_start = pad_end - padded
    start = jnp.cumsum(counts) - counts
    dest = pad_start[se] + jnp.arange(n_assign, dtype=jnp.int32) - start[se]
    n_slots = -(-(n_assign + N_EXPERTS * (MOE_BLOCK - 1)) // MOE_BLOCK) * MOE_BLOCK
    n_blocks = n_slots // MOE_BLOCK
    slot_tok = jnp.zeros((n_slots,), jnp.int32).at[dest].set(st)
    slot_w = jnp.zeros((n_slots,), jnp.float32).at[dest].set(sw)
    block_e = jnp.minimum(jnp.searchsorted(pad_end, jnp.arange(n_blocks, dtype=jnp.int32) * MOE_BLOCK, side='right'),
                          N_EXPERTS - 1)
    xs = ht[slot_tok].reshape(n_blocks, MOE_BLOCK, d_)

    def expert_block(args):
        xb, e = args
        gate, up = jnp.split(xb @ w_up[e], 2, axis=-1)
        return (jax.nn.silu(gate) * up) @ w_down[e]

    ys = lax.map(expert_block, (xs, block_e)).reshape(n_slots, d_)
    out = jnp.zeros((t_, d_), jnp.float32).at[slot_tok].add(ys.astype(jnp.float32) * slot_w[:, None])
    return out.reshape(b_, s_, d_).astype(h.dtype)


def setup_inputs(seed: int = 0) -> dict:
    key = jax.random.key(seed)
    ks = jax.random.split(key, 24)
    nrm = jax.random.normal
    f32 = jnp.float32
    dt = jnp.exp(jax.random.uniform(ks[5], (DEPTH, GDN_HEADS), f32, jnp.log(1e-3), jnp.log(1e-1)))
    return {
        'x': nrm(ks[0], (BATCH, SEQ, D_MODEL), f32),
        'positions': jnp.broadcast_to(jnp.arange(SEQ, dtype=jnp.int32), (BATCH, SEQ)),
        'norm1_g': 1.0 + 0.02 * nrm(ks[1], (DEPTH, D_MODEL), f32),
        'w_in': nrm(ks[2], (DEPTH, D_MODEL, N_IN), f32) * D_MODEL ** -0.5,
        'conv_a_w': nrm(ks[3], (DEPTH, GDN_CONV, 3 * GDN_WIDTH), f32) * GDN_CONV ** -0.5,
        'a_log': jnp.log(jax.random.uniform(ks[4], (DEPTH, GDN_HEADS), f32, 1.0, 16.0)),
        'dt_bias': dt + jnp.log(-jnp.expm1(-dt)),
        'gdn_norm_g': 1.0 + 0.02 * nrm(ks[6], (DEPTH, GDN_HEAD_DIM), f32),
        'conv_b_w': nrm(ks[7], (DEPTH, CONF_CONV, CONF_WIDTH), f32) * CONF_CONV ** -0.5,
        'conv_b_b': 0.02 * nrm(ks[8], (DEPTH, CONF_WIDTH), f32),
        'ln_b_g': 1.0 + 0.02 * nrm(ks[9], (DEPTH, CONF_WIDTH), f32),
        'ln_b_b': 0.02 * nrm(ks[10], (DEPTH, CONF_WIDTH), f32),
        'pool_w': nrm(ks[11], (DEPTH, len(POOL_WINDOWS), POOL_GROUP_WIDTH, POOL_GROUP_WIDTH), f32) * POOL_GROUP_WIDTH ** -0.5,
        'pool_scale': 1.0 + 0.1 * nrm(ks[12], (DEPTH, POOL_WIDTH), f32),
        'w_branch': nrm(ks[13], (DEPTH, N_BRANCHES, BRANCH_WIDTH, D_MODEL), f32) * BRANCH_WIDTH ** -0.5,
        'w_out': nrm(ks[14], (DEPTH, D_MODEL, D_MODEL), f32) * D_MODEL ** -0.5,
        'norm2_g': 1.0 + 0.02 * nrm(ks[15], (DEPTH, D_MODEL), f32),
        'router_group_w': nrm(ks[16], (DEPTH, D_MODEL, N_EXPERT_GROUPS), f32) * D_MODEL ** -0.5,
        'router_group_b': 0.01 * nrm(ks[17], (DEPTH, N_EXPERT_GROUPS), f32),
        'router_expert_w': nrm(ks[18], (DEPTH, D_MODEL, N_EXPERTS), f32) * D_MODEL ** -0.5,
        'router_expert_b': 0.01 * nrm(ks[19], (DEPTH, N_EXPERTS), f32),
        'w_up': nrm(ks[20], (DEPTH, N_EXPERTS, D_MODEL, 2 * EXPERT_HIDDEN), f32) * D_MODEL ** -0.5,
        'w_down': nrm(ks[21], (DEPTH, N_EXPERTS, EXPERT_HIDDEN, D_MODEL), f32) * EXPERT_HIDDEN ** -0.5,
        'final_norm_g': 1.0 + 0.02 * nrm(ks[22], (D_MODEL,), f32),
    }


def reference(x, positions, norm1_g, w_in, conv_a_w, a_log, dt_bias, gdn_norm_g, conv_b_w, conv_b_b,
              ln_b_g, ln_b_b, pool_w, pool_scale, w_branch, w_out, norm2_g, router_group_w,
              router_group_b, router_expert_w, router_expert_b, w_up, w_down, final_norm_g):
    for layer in range(DEPTH):
        h = rms_norm(x, norm1_g[layer])
        x = x + hybrid_mixer(h, positions, w_in[layer], conv_a_w[layer], a_log[layer], dt_bias[layer],
                             gdn_norm_g[layer], conv_b_w[layer], conv_b_b[layer], ln_b_g[layer], ln_b_b[layer],
                             pool_w[layer], pool_scale[layer], w_branch[layer], w_out[layer])
        h = rms_norm(x, norm2_g[layer])
        x = x + hierarchical_moe(h, router_group_w[layer], router_group_b[layer], router_expert_w[layer],
                                 router_expert_b[layer], w_up[layer], w_down[layer])
    return rms_norm(x, final_norm_g)
```

```python
import functools

import jax
import jax.numpy as jnp
from jax import lax
from jax.experimental import pallas as pl
from jax.experimental.pallas import tpu as pltpu

F32 = jnp.float32
BF16 = jnp.bfloat16

D_MODEL = 2048
N_HEADS = 6
HEAD_DIM = 128
WIDTH = N_HEADS * HEAD_DIM
GDN_CONV = 4
GDN_CHUNK = 64
GDN_SUPER = 256
CONF_CONV = 31
ATT_PATTERNS = ((128, 1), (512, 4), (2048, 16))
ATT_BLOCK = 128
ROPE_THETA = 10000.0
POOL_WINDOWS = (2, 4, 8, 16)
POOL_GROUP = 192
N_BRANCHES = 4
N_GROUPS = 8
PER_GROUP = 8
N_EXPERTS = 64
TOP_K = 2
EXPERT_HIDDEN = 512
MOE_BLOCK = 128
LOCAL_ROWS = 32
NORM_EPS = 1e-6
LN_EPS = 1e-5

U_QKV_A = 0
U_Z_A = 2304
U_B = 3072
U_C = 4608
U_D = 11520
U_WIDTH = 12288
W_BETA = 3072
W_UB = 3084
W_GATE = 12300

VMEM_LIMIT = 56 * 1024 * 1024


def _cparams(semantics, vmem=VMEM_LIMIT):
    return pltpu.CompilerParams(dimension_semantics=semantics, vmem_limit_bytes=vmem)


def _sigmoid(x):
    return 1.0 / (1.0 + jnp.exp(-x))


def _silu(x):
    return x * _sigmoid(x)


def _dot(a, b):
    return jnp.dot(a, b, preferred_element_type=F32)


def _dot_nt(a, b):
    return lax.dot_general(a, b, (((1,), (1,)), ((), ())), preferred_element_type=F32)


def _dot_tn(a, b):
    return lax.dot_general(a, b, (((0,), (0,)), ((), ())), preferred_element_type=F32)


def _rmsnorm_body(x_ref, g_ref, o_ref):
    x = x_ref[...]
    y = x * lax.rsqrt(jnp.mean(x * x, axis=-1, keepdims=True) + NORM_EPS) * g_ref[...]
    o_ref[...] = y.astype(o_ref.dtype)


def rmsnorm(x, g, out_dtype, tm=512):
    t, d = x.shape
    return pl.pallas_call(
        _rmsnorm_body,
        grid=(t // tm,),
        in_specs=[pl.BlockSpec((tm, d), lambda i: (i, 0)), pl.BlockSpec((1, d), lambda i: (0, 0))],
        out_specs=pl.BlockSpec((tm, d), lambda i: (i, 0)),
        out_shape=jax.ShapeDtypeStruct((t, d), out_dtype),
        compiler_params=_cparams(("parallel",)),
        name="rmsnorm",
    )(x, g.reshape(1, d))


def _matmul_body(h_ref, w_ref, o_ref):
    o_ref[...] = _dot(h_ref[...], w_ref[...]).astype(o_ref.dtype)


def project(h, w, tm=1024, tn=768):
    t, k = h.shape
    n = w.shape[1]
    tn = min(tn, n)
    return pl.pallas_call(
        _matmul_body,
        grid=(n // tn, t // tm),
        in_specs=[pl.BlockSpec((tm, k), lambda j, i: (i, 0)), pl.BlockSpec((k, tn), lambda j, i: (0, j))],
        out_specs=pl.BlockSpec((tm, tn), lambda j, i: (i, j)),
        out_shape=jax.ShapeDtypeStruct((t, n), F32),
        compiler_params=_cparams(("parallel", "parallel")),
        name="in_proj",
    )(h, w)


def _rope_body(pos_ref, inv_ref, cos_ref, sin_ref):
    ang = pos_ref[...].astype(F32) * inv_ref[...]
    lane = lax.broadcasted_iota(jnp.int32, ang.shape, 1)
    cos_ref[...] = jnp.cos(ang)
    s = jnp.sin(ang)
    sin_ref[...] = jnp.where(lane < HEAD_DIM // 2, -s, s)


def rope_tables(pos, tm=1024):
    t = pos.shape[0]
    half = HEAD_DIM // 2
    inv = ROPE_THETA ** (-jnp.arange(half, dtype=F32) * 2.0 / HEAD_DIM)
    inv = jnp.concatenate([inv, inv]).reshape(1, HEAD_DIM)
    spec = pl.BlockSpec((tm, HEAD_DIM), lambda i: (i, 0))
    return pl.pallas_call(
        _rope_body,
        grid=(t // tm,),
        in_specs=[pl.BlockSpec((tm, 1), lambda i: (i, 0)), pl.BlockSpec((1, HEAD_DIM), lambda i: (0, 0))],
        out_specs=[spec, spec],
        out_shape=[jax.ShapeDtypeStruct((t, HEAD_DIM), F32)] * 2,
        compiler_params=_cparams(("parallel",)),
        name="rope_tables",
    )(pos.reshape(t, 1), inv)


def _rope(x, cos, sin):
    return x * cos + pltpu.roll(x, HEAD_DIM // 2, axis=1) * sin


def _attn_body(*refs, first):
    if first:
        q_ref, kc_ref, kp_ref, vc_ref, vp_ref, cc_ref, sc_ref, cp_ref, sp_ref, o_out, lse_out = refs
        o_in = lse_in = None
    else:
        (q_ref, kc_ref, kp_ref, vc_ref, vp_ref, cc_ref, sc_ref, cp_ref, sp_ref,
         o_in, lse_in, o_out, lse_out) = refs
    n = pl.program_id(1)
    blk = ATT_BLOCK
    cosc, sinc, cosp, sinp = cc_ref[...], sc_ref[...], cp_ref[...], sp_ref[...]
    row = lax.broadcasted_iota(jnp.int32, (blk, blk), 0)
    col = lax.broadcasted_iota(jnp.int32, (blk, blk), 1)
    mask_cur = col <= row
    mask_prev = (col >= row) & (n > 0)
    scale = HEAD_DIM ** -0.5
    lse_prev = None if first else lse_in[...]
    lse_new = jnp.zeros((blk, HEAD_DIM), F32)
    neg = -jnp.inf
    for h in range(N_HEADS):
        sl = slice(h * HEAD_DIM, (h + 1) * HEAD_DIM)
        q = (_rope(q_ref[:, sl], cosc, sinc) * scale).astype(BF16)
        kc = _rope(kc_ref[:, sl], cosc, sinc).astype(BF16)
        kp = _rope(kp_ref[:, sl], cosp, sinp).astype(BF16)
        s_c = jnp.where(mask_cur, _dot_nt(q, kc), neg)
        s_p = jnp.where(mask_prev, _dot_nt(q, kp), neg)
        m = jnp.maximum(jnp.max(s_c, axis=-1, keepdims=True), jnp.max(s_p, axis=-1, keepdims=True))
        if not first:
            m_old = lse_prev[:, h:h + 1]
            m = jnp.maximum(m, m_old)
        p_c = jnp.exp(s_c - m)
        p_p = jnp.exp(s_p - m)
        l = jnp.sum(p_c, axis=-1, keepdims=True) + jnp.sum(p_p, axis=-1, keepdims=True)
        acc = _dot(p_c.astype(BF16), vc_ref[:, sl].astype(BF16)) + _dot(p_p.astype(BF16), vp_ref[:, sl].astype(BF16))
        if not first:
            w_old = jnp.exp(m_old - m)
            l = l + w_old
            acc = acc + w_old * o_in[:, sl]
        o_out[:, sl] = acc / l
        lse_new = jnp.where(col == h, m + jnp.log(l), lse_new)
    lse_out[...] = lse_new


def attention_group(u, cos_t, sin_t, group, dilation, o_prev, lse_prev):
    t = u.shape[0]
    d = dilation
    length = t // d
    nb = length // ATT_BLOCK
    blk = ATT_BLOCK
    first = o_prev is None
    ucols = U_WIDTH // WIDTH
    uv = u.reshape(length, d * U_WIDTH)
    cv = cos_t.reshape(length, d * HEAD_DIM)
    sv = sin_t.reshape(length, d * HEAD_DIM)
    qoff, koff, voff = U_C // WIDTH + group, U_C // WIDTH + 3 + group, U_C // WIDTH + 6 + group

    def cur(off):
        return pl.BlockSpec((blk, WIDTH), lambda r, n: (n, r * ucols + off))

    def prev(off):
        return pl.BlockSpec((blk, WIDTH), lambda r, n: (jnp.maximum(n - 1, 0), r * ucols + off))

    tab_cur = pl.BlockSpec((blk, HEAD_DIM), lambda r, n: (n, r))
    tab_prev = pl.BlockSpec((blk, HEAD_DIM), lambda r, n: (jnp.maximum(n - 1, 0), r))
    o_spec = pl.BlockSpec((blk, WIDTH), lambda r, n: (n, r))
    in_specs = [cur(qoff), cur(koff), prev(koff), cur(voff), prev(voff), tab_cur, tab_cur, tab_prev, tab_prev]
    args = [uv, uv, uv, uv, uv, cv, sv, cv, sv]
    if not first:
        in_specs += [o_spec, tab_cur]
        args += [o_prev.reshape(length, d * WIDTH), lse_prev.reshape(length, d * HEAD_DIM)]
    o, lse = pl.pallas_call(
        functools.partial(_attn_body, first=first),
        grid=(d, nb),
        in_specs=in_specs,
        out_specs=[o_spec, tab_cur],
        out_shape=[jax.ShapeDtypeStruct((length, d * WIDTH), F32),
                   jax.ShapeDtypeStruct((length, d * HEAD_DIM), F32)],
        compiler_params=_cparams(("parallel", "parallel")),
        name=f"attn_d{d}",
    )(*args)
    return o.reshape(t, WIDTH), lse.reshape(t, HEAD_DIM)


def _local_body(ub_ref, ud_ref, cw_ref, cb_ref, lg_ref, lb_ref, pw_ref, ps_ref, yb_ref, yd_ref, extb, extd, pool_buf, *, tm):
    i = pl.program_id(0)
    hb, hd = 32, 16

    @pl.when(i == 0)
    def _():
        extb[0:hb, :] = jnp.zeros((hb, WIDTH), F32)
        extd[0:hd, :] = jnp.zeros((hd, WIDTH), F32)

    @pl.when(i > 0)
    def _():
        extb[0:hb, :] = extb[tm:tm + hb, :]
        extd[0:hd, :] = extd[tm:tm + hd, :]

    ub = ub_ref[...]
    extb[hb:hb + tm, :] = ub[:, :WIDTH] * _sigmoid(ub[:, WIDTH:])
    extd[hd:hd + tm, :] = ud_ref[...]

    rc = LOCAL_ROWS
    ch = lax.broadcasted_iota(jnp.int32, (rc, WIDTH), 1)
    rows = lax.broadcasted_iota(jnp.int32, (rc, WIDTH), 0)
    for c0 in range(0, tm, rc):
        acc = jnp.zeros((rc, WIDTH), F32)
        for j in range(CONF_CONV):
            off = c0 + hb - (CONF_CONV - 1) + j
            acc = acc + cw_ref[j:j + 1, :] * extb[off:off + rc, :]
        cb = acc + cb_ref[...]
        mu = jnp.mean(cb, axis=-1, keepdims=True)
        cen = cb - mu
        var = jnp.mean(cen * cen, axis=-1, keepdims=True)
        yb = _silu(cen * lax.rsqrt(var + LN_EPS) * lg_ref[...] + lb_ref[...])
        yb_ref[c0:c0 + rc, :] = yb.astype(yb_ref.dtype)

        xd = extd[c0 + hd:c0 + hd + rc, :]
        tpos = i * tm + c0 + rows + 1
        run = xd
        pooled = None
        shift = 1
        for gi, w in enumerate(POOL_WINDOWS):
            while shift < w:
                run = run + extd[c0 + hd - shift:c0 + hd - shift + rc, :]
                shift += 1
            val = run / jnp.minimum(tpos, w).astype(F32) - xd
            pooled = val if pooled is None else jnp.where(ch >= gi * POOL_GROUP, val, pooled)
        pool_buf[c0:c0 + rc, :] = pooled.astype(pool_buf.dtype)
    yd = _dot(pool_buf[...], pw_ref[...]) * ps_ref[...]
    yd_ref[...] = yd.astype(yd_ref.dtype)


def local_mixers(u, conv_w, conv_b, ln_g, ln_b, pool_w_bd, pool_scale, tm=256):
    t = u.shape[0]
    row = lambda a: a.reshape(1, WIDTH)
    vec = pl.BlockSpec((1, WIDTH), lambda i: (0, 0))
    out = pl.BlockSpec((tm, WIDTH), lambda i: (i, 0))
    return pl.pallas_call(
        functools.partial(_local_body, tm=tm),
        grid=(t // tm,),
        in_specs=[pl.BlockSpec((tm, 2 * WIDTH), lambda i: (i, U_B // (2 * WIDTH))),
                  pl.BlockSpec((tm, WIDTH), lambda i: (i, U_D // WIDTH)),
                  pl.BlockSpec((CONF_CONV, WIDTH), lambda i: (0, 0)), vec, vec, vec,
                  pl.BlockSpec((WIDTH, WIDTH), lambda i: (0, 0)), vec],
        out_specs=[out, out],
        out_shape=[jax.ShapeDtypeStruct((t, WIDTH), BF16)] * 2,
        scratch_shapes=[pltpu.VMEM((tm + 32, WIDTH), F32), pltpu.VMEM((tm + 16, WIDTH), F32),
                        pltpu.VMEM((tm, WIDTH), BF16)],
        compiler_params=_cparams(("arbitrary",)),
        name="conv_pool",
    )(u, u, conv_w, row(conv_b), row(ln_g), row(ln_b), pool_w_bd, row(pool_scale))


def _gdn_body(qkv_ref, z_ref, bd_ref, cw_ref, alog_ref, dtb_ref, gn_ref, ya_ref, ext, state):
    s = pl.program_id(0)
    sup, c = GDN_SUPER, GDN_CHUNK
    halo = 8

    @pl.when(s == 0)
    def _():
        ext[0:halo, :] = jnp.zeros((halo, 3 * WIDTH), F32)
        state[...] = jnp.zeros(state.shape, F32)

    @pl.when(s > 0)
    def _():
        ext[0:halo, :] = ext[sup:sup + halo, :]

    ext[halo:halo + sup, :] = qkv_ref[...]
    acc = jnp.zeros((sup, 3 * WIDTH), F32)
    for j in range(GDN_CONV):
        off = halo - (GDN_CONV - 1) + j
        acc = acc + cw_ref[j:j + 1, :] * ext[off:off + sup, :]
    qkv = _silu(acc)

    bd = bd_ref[...]
    beta_all = _sigmoid(bd)
    xs = bd + dtb_ref[...]
    softplus = jnp.maximum(xs, 0.0) + jnp.log(1.0 + jnp.exp(-jnp.abs(xs)))
    g_all = -jnp.exp(alog_ref[...]) * softplus
    rowi = lax.broadcasted_iota(jnp.int32, (sup, HEAD_DIM), 0)
    in_chunk = rowi & (c - 1)
    gc = g_all
    sh = 1
    while sh < c:
        gc = gc + jnp.where(in_chunk >= sh, pltpu.roll(gc, sh, axis=0), 0.0)
        sh *= 2
    gct = gc.T

    ri = lax.broadcasted_iota(jnp.int32, (sup, sup), 0)
    ci = lax.broadcasted_iota(jnp.int32, (sup, sup), 1)
    same = (ri & -c) == (ci & -c)
    incl = same & (ri >= ci)
    strict = same & (ri > ci)
    eye = (ri == ci).astype(F32)

    for h in range(N_HEADS):
        sl = slice(h * HEAD_DIM, (h + 1) * HEAD_DIM)
        q = qkv[:, sl]
        k = qkv[:, WIDTH + h * HEAD_DIM:WIDTH + (h + 1) * HEAD_DIM]
        v = qkv[:, 2 * WIDTH + h * HEAD_DIM:2 * WIDTH + (h + 1) * HEAD_DIM]
        q = q * lax.rsqrt(jnp.sum(q * q, axis=-1, keepdims=True) + 1e-6) * (HEAD_DIM ** -0.5)
        k = k * lax.rsqrt(jnp.sum(k * k, axis=-1, keepdims=True) + 1e-6)
        b_col = beta_all[:, h:h + 1]
        gc_col = gc[:, N_HEADS + h:N_HEADS + h + 1]
        gc_row = gct[N_HEADS + h:N_HEADS + h + 1, :]
        dm = jnp.exp(jnp.where(incl, gc_col - gc_row, -jnp.inf))
        kb = k.astype(BF16)
        kk = _dot_nt(kb, kb)
        nmat = jnp.where(strict, -(b_col * kk * dm), 0.0)
        pmat = eye + nmat
        mb = nmat.astype(BF16)
        step = 2
        while step < c:
            m2 = _dot(mb, mb).astype(BF16)
            pmat = pmat + _dot(pmat.astype(BF16), m2)
            mb = m2
            step *= 2
        eg = jnp.exp(gc_col)
        rhs = jnp.concatenate([v * b_col, k * (b_col * eg)], axis=1).astype(BF16)
        sol = _dot(pmat.astype(BF16), rhs)
        u_c, w_c = sol[:, :HEAD_DIM], sol[:, HEAD_DIM:]
        qk = _dot_nt(q.astype(BF16), kb) * dm
        q_dec = (q * eg).astype(BF16)
        gl_col = jnp.concatenate(
            [jnp.broadcast_to(gc_col[(n + 1) * c - 1:(n + 1) * c, :], (c, 1)) for n in range(sup // c)], axis=0)
        k_dec = k * jnp.exp(gl_col - gc_col)
        w_b = w_c.astype(BF16)
        qk_b = qk.astype(BF16)
        st = state[h]
        outs = []
        for n in range(sup // c):
            rs = slice(n * c, (n + 1) * c)
            sb = st.astype(BF16)
            v_new = u_c[rs] - _dot(w_b[rs], sb)
            vb = v_new.astype(BF16)
            outs.append(_dot(q_dec[rs], sb) + _dot(qk_b[rs, rs], vb))
            g_last = jnp.exp(gc_col[(n + 1) * c - 1:(n + 1) * c, :])
            st = st * g_last + _dot(k_dec[rs].T.astype(BF16), vb)
        state[h] = st
        o = jnp.concatenate(outs, axis=0)
        o = o * lax.rsqrt(jnp.mean(o * o, axis=-1, keepdims=True) + NORM_EPS) * gn_ref[...]
        ya_ref[:, sl] = (o * _silu(z_ref[:, sl])).astype(ya_ref.dtype)


def gated_deltanet(u, bd, conv_w, a_log, dt_bias, norm_g):
    t = u.shape[0]
    sup = GDN_SUPER
    pad = jnp.zeros((N_HEADS,), F32)
    lane_row = lambda a: jnp.concatenate([pad, a, jnp.zeros((HEAD_DIM - 2 * N_HEADS,), F32)]).reshape(1, HEAD_DIM)
    vec = pl.BlockSpec((1, HEAD_DIM), lambda s: (0, 0))
    return pl.pallas_call(
        _gdn_body,
        grid=(t // sup,),
        in_specs=[pl.BlockSpec((sup, 3 * WIDTH), lambda s: (s, U_QKV_A // (3 * WIDTH))),
                  pl.BlockSpec((sup, WIDTH), lambda s: (s, U_Z_A // WIDTH)),
                  pl.BlockSpec((sup, HEAD_DIM), lambda s: (s, 0)),
                  pl.BlockSpec((GDN_CONV, 3 * WIDTH), lambda s: (0, 0)), vec, vec, vec],
        out_specs=pl.BlockSpec((sup, WIDTH), lambda s: (s, 0)),
        out_shape=jax.ShapeDtypeStruct((t, WIDTH), BF16),
        scratch_shapes=[pltpu.VMEM((sup + 8, 3 * WIDTH), F32), pltpu.VMEM((N_HEADS, HEAD_DIM, HEAD_DIM), F32)],
        compiler_params=_cparams(("arbitrary",)),
        name="gated_deltanet",
    )(u, u, bd, conv_w, lane_row(a_log), lane_row(dt_bias), norm_g.reshape(1, HEAD_DIM))


def _merge_body(h_ref, ya_ref, yb_ref, yc_ref, yd_ref, g0_ref, g1_ref, g2_ref, g3_ref,
                b0_ref, b1_ref, b2_ref, b3_ref, wo_ref, x_ref, n2_ref, wr_ref, br_ref,
                x1_ref, rl_ref, acc):
    j = pl.program_id(1)

    @pl.when(j == 0)
    def _():
        acc[...] = jnp.zeros(acc.shape, F32)

    hb = h_ref[...]
    merged = None
    for y_ref, g_ref, b_ref in ((ya_ref, g0_ref, b0_ref), (yb_ref, g1_ref, b1_ref),
                                (yc_ref, g2_ref, b2_ref), (yd_ref, g3_ref, b3_ref)):
        gate = _sigmoid(_dot(hb, g_ref[...]))
        term = gate * _dot(y_ref[...].astype(BF16), b_ref[...])
        merged = term if merged is None else merged + term
    acc[...] += _dot(merged.astype(BF16), wo_ref[...])

    @pl.when(j == pl.num_programs(1) - 1)
    def _():
        x1 = x_ref[...] + acc[...]
        x1_ref[...] = x1
        h2 = x1 * lax.rsqrt(jnp.mean(x1 * x1, axis=-1, keepdims=True) + NORM_EPS) * n2_ref[...]
        rl_ref[...] = jnp.dot(h2, wr_ref[...], preferred_element_type=F32,
                              precision=lax.Precision.HIGHEST) + br_ref[...]


def merge_project(h, ya, yb, yc, yd, w_gate, w_branch, w_out, x, norm2_g, w_router, b_router, tm=512, tn=256):
    t = h.shape[0]
    d = D_MODEL
    nj = d // tn
    row = pl.BlockSpec((tm, d), lambda i, j: (i, 0))
    ysp = pl.BlockSpec((tm, WIDTH), lambda i, j: (i, 0))
    gate_spec = lambda k: pl.BlockSpec((d, tn), lambda i, j: (0, k * nj + j))
    br_spec = lambda k: pl.BlockSpec((None, WIDTH, tn), lambda i, j: (k, 0, j))
    lanes = pl.BlockSpec((tm, HEAD_DIM), lambda i, j: (i, 0))
    return pl.pallas_call(
        _merge_body,
        grid=(t // tm, nj),
        in_specs=[row, ysp, ysp, ysp, ysp] + [gate_spec(k) for k in range(4)] + [br_spec(k) for k in range(4)]
        + [pl.BlockSpec((tn, d), lambda i, j: (j, 0)), row, pl.BlockSpec((1, d), lambda i, j: (0, 0)),
           pl.BlockSpec((d, HEAD_DIM), lambda i, j: (0, 0)), pl.BlockSpec((1, HEAD_DIM), lambda i, j: (0, 0))],
        out_specs=[row, lanes],
        out_shape=[jax.ShapeDtypeStruct((t, d), F32), jax.ShapeDtypeStruct((t, HEAD_DIM), F32)],
        scratch_shapes=[pltpu.VMEM((tm, d), F32)],
        compiler_params=_cparams(("parallel", "arbitrary")),
        name="merge_project",
    )(h, ya, yb, yc, yd, w_gate, w_gate, w_gate, w_gate, w_branch, w_branch, w_branch, w_branch,
      w_out, x, norm2_g.reshape(1, d), w_router, b_router)


def _router_body(rl_ref, eid_ref, wts_ref):
    rl = rl_ref[...]
    lane_i = lax.broadcasted_iota(jnp.int32, rl.shape, 1)
    lane = lane_i.astype(F32)
    neg = -jnp.inf
    big = 1e9
    gl = jnp.where(lane < N_GROUPS, rl, neg)
    gmax = jnp.max(gl, axis=-1, keepdims=True)
    gi = jnp.min(jnp.where(gl == gmax, lane, big), axis=-1, keepdims=True)
    gp = 1.0 / jnp.sum(jnp.exp(gl - gmax), axis=-1, keepdims=True)
    lo = N_GROUPS + gi * PER_GROUP
    el = jnp.where((lane >= lo) & (lane < lo + PER_GROUP), rl, neg)
    e1 = jnp.max(el, axis=-1, keepdims=True)
    i1 = jnp.min(jnp.where(el == e1, lane, big), axis=-1, keepdims=True)
    el2 = jnp.where(lane == i1, neg, el)
    e2 = jnp.max(el2, axis=-1, keepdims=True)
    i2 = jnp.min(jnp.where(el2 == e2, lane, big), axis=-1, keepdims=True)
    esum = jnp.sum(jnp.exp(el - e1), axis=-1, keepdims=True)
    p1 = 1.0 / esum
    p2 = jnp.exp(e2 - e1) / esum
    w1 = gp * p1 / (p1 + p2)
    w2 = gp * p2 / (p1 + p2)
    eid = jnp.where(lane_i == 0, i1 - N_GROUPS, jnp.where(lane_i == 1, i2 - N_GROUPS, 0.0))
    eid_ref[...] = eid.astype(jnp.int32)
    wts_ref[...] = jnp.where(lane_i == 0, w1, jnp.where(lane_i == 1, w2, 0.0))


def route(rl, tm=512):
    t = rl.shape[0]
    spec = pl.BlockSpec((tm, HEAD_DIM), lambda i: (i, 0))
    return pl.pallas_call(
        _router_body,
        grid=(t // tm,),
        in_specs=[spec],
        out_specs=[spec, spec],
        out_shape=[jax.ShapeDtypeStruct((t, HEAD_DIM), jnp.int32), jax.ShapeDtypeStruct((t, HEAD_DIM), F32)],
        compiler_params=_cparams(("parallel",)),
        name="router",
    )(rl)


def _gather_copy(h_hbm, tok_ref, xbuf, gsem, blk, slot, r):
    tok = tok_ref[blk * MOE_BLOCK + r]
    return pltpu.make_async_copy(h_hbm.at[pl.ds(tok, 1), :], xbuf.at[slot, pl.ds(r, 1), :], gsem.at[slot])


def _scatter_copy(y_hbm, dst_ref, obuf, ssem, blk, slot, r):
    dst = dst_ref[blk * MOE_BLOCK + r]
    return pltpu.make_async_copy(obuf.at[slot, pl.ds(r, 1), :], y_hbm.at[pl.ds(dst, 1), :], ssem.at[slot])


def _expert_body(be_ref, nused_ref, tok_ref, dst_ref, h_hbm, g_ref, wup_ref, wdn_ref, y_hbm,
                 xbuf, obuf, wup_b, wdn_b, gsem, ssem):
    b = pl.program_id(0)
    n_used = nused_ref[0]
    slot = b % 2
    other = 1 - slot

    @pl.when(b == 0)
    def _():
        for r in range(MOE_BLOCK):
            _gather_copy(h_hbm, tok_ref, xbuf, gsem, b, slot, r).start()
        obuf[...] = jnp.zeros(obuf.shape, F32)
        n_real = y_hbm.shape[0] - 2 * MOE_BLOCK
        for half in range(2):
            spare = pltpu.make_async_copy(obuf.at[half], y_hbm.at[pl.ds(n_real + half * MOE_BLOCK, MOE_BLOCK), :],
                                          ssem.at[half])
            spare.start()
            spare.wait()

    @pl.when(b < n_used)
    def _():
        for r in range(MOE_BLOCK):
            _gather_copy(h_hbm, tok_ref, xbuf, gsem, b, slot, r).wait()

        @pl.when(b + 1 < n_used)
        def _():
            for r in range(MOE_BLOCK):
                _gather_copy(h_hbm, tok_ref, xbuf, gsem, b + 1, other, r).start()

        changed = jnp.logical_or(b == 0, be_ref[b] != be_ref[jnp.maximum(b - 1, 0)])

        @pl.when(changed)
        def _():
            wup_b[...] = wup_ref[...].astype(BF16)
            wdn_b[...] = wdn_ref[...].astype(BF16)

        xr = xbuf[slot]
        xn = xr * lax.rsqrt(jnp.mean(xr * xr, axis=-1, keepdims=True) + NORM_EPS) * g_ref[...]
        gu = _dot(xn.astype(BF16), wup_b[...])
        act = (_silu(gu[:, :EXPERT_HIDDEN]) * gu[:, EXPERT_HIDDEN:]).astype(BF16)
        y = _dot(act, wdn_b[...])

        @pl.when(b >= 2)
        def _():
            for r in range(MOE_BLOCK):
                _scatter_copy(y_hbm, dst_ref, obuf, ssem, b - 2, slot, r).wait()

        obuf[slot] = y
        for r in range(MOE_BLOCK):
            _scatter_copy(y_hbm, dst_ref, obuf, ssem, b, slot, r).start()

        @pl.when(b == n_used - 1)
        def _():
            @pl.when(b >= 1)
            def _():
                for r in range(MOE_BLOCK):
                    _scatter_copy(y_hbm, dst_ref, obuf, ssem, b - 1, other, r).wait()

            for r in range(MOE_BLOCK):
                _scatter_copy(y_hbm, dst_ref, obuf, ssem, b, slot, r).wait()


def experts(x1, norm_g, w_up, w_down, block_e, n_used, slot_tok, slot_dst, n_rows):
    n_blocks = block_e.shape[0]
    d = D_MODEL
    grid_spec = pltpu.PrefetchScalarGridSpec(
        num_scalar_prefetch=4,
        grid=(n_blocks,),
        in_specs=[pl.BlockSpec(memory_space=pl.ANY),
                  pl.BlockSpec((1, d), lambda b, be, nu, tok, dst: (0, 0)),
                  pl.BlockSpec((None, d, 2 * EXPERT_HIDDEN), lambda b, be, nu, tok, dst: (be[b], 0, 0)),
                  pl.BlockSpec((None, EXPERT_HIDDEN, d), lambda b, be, nu, tok, dst: (be[b], 0, 0))],
        out_specs=pl.BlockSpec(memory_space=pl.ANY),
        scratch_shapes=[pltpu.VMEM((2, MOE_BLOCK, d), F32), pltpu.VMEM((2, MOE_BLOCK, d), F32),
                        pltpu.VMEM((d, 2 * EXPERT_HIDDEN), BF16), pltpu.VMEM((EXPERT_HIDDEN, d), BF16),
                        pltpu.SemaphoreType.DMA((2,)), pltpu.SemaphoreType.DMA((2,))],
    )
    return pl.pallas_call(
        _expert_body,
        grid_spec=grid_spec,
        out_shape=jax.ShapeDtypeStruct((n_rows, d), F32),
        compiler_params=_cparams(("arbitrary",)),
        name="experts",
    )(block_e, n_used, slot_tok, slot_dst, x1, norm_g.reshape(1, d), w_up, w_down)


def dispatch_tables(eid, t):
    n_assign = t * TOP_K
    flat_e = eid.reshape(n_assign)
    onehot = (flat_e[:, None] == jnp.arange(N_EXPERTS, dtype=jnp.int32)[None, :]).astype(jnp.int32)
    csum = jnp.cumsum(onehot, axis=0)
    counts = csum[-1]
    rank = jnp.sum(csum * onehot, axis=1) - 1
    padded = (counts + MOE_BLOCK - 1) // MOE_BLOCK * MOE_BLOCK
    pad_end = jnp.cumsum(padded)
    pad_start = pad_end - padded
    dest = pad_start[flat_e] + rank
    n_blocks = -(-(n_assign + N_EXPERTS * (MOE_BLOCK - 1)) // MOE_BLOCK)
    n_slots = n_blocks * MOE_BLOCK
    flat_t = jnp.arange(n_assign, dtype=jnp.int32) // TOP_K
    slot_tok = jnp.zeros((n_slots,), jnp.int32).at[dest].set(flat_t)
    spare = n_assign + (jnp.arange(n_slots, dtype=jnp.int32) % (2 * MOE_BLOCK))
    slot_dst = spare.at[dest].set(jnp.arange(n_assign, dtype=jnp.int32))
    n_used = pad_end[-1] // MOE_BLOCK
    starts = jnp.arange(n_blocks, dtype=jnp.int32) * MOE_BLOCK
    block_e = jnp.minimum(jnp.searchsorted(pad_end, starts, side="right"), N_EXPERTS - 1).astype(jnp.int32)
    last_e = block_e[jnp.maximum(n_used - 1, 0)]
    block_e = jnp.where(jnp.arange(n_blocks) < n_used, block_e, last_e)
    return block_e, n_used.reshape(1).astype(jnp.int32), slot_tok, slot_dst, n_assign + 2 * MOE_BLOCK


def _combine_body(x_ref, y_ref, w_ref, g_ref, x2_ref, hn_ref):
    w = w_ref[...]
    y = y_ref[...]
    x2 = x_ref[...] + (w[:, 0:1] * y[:, :D_MODEL] + w[:, 1:2] * y[:, D_MODEL:])
    x2_ref[...] = x2
    hn = x2 * lax.rsqrt(jnp.mean(x2 * x2, axis=-1, keepdims=True) + NORM_EPS) * g_ref[...]
    hn_ref[...] = hn.astype(hn_ref.dtype)


def combine(x1, y2, wts, next_g, next_dtype, tm=256):
    t, d = x1.shape
    yv = y2.reshape(y2.shape[0] // TOP_K, TOP_K * d)
    row = pl.BlockSpec((tm, d), lambda i: (i, 0))
    return pl.pallas_call(
        _combine_body,
        grid=(t // tm,),
        in_specs=[row, pl.BlockSpec((tm, TOP_K * d), lambda i: (i, 0)),
                  pl.BlockSpec((tm, HEAD_DIM), lambda i: (i, 0)), pl.BlockSpec((1, d), lambda i: (0, 0))],
        out_specs=[row, row],
        out_shape=[jax.ShapeDtypeStruct((t, d), F32), jax.ShapeDtypeStruct((t, d), next_dtype)],
        compiler_params=_cparams(("parallel",)),
        name="moe_combine",
    )(x1, yv, wts, next_g.reshape(1, d))


def _layer(x, h, cos_t, sin_t, w_in, conv_a_w, a_log, dt_bias, gdn_norm_g, conv_b_w, conv_b_b, ln_b_g, ln_b_b,
           pool_w, pool_scale, w_branch, w_out, norm2_g, wg, bg, we, be, w_up, w_down, next_g, next_dtype):
    t = x.shape[0]
    w_u = jnp.concatenate([w_in[:, :W_BETA], w_in[:, W_UB:W_GATE]], axis=1).astype(BF16)
    w_bd = jnp.pad(w_in[:, W_BETA:W_UB], ((0, 0), (0, HEAD_DIM - 2 * N_HEADS))).astype(BF16)
    w_gate = w_in[:, W_GATE:].astype(BF16)
    pool_bd = jnp.zeros((WIDTH, WIDTH), F32)
    for gi in range(len(POOL_WINDOWS)):
        pool_bd = lax.dynamic_update_slice(pool_bd, pool_w[gi], (gi * POOL_GROUP, gi * POOL_GROUP))
    w_router = jnp.pad(jnp.concatenate([wg, we], axis=1), ((0, 0), (0, HEAD_DIM - N_GROUPS - N_EXPERTS)))
    b_router = jnp.pad(jnp.concatenate([bg, be]), (0, HEAD_DIM - N_GROUPS - N_EXPERTS)).reshape(1, HEAD_DIM)

    u = project(h, w_u)
    bd = project(h, w_bd)
    ya = gated_deltanet(u, bd, conv_a_w, a_log, dt_bias, gdn_norm_g)
    yb, yd = local_mixers(u, conv_b_w, conv_b_b, ln_b_g, ln_b_b, pool_bd.astype(BF16), pool_scale)
    yc = lse = None
    for gi, (_, dilation) in enumerate(ATT_PATTERNS):
        yc, lse = attention_group(u, cos_t, sin_t, gi, dilation, yc, lse)
    x1, rl = merge_project(h, ya, yb, yc, yd, w_gate, w_branch.astype(BF16), w_out.astype(BF16), x,
                               norm2_g, w_router, b_router)
    eid, wts = route(rl)
    block_e, n_used, slot_tok, slot_dst, n_rows = dispatch_tables(eid[:, :TOP_K], t)
    y2 = experts(x1, norm2_g, w_up, w_down, block_e, n_used, slot_tok, slot_dst, n_rows)
    return combine(x1, y2, wts, next_g, next_dtype)


def kernel(x, positions, norm1_g, w_in, conv_a_w, a_log, dt_bias, gdn_norm_g, conv_b_w, conv_b_b, ln_b_g, ln_b_b,
           pool_w, pool_scale, w_branch, w_out, norm2_g, router_group_w, router_group_b, router_expert_w,
           router_expert_b, w_up, w_down, final_norm_g):
    b_, s_, d = x.shape
    depth = w_in.shape[0]
    outs = []
    for bi in range(b_):
        xb = x[bi]
        cos_t, sin_t = rope_tables(positions[bi])
        h = rmsnorm(xb, norm1_g[0], BF16)
        for layer in range(depth):
            last = layer == depth - 1
            next_g = final_norm_g if last else norm1_g[layer + 1]
            xb, h = _layer(xb, h, cos_t, sin_t, w_in[layer], conv_a_w[layer], a_log[layer], dt_bias[layer],
                           gdn_norm_g[layer], conv_b_w[layer], conv_b_b[layer], ln_b_g[layer], ln_b_b[layer],
                           pool_w[layer], pool_scale[layer], w_branch[layer], w_out[layer], norm2_g[layer],
                           router_group_w[layer], router_group_b[layer], router_expert_w[layer],
                           router_expert_b[layer], w_up[layer], w_down[layer], next_g, F32 if last else BF16)
        outs.append(h)
    return jnp.stack(outs, axis=0)
```

```python
import functools

import jax
import jax.numpy as jnp
from jax import lax
from jax.experimental import pallas as pl
from jax.experimental.pallas import tpu as pltpu

F32 = jnp.float32
BF16 = jnp.bfloat16

D_MODEL = 2048
N_HEADS = 6
HEAD_DIM = 128
WIDTH = N_HEADS * HEAD_DIM
GDN_CONV = 4
GDN_CHUNK = 64
GDN_SUPER = 256
CONF_CONV = 31
ATT_PATTERNS = ((128, 1), (512, 4), (2048, 16))
ATT_BLOCK = 128
ATT_HEADS_PER_STEP = {1: 6, 4: 1, 16: 1}
ROPE_THETA = 10000.0
POOL_WINDOWS = (2, 4, 8, 16)
POOL_GROUP = 192
N_BRANCHES = 4
N_GROUPS = 8
PER_GROUP = 8
N_EXPERTS = 64
TOP_K = 2
EXPERT_HIDDEN = 512
MOE_BLOCK = 128
MOE_SPARE = 2 * MOE_BLOCK
LOCAL_ROWS = 32
NORM_EPS = 1e-6
LN_EPS = 1e-5

U_QKV_A = 0
U_Z_A = 2304
U_B = 3072
U_C = 4608
U_D = 11520
U_WIDTH = 12288
W_BETA = 3072
W_UB = 3084
W_GATE = 12300

VMEM_LIMIT = 56 * 1024 * 1024


def _cparams(semantics, vmem=VMEM_LIMIT):
    return pltpu.CompilerParams(dimension_semantics=semantics, vmem_limit_bytes=vmem)


def _sigmoid(x):
    return 1.0 / (1.0 + jnp.exp(-x))


def _silu(x):
    return x * _sigmoid(x)


def _dot(a, b):
    return jnp.dot(a, b, preferred_element_type=F32)


def _dot_nt(a, b):
    return lax.dot_general(a, b, (((1,), (1,)), ((), ())), preferred_element_type=F32)


def _dot_tn(a, b):
    return lax.dot_general(a, b, (((0,), (0,)), ((), ())), preferred_element_type=F32)


def _rmsnorm_body(x_ref, g_ref, o_ref):
    x = x_ref[...]
    y = x * lax.rsqrt(jnp.mean(x * x, axis=-1, keepdims=True) + NORM_EPS) * g_ref[...]
    o_ref[...] = y.astype(o_ref.dtype)


def rmsnorm(x, g, out_dtype, tm=512):
    t, d = x.shape
    return pl.pallas_call(
        _rmsnorm_body,
        grid=(t // tm,),
        in_specs=[pl.BlockSpec((tm, d), lambda i: (i, 0)), pl.BlockSpec((1, d), lambda i: (0, 0))],
        out_specs=pl.BlockSpec((tm, d), lambda i: (i, 0)),
        out_shape=jax.ShapeDtypeStruct((t, d), out_dtype),
        compiler_params=_cparams(("parallel",)),
        name="rmsnorm",
    )(x, g.reshape(1, d))


def _wprep_body(a_ref, b_ref, o_ref, *, n_plain, shift, rows):
    j = pl.program_id(1)
    tn = o_ref.shape[1]

    @pl.when(j < n_plain)
    def _():
        o_ref[...] = a_ref[...].astype(BF16)

    @pl.when(j >= n_plain)
    def _():
        for r0 in range(0, a_ref.shape[0], rows):
            w = jnp.concatenate([a_ref[r0:r0 + rows, :], b_ref[r0:r0 + rows, :]], axis=1)
            o_ref[r0:r0 + rows, :] = w[:, shift:shift + tn].astype(BF16)


def prepare_w_in(w_in, tn=1024):
    depth, k, _ = w_in.shape
    n_out = W_BETA + (w_in.shape[2] - W_UB)
    n_plain = W_BETA // tn
    lanes = HEAD_DIM
    return pl.pallas_call(
        functools.partial(_wprep_body, n_plain=n_plain, shift=W_UB - W_BETA, rows=256),
        grid=(depth, n_out // tn),
        in_specs=[pl.BlockSpec((None, k, tn), lambda l, j: (l, 0, j)),
                  pl.BlockSpec((None, k, lanes), lambda l, j: (l, 0, (j + 1) * (tn // lanes)))],
        out_specs=pl.BlockSpec((None, k, tn), lambda l, j: (l, 0, j)),
        out_shape=jax.ShapeDtypeStruct((depth, k, n_out), BF16),
        compiler_params=_cparams(("parallel", "parallel")),
        name="prepare_w_in",
    )(w_in, w_in)


def _matmul_body(h_ref, w_ref, o_ref):
    o_ref[...] = _dot(h_ref[...], w_ref[...].astype(BF16)).astype(o_ref.dtype)


def project(h, w, layer, n, col0=0, tm=1024, tn=768):
    t, k = h.shape
    tn = min(tn, n)
    return pl.pallas_call(
        _matmul_body,
        grid=(n // tn, t // tm),
        in_specs=[pl.BlockSpec((tm, k), lambda j, i: (i, 0)),
                  pl.BlockSpec((None, k, tn), lambda j, i: (layer, 0, col0 // tn + j))],
        out_specs=pl.BlockSpec((tm, tn), lambda j, i: (i, j)),
        out_shape=jax.ShapeDtypeStruct((t, n), F32),
        compiler_params=_cparams(("parallel", "parallel")),
        name="in_proj",
    )(h, w)


def _rope_body(pos_ref, inv_ref, cos_ref, sin_ref):
    ang = pos_ref[...].astype(F32) * inv_ref[...]
    lane = lax.broadcasted_iota(jnp.int32, ang.shape, 1)
    cos_ref[...] = jnp.cos(ang)
    s = jnp.sin(ang)
    sin_ref[...] = jnp.where(lane < HEAD_DIM // 2, -s, s)


def rope_tables(pos, tm=1024):
    t = pos.shape[0]
    half = HEAD_DIM // 2
    inv = ROPE_THETA ** (-jnp.arange(half, dtype=F32) * 2.0 / HEAD_DIM)
    inv = jnp.concatenate([inv, inv]).reshape(1, HEAD_DIM)
    spec = pl.BlockSpec((tm, HEAD_DIM), lambda i: (i, 0))
    return pl.pallas_call(
        _rope_body,
        grid=(t // tm,),
        in_specs=[pl.BlockSpec((tm, 1), lambda i: (i, 0)), pl.BlockSpec((1, HEAD_DIM), lambda i: (0, 0))],
        out_specs=[spec, spec],
        out_shape=[jax.ShapeDtypeStruct((t, HEAD_DIM), F32)] * 2,
        compiler_params=_cparams(("parallel",)),
        name="rope_tables",
    )(pos.reshape(t, 1), inv)


def _rope(x, cos, sin):
    return x * cos + pltpu.roll(x, HEAD_DIM // 2, axis=1) * sin


def _attn_body(*refs, first, d, hw):
    if first:
        q_ref, kc_ref, kp_ref, vc_ref, vp_ref, cc_ref, sc_ref, cp_ref, sp_ref, o_out, lse_out = refs
        o_in = lse_in = None
    else:
        (q_ref, kc_ref, kp_ref, vc_ref, vp_ref, cc_ref, sc_ref, cp_ref, sp_ref,
         o_in, lse_in, o_out, lse_out) = refs
    n = pl.program_id(0)
    blk = ATT_BLOCK
    row = lax.broadcasted_iota(jnp.int32, (blk, blk), 0)
    col = lax.broadcasted_iota(jnp.int32, (blk, blk), 1)
    mask_cur = col <= row
    mask_prev = (col >= row) & (n > 0)
    scale = HEAD_DIM ** -0.5
    neg = -jnp.inf
    for r in range(d):
        rows = pl.ds(r, blk, stride=d) if d > 1 else pl.ds(0, blk)
        cosc, sinc, cosp, sinp = cc_ref[rows, :], sc_ref[rows, :], cp_ref[rows, :], sp_ref[rows, :]
        for h in range(hw):
            sl = slice(h * HEAD_DIM, (h + 1) * HEAD_DIM)
            q = (_rope(q_ref[rows, sl], cosc, sinc) * scale).astype(BF16)
            kc = _rope(kc_ref[rows, sl], cosc, sinc).astype(BF16)
            kp = _rope(kp_ref[rows, sl], cosp, sinp).astype(BF16)
            s_c = jnp.where(mask_cur, _dot_nt(q, kc), neg)
            s_p = jnp.where(mask_prev, _dot_nt(q, kp), neg)
            m = jnp.maximum(jnp.max(s_c, axis=-1, keepdims=True), jnp.max(s_p, axis=-1, keepdims=True))
            if not first:
                m_old = lse_in[rows, sl][:, 0:1]
                m = jnp.maximum(m, m_old)
            p_c = jnp.exp(s_c - m)
            p_p = jnp.exp(s_p - m)
            l = jnp.sum(p_c, axis=-1, keepdims=True) + jnp.sum(p_p, axis=-1, keepdims=True)
            acc = (_dot(p_c.astype(BF16), vc_ref[rows, sl].astype(BF16))
                   + _dot(p_p.astype(BF16), vp_ref[rows, sl].astype(BF16)))
            if not first:
                w_old = jnp.exp(m_old - m)
                l = l + w_old
                acc = acc + w_old * o_in[rows, sl]
            o_out[rows, sl] = acc / l
            lse_out[rows, sl] = jnp.broadcast_to(m + jnp.log(l), (blk, HEAD_DIM))


def attention_group(u, cos_t, sin_t, group, dilation, o_prev, lse_prev):
    t = u.shape[0]
    d = dilation
    span = ATT_BLOCK * d
    hw = ATT_HEADS_PER_STEP[d]
    cw = hw * HEAD_DIM
    first = o_prev is None
    qoff = (U_C + group * WIDTH) // cw
    koff = (U_C + 3 * WIDTH + group * WIDTH) // cw
    voff = (U_C + 6 * WIDTH + group * WIDTH) // cw

    def cur(off):
        return pl.BlockSpec((span, cw), lambda n, g: (n, off + g))

    def prev(off):
        return pl.BlockSpec((span, cw), lambda n, g: (jnp.maximum(n - 1, 0), off + g))

    tab_cur = pl.BlockSpec((span, HEAD_DIM), lambda n, g: (n, 0))
    tab_prev = pl.BlockSpec((span, HEAD_DIM), lambda n, g: (jnp.maximum(n - 1, 0), 0))
    o_spec = cur(0)
    in_specs = [cur(qoff), cur(koff), prev(koff), cur(voff), prev(voff), tab_cur, tab_cur, tab_prev, tab_prev]
    args = [u, u, u, u, u, cos_t, sin_t, cos_t, sin_t]
    if not first:
        in_specs += [o_spec, o_spec]
        args += [o_prev, lse_prev]
    return pl.pallas_call(
        functools.partial(_attn_body, first=first, d=d, hw=hw),
        grid=(t // span, N_HEADS // hw),
        in_specs=in_specs,
        out_specs=[o_spec, o_spec],
        out_shape=[jax.ShapeDtypeStruct((t, WIDTH), F32), jax.ShapeDtypeStruct((t, WIDTH), F32)],
        compiler_params=_cparams(("parallel", "parallel")),
        name=f"attn_d{d}",
    )(*args)


def _local_body(ub_ref, ud_ref, cw_ref, cb_ref, lg_ref, lb_ref, pw_ref, ps_ref, yb_ref, yd_ref, extb, extd, pool_buf, *, tm):
    i = pl.program_id(0)
    hb, hd = 32, 16

    @pl.when(i == 0)
    def _():
        extb[0:hb, :] = jnp.zeros((hb, WIDTH), F32)
        extd[0:hd, :] = jnp.zeros((hd, WIDTH), F32)

    @pl.when(i > 0)
    def _():
        extb[0:hb, :] = extb[tm:tm + hb, :]
        extd[0:hd, :] = extd[tm:tm + hd, :]

    ub = ub_ref[...]
    extb[hb:hb + tm, :] = ub[:, :WIDTH] * _sigmoid(ub[:, WIDTH:])
    extd[hd:hd + tm, :] = ud_ref[...]

    rc = LOCAL_ROWS
    ch = lax.broadcasted_iota(jnp.int32, (rc, WIDTH), 1)
    rows = lax.broadcasted_iota(jnp.int32, (rc, WIDTH), 0)
    for c0 in range(0, tm, rc):
        acc = jnp.zeros((rc, WIDTH), F32)
        for j in range(CONF_CONV):
            off = c0 + hb - (CONF_CONV - 1) + j
            acc = acc + cw_ref[j:j + 1, :] * extb[off:off + rc, :]
        cb = acc + cb_ref[...]
        mu = jnp.mean(cb, axis=-1, keepdims=True)
        cen = cb - mu
        var = jnp.mean(cen * cen, axis=-1, keepdims=True)
        yb = _silu(cen * lax.rsqrt(var + LN_EPS) * lg_ref[...] + lb_ref[...])
        yb_ref[c0:c0 + rc, :] = yb.astype(yb_ref.dtype)

        xd = extd[c0 + hd:c0 + hd + rc, :]
        tpos = i * tm + c0 + rows + 1
        run = xd
        pooled = None
        shift = 1
        for gi, w in enumerate(POOL_WINDOWS):
            while shift < w:
                run = run + extd[c0 + hd - shift:c0 + hd - shift + rc, :]
                shift += 1
            val = run / jnp.minimum(tpos, w).astype(F32) - xd
            pooled = val if pooled is None else jnp.where(ch >= gi * POOL_GROUP, val, pooled)
        pool_buf[c0:c0 + rc, :] = pooled.astype(pool_buf.dtype)
    yd = _dot(pool_buf[...], pw_ref[...]) * ps_ref[...]
    yd_ref[...] = yd.astype(yd_ref.dtype)


def local_mixers(u, conv_w, conv_b, ln_g, ln_b, pool_w_bd, pool_scale, tm=256):
    t = u.shape[0]
    row = lambda a: a.reshape(1, WIDTH)
    vec = pl.BlockSpec((1, WIDTH), lambda i: (0, 0))
    out = pl.BlockSpec((tm, WIDTH), lambda i: (i, 0))
    return pl.pallas_call(
        functools.partial(_local_body, tm=tm),
        grid=(t // tm,),
        in_specs=[pl.BlockSpec((tm, 2 * WIDTH), lambda i: (i, U_B // (2 * WIDTH))),
                  pl.BlockSpec((tm, WIDTH), lambda i: (i, U_D // WIDTH)),
                  pl.BlockSpec((CONF_CONV, WIDTH), lambda i: (0, 0)), vec, vec, vec,
                  pl.BlockSpec((WIDTH, WIDTH), lambda i: (0, 0)), vec],
        out_specs=[out, out],
        out_shape=[jax.ShapeDtypeStruct((t, WIDTH), BF16)] * 2,
        scratch_shapes=[pltpu.VMEM((tm + 32, WIDTH), F32), pltpu.VMEM((tm + 16, WIDTH), F32),
                        pltpu.VMEM((tm, WIDTH), BF16)],
        compiler_params=_cparams(("arbitrary",)),
        name="conv_pool",
    )(u, u, conv_w, row(conv_b), row(ln_g), row(ln_b), pool_w_bd, row(pool_scale))


def _gdn_body(qkv_ref, z_ref, bd_ref, cw_ref, alog_ref, dtb_ref, gn_ref, ya_ref, ext, state):
    s = pl.program_id(0)
    sup, c = GDN_SUPER, GDN_CHUNK
    halo = 8

    @pl.when(s == 0)
    def _():
        ext[0:halo, :] = jnp.zeros((halo, 3 * WIDTH), F32)
        state[...] = jnp.zeros(state.shape, F32)

    @pl.when(s > 0)
    def _():
        ext[0:halo, :] = ext[sup:sup + halo, :]

    ext[halo:halo + sup, :] = qkv_ref[...]
    acc = jnp.zeros((sup, 3 * WIDTH), F32)
    for j in range(GDN_CONV):
        off = halo - (GDN_CONV - 1) + j
        acc = acc + cw_ref[j:j + 1, :] * ext[off:off + sup, :]
    qkv = _silu(acc)

    bd = bd_ref[...]
    beta_all = _sigmoid(bd)
    xs = bd + dtb_ref[...]
    softplus = jnp.maximum(xs, 0.0) + jnp.log(1.0 + jnp.exp(-jnp.abs(xs)))
    g_all = -jnp.exp(alog_ref[...]) * softplus
    rowi = lax.broadcasted_iota(jnp.int32, (sup, HEAD_DIM), 0)
    in_chunk = rowi & (c - 1)
    gc = g_all
    sh = 1
    while sh < c:
        gc = gc + jnp.where(in_chunk >= sh, pltpu.roll(gc, sh, axis=0), 0.0)
        sh *= 2
    gct = gc.T

    ri = lax.broadcasted_iota(jnp.int32, (sup, sup), 0)
    ci = lax.broadcasted_iota(jnp.int32, (sup, sup), 1)
    same = (ri & -c) == (ci & -c)
    incl = same & (ri >= ci)
    strict = same & (ri > ci)
    eye = (ri == ci).astype(F32)

    for h in range(N_HEADS):
        sl = slice(h * HEAD_DIM, (h + 1) * HEAD_DIM)
        q = qkv[:, sl]
        k = qkv[:, WIDTH + h * HEAD_DIM:WIDTH + (h + 1) * HEAD_DIM]
        v = qkv[:, 2 * WIDTH + h * HEAD_DIM:2 * WIDTH + (h + 1) * HEAD_DIM]
        q = q * lax.rsqrt(jnp.sum(q * q, axis=-1, keepdims=True) + 1e-6) * (HEAD_DIM ** -0.5)
        k = k * lax.rsqrt(jnp.sum(k * k, axis=-1, keepdims=True) + 1e-6)
        b_col = beta_all[:, h:h + 1]
        gc_col = gc[:, N_HEADS + h:N_HEADS + h + 1]
        gc_row = gct[N_HEADS + h:N_HEADS + h + 1, :]
        dm = jnp.exp(jnp.where(incl, gc_col - gc_row, -jnp.inf))
        kb = k.astype(BF16)
        kk = _dot_nt(kb, kb)
        nmat = jnp.where(strict, -(b_col * kk * dm), 0.0)
        pmat = eye + nmat
        mb = nmat.astype(BF16)
        step = 2
        while step < c:
            m2 = _dot(mb, mb).astype(BF16)
            pmat = pmat + _dot(pmat.astype(BF16), m2)
            mb = m2
            step *= 2
        eg = jnp.exp(gc_col)
        rhs = jnp.concatenate([v * b_col, k * (b_col * eg)], axis=1).astype(BF16)
        sol = _dot(pmat.astype(BF16), rhs)
        u_c, w_c = sol[:, :HEAD_DIM], sol[:, HEAD_DIM:]
        qk = _dot_nt(q.astype(BF16), kb) * dm
        q_dec = (q * eg).astype(BF16)
        gl_col = jnp.concatenate(
            [jnp.broadcast_to(gc_col[(n + 1) * c - 1:(n + 1) * c, :], (c, 1)) for n in range(sup // c)], axis=0)
        k_dec = k * jnp.exp(gl_col - gc_col)
        w_b = w_c.astype(BF16)
        qk_b = qk.astype(BF16)
        st = state[h]
        outs = []
        for n in range(sup // c):
            rs = slice(n * c, (n + 1) * c)
            sb = st.astype(BF16)
            v_new = u_c[rs] - _dot(w_b[rs], sb)
            vb = v_new.astype(BF16)
            outs.append(_dot(q_dec[rs], sb) + _dot(qk_b[rs, rs], vb))
            g_last = jnp.exp(gc_col[(n + 1) * c - 1:(n + 1) * c, :])
            st = st * g_last + _dot(k_dec[rs].T.astype(BF16), vb)
        state[h] = st
        o = jnp.concatenate(outs, axis=0)
        o = o * lax.rsqrt(jnp.mean(o * o, axis=-1, keepdims=True) + NORM_EPS) * gn_ref[...]
        ya_ref[:, sl] = (o * _silu(z_ref[:, sl])).astype(ya_ref.dtype)


def gated_deltanet(u, bd, conv_w, a_log, dt_bias, norm_g):
    t = u.shape[0]
    sup = GDN_SUPER
    pad = jnp.zeros((N_HEADS,), F32)
    lane_row = lambda a: jnp.concatenate([pad, a, jnp.zeros((HEAD_DIM - 2 * N_HEADS,), F32)]).reshape(1, HEAD_DIM)
    vec = pl.BlockSpec((1, HEAD_DIM), lambda s: (0, 0))
    return pl.pallas_call(
        _gdn_body,
        grid=(t // sup,),
        in_specs=[pl.BlockSpec((sup, 3 * WIDTH), lambda s: (s, U_QKV_A // (3 * WIDTH))),
                  pl.BlockSpec((sup, WIDTH), lambda s: (s, U_Z_A // WIDTH)),
                  pl.BlockSpec((sup, HEAD_DIM), lambda s: (s, 0)),
                  pl.BlockSpec((GDN_CONV, 3 * WIDTH), lambda s: (0, 0)), vec, vec, vec],
        out_specs=pl.BlockSpec((sup, WIDTH), lambda s: (s, 0)),
        out_shape=jax.ShapeDtypeStruct((t, WIDTH), BF16),
        scratch_shapes=[pltpu.VMEM((sup + 8, 3 * WIDTH), F32), pltpu.VMEM((N_HEADS, HEAD_DIM, HEAD_DIM), F32)],
        compiler_params=_cparams(("arbitrary",)),
        name="gated_deltanet",
    )(u, u, bd, conv_w, lane_row(a_log), lane_row(dt_bias), norm_g.reshape(1, HEAD_DIM))


def _merge_body(h_ref, ya_ref, yb_ref, yc_ref, yd_ref, g0_ref, g1_ref, g2_ref, g3_ref,
                b0_ref, b1_ref, b2_ref, b3_ref, wo_ref, x_ref, n2_ref, wr_ref, br_ref,
                x1_ref, rl_ref, acc):
    j = pl.program_id(1)

    @pl.when(j == 0)
    def _():
        acc[...] = jnp.zeros(acc.shape, F32)

    hb = h_ref[...]
    merged = None
    for y_ref, g_ref, b_ref in ((ya_ref, g0_ref, b0_ref), (yb_ref, g1_ref, b1_ref),
                                (yc_ref, g2_ref, b2_ref), (yd_ref, g3_ref, b3_ref)):
        gate = _sigmoid(_dot(hb, g_ref[...]))
        term = gate * _dot(y_ref[...].astype(BF16), b_ref[...])
        merged = term if merged is None else merged + term
    acc[...] += _dot(merged.astype(BF16), wo_ref[...])

    @pl.when(j == pl.num_programs(1) - 1)
    def _():
        x1 = x_ref[...] + acc[...]
        x1_ref[...] = x1
        h2 = x1 * lax.rsqrt(jnp.mean(x1 * x1, axis=-1, keepdims=True) + NORM_EPS) * n2_ref[...]
        rl_ref[...] = jnp.dot(h2, wr_ref[...], preferred_element_type=F32,
                              precision=lax.Precision.HIGHEST) + br_ref[...]


def merge_project(h, ya, yb, yc, yd, w_all, w_branch, w_out, layer, x, norm2_g, w_router, b_router, tm=512, tn=256):
    t = h.shape[0]
    d = D_MODEL
    nj = d // tn
    g0 = U_WIDTH // tn
    row = pl.BlockSpec((tm, d), lambda i, j: (i, 0))
    ysp = pl.BlockSpec((tm, WIDTH), lambda i, j: (i, 0))
    gate_spec = lambda k: pl.BlockSpec((None, d, tn), lambda i, j: (layer, 0, g0 + k * nj + j))
    br_spec = lambda k: pl.BlockSpec((None, None, WIDTH, tn), lambda i, j: (layer, k, 0, j))
    lanes = pl.BlockSpec((tm, HEAD_DIM), lambda i, j: (i, 0))
    w_gate = w_all
    return pl.pallas_call(
        _merge_body,
        grid=(t // tm, nj),
        in_specs=[row, ysp, ysp, ysp, ysp] + [gate_spec(k) for k in range(4)] + [br_spec(k) for k in range(4)]
        + [pl.BlockSpec((None, tn, d), lambda i, j: (layer, j, 0)), row, pl.BlockSpec((1, d), lambda i, j: (0, 0)),
           pl.BlockSpec((d, HEAD_DIM), lambda i, j: (0, 0)), pl.BlockSpec((1, HEAD_DIM), lambda i, j: (0, 0))],
        out_specs=[row, lanes],
        out_shape=[jax.ShapeDtypeStruct((t, d), F32), jax.ShapeDtypeStruct((t, HEAD_DIM), F32)],
        scratch_shapes=[pltpu.VMEM((tm, d), F32)],
        compiler_params=_cparams(("parallel", "arbitrary")),
        name="merge_project",
    )(h, ya, yb, yc, yd, w_gate, w_gate, w_gate, w_gate, w_branch, w_branch, w_branch, w_branch,
      w_out, x, norm2_g.reshape(1, d), w_router, b_router)


def _router_body(rl_ref, eid_ref, wts_ref):
    rl = rl_ref[...]
    lane_i = lax.broadcasted_iota(jnp.int32, rl.shape, 1)
    lane = lane_i.astype(F32)
    neg = -jnp.inf
    big = 1e9
    gl = jnp.where(lane < N_GROUPS, rl, neg)
    gmax = jnp.max(gl, axis=-1, keepdims=True)
    gi = jnp.min(jnp.where(gl == gmax, lane, big), axis=-1, keepdims=True)
    gp = 1.0 / jnp.sum(jnp.exp(gl - gmax), axis=-1, keepdims=True)
    lo = N_GROUPS + gi * PER_GROUP
    el = jnp.where((lane >= lo) & (lane < lo + PER_GROUP), rl, neg)
    e1 = jnp.max(el, axis=-1, keepdims=True)
    i1 = jnp.min(jnp.where(el == e1, lane, big), axis=-1, keepdims=True)
    el2 = jnp.where(lane == i1, neg, el)
    e2 = jnp.max(el2, axis=-1, keepdims=True)
    i2 = jnp.min(jnp.where(el2 == e2, lane, big), axis=-1, keepdims=True)
    esum = jnp.sum(jnp.exp(el - e1), axis=-1, keepdims=True)
    p1 = 1.0 / esum
    p2 = jnp.exp(e2 - e1) / esum
    w1 = gp * p1 / (p1 + p2)
    w2 = gp * p2 / (p1 + p2)
    eid = jnp.where(lane_i == 0, i1 - N_GROUPS, jnp.where(lane_i == 1, i2 - N_GROUPS, 0.0))
    eid_ref[...] = eid.astype(jnp.int32)
    wts_ref[...] = jnp.where(lane_i == 0, w1, jnp.where(lane_i == 1, w2, 0.0))


def route(rl, tm=512):
    t = rl.shape[0]
    spec = pl.BlockSpec((tm, HEAD_DIM), lambda i: (i, 0))
    return pl.pallas_call(
        _router_body,
        grid=(t // tm,),
        in_specs=[spec],
        out_specs=[spec, spec],
        out_shape=[jax.ShapeDtypeStruct((t, HEAD_DIM), jnp.int32), jax.ShapeDtypeStruct((t, HEAD_DIM), F32)],
        compiler_params=_cparams(("parallel",)),
        name="router",
    )(rl)


def _gather_copy(h_hbm, tok_ref, xbuf, gsem, blk, slot, r):
    tok = tok_ref[blk * MOE_BLOCK + r]
    return pltpu.make_async_copy(h_hbm.at[pl.ds(tok, 1), :], xbuf.at[slot, pl.ds(r, 1), :], gsem.at[slot])


def _scatter_copy(y_hbm, dst_ref, obuf, ssem, blk, slot, r):
    dst = dst_ref[blk * MOE_BLOCK + r]
    return pltpu.make_async_copy(obuf.at[slot, pl.ds(r, 1), :], y_hbm.at[pl.ds(dst, 1), :], ssem.at[slot])


def _expert_body(be_ref, nused_ref, tok_ref, dst_ref, h_hbm, g_ref, wup_ref, wdn_ref, y_hbm,
                 xbuf, obuf, wup_b, wdn_b, gsem, ssem):
    b = pl.program_id(0)
    n_used = nused_ref[0]
    slot = b % 2
    other = 1 - slot

    @pl.when(b == 0)
    def _():
        for r in range(MOE_BLOCK):
            _gather_copy(h_hbm, tok_ref, xbuf, gsem, b, slot, r).start()
        obuf[...] = jnp.zeros(obuf.shape, F32)
        half = y_hbm.shape[0] // TOP_K
        for q in range(2 * TOP_K):
            row0 = (q // 2) * half + (half - MOE_SPARE) + (q % 2) * MOE_BLOCK
            spare = pltpu.make_async_copy(obuf.at[q % 2], y_hbm.at[pl.ds(row0, MOE_BLOCK), :], ssem.at[q % 2])
            spare.start()
            spare.wait()

    @pl.when(b < n_used)
    def _():
        for r in range(MOE_BLOCK):
            _gather_copy(h_hbm, tok_ref, xbuf, gsem, b, slot, r).wait()

        @pl.when(b + 1 < n_used)
        def _():
            for r in range(MOE_BLOCK):
                _gather_copy(h_hbm, tok_ref, xbuf, gsem, b + 1, other, r).start()

        changed = jnp.logical_or(b == 0, be_ref[b] != be_ref[jnp.maximum(b - 1, 0)])

        @pl.when(changed)
        def _():
            wup_b[...] = wup_ref[...].astype(BF16)
            wdn_b[...] = wdn_ref[...].astype(BF16)

        xr = xbuf[slot]
        xn = xr * lax.rsqrt(jnp.mean(xr * xr, axis=-1, keepdims=True) + NORM_EPS) * g_ref[...]
        gu = _dot(xn.astype(BF16), wup_b[...])
        act = (_silu(gu[:, :EXPERT_HIDDEN]) * gu[:, EXPERT_HIDDEN:]).astype(BF16)
        y = _dot(act, wdn_b[...])

        @pl.when(b >= 2)
        def _():
            for r in range(MOE_BLOCK):
                _scatter_copy(y_hbm, dst_ref, obuf, ssem, b - 2, slot, r).wait()

        obuf[slot] = y
        for r in range(MOE_BLOCK):
            _scatter_copy(y_hbm, dst_ref, obuf, ssem, b, slot, r).start()

        @pl.when(b == n_used - 1)
        def _():
            @pl.when(b >= 1)
            def _():
                for r in range(MOE_BLOCK):
                    _scatter_copy(y_hbm, dst_ref, obuf, ssem, b - 1, other, r).wait()

            for r in range(MOE_BLOCK):
                _scatter_copy(y_hbm, dst_ref, obuf, ssem, b, slot, r).wait()


def experts(x1, norm_g, w_up, w_down, layer, block_e, n_used, slot_tok, slot_dst, n_rows):
    n_blocks = block_e.shape[0]
    d = D_MODEL
    grid_spec = pltpu.PrefetchScalarGridSpec(
        num_scalar_prefetch=4,
        grid=(n_blocks,),
        in_specs=[pl.BlockSpec(memory_space=pl.ANY),
                  pl.BlockSpec((1, d), lambda b, be, nu, tok, dst: (0, 0)),
                  pl.BlockSpec((None, None, d, 2 * EXPERT_HIDDEN), lambda b, be, nu, tok, dst: (layer, be[b], 0, 0)),
                  pl.BlockSpec((None, None, EXPERT_HIDDEN, d), lambda b, be, nu, tok, dst: (layer, be[b], 0, 0))],
        out_specs=pl.BlockSpec(memory_space=pl.ANY),
        scratch_shapes=[pltpu.VMEM((2, MOE_BLOCK, d), F32), pltpu.VMEM((2, MOE_BLOCK, d), F32),
                        pltpu.VMEM((d, 2 * EXPERT_HIDDEN), BF16), pltpu.VMEM((EXPERT_HIDDEN, d), BF16),
                        pltpu.SemaphoreType.DMA((2,)), pltpu.SemaphoreType.DMA((2,))],
    )
    return pl.pallas_call(
        _expert_body,
        grid_spec=grid_spec,
        out_shape=jax.ShapeDtypeStruct((n_rows, d), F32),
        compiler_params=_cparams(("arbitrary",)),
        name="experts",
    )(block_e, n_used, slot_tok, slot_dst, x1, norm_g.reshape(1, d), w_up, w_down)


def dispatch_tables(eid, t):
    n_assign = t * TOP_K
    flat_e = eid.reshape(n_assign)
    onehot = (flat_e[:, None] == jnp.arange(N_EXPERTS, dtype=jnp.int32)[None, :]).astype(jnp.int32)
    csum = jnp.cumsum(onehot, axis=0)
    counts = csum[-1]
    rank = jnp.sum(csum * onehot, axis=1) - 1
    padded = (counts + MOE_BLOCK - 1) // MOE_BLOCK * MOE_BLOCK
    pad_end = jnp.cumsum(padded)
    pad_start = pad_end - padded
    dest = pad_start[flat_e] + rank
    n_blocks = -(-(n_assign + N_EXPERTS * (MOE_BLOCK - 1)) // MOE_BLOCK)
    n_slots = n_blocks * MOE_BLOCK
    flat_a = jnp.arange(n_assign, dtype=jnp.int32)
    slot_tok = jnp.zeros((n_slots,), jnp.int32).at[dest].set(flat_a // TOP_K)
    half = t + MOE_SPARE
    sp = jnp.arange(n_slots, dtype=jnp.int32) % (TOP_K * MOE_SPARE)
    spare = (sp // MOE_SPARE) * half + t + sp % MOE_SPARE
    slot_dst = spare.at[dest].set((flat_a % TOP_K) * half + flat_a // TOP_K)
    n_used = pad_end[-1] // MOE_BLOCK
    starts = jnp.arange(n_blocks, dtype=jnp.int32) * MOE_BLOCK
    block_e = jnp.minimum(jnp.searchsorted(pad_end, starts, side="right"), N_EXPERTS - 1).astype(jnp.int32)
    last_e = block_e[jnp.maximum(n_used - 1, 0)]
    block_e = jnp.where(jnp.arange(n_blocks) < n_used, block_e, last_e)
    return block_e, n_used.reshape(1).astype(jnp.int32), slot_tok, slot_dst, TOP_K * half


def _combine_body(x_ref, y0_ref, y1_ref, w_ref, g_ref, x2_ref, hn_ref):
    w = w_ref[...]
    x2 = x_ref[...] + (w[:, 0:1] * y0_ref[...] + w[:, 1:2] * y1_ref[...])
    x2_ref[...] = x2
    hn = x2 * lax.rsqrt(jnp.mean(x2 * x2, axis=-1, keepdims=True) + NORM_EPS) * g_ref[...]
    hn_ref[...] = hn.astype(hn_ref.dtype)


def combine(x1, y2, wts, next_g, next_dtype, tm=MOE_SPARE):
    t, d = x1.shape
    half_blocks = y2.shape[0] // TOP_K // tm
    row = pl.BlockSpec((tm, d), lambda i: (i, 0))
    return pl.pallas_call(
        _combine_body,
        grid=(t // tm,),
        in_specs=[row, row, pl.BlockSpec((tm, d), lambda i: (i + half_blocks, 0)),
                  pl.BlockSpec((tm, HEAD_DIM), lambda i: (i, 0)), pl.BlockSpec((1, d), lambda i: (0, 0))],
        out_specs=[row, row],
        out_shape=[jax.ShapeDtypeStruct((t, d), F32), jax.ShapeDtypeStruct((t, d), next_dtype)],
        compiler_params=_cparams(("parallel",)),
        name="moe_combine",
    )(x1, y2, y2, wts, next_g.reshape(1, d))


def _layer(x, h, cos_t, sin_t, layer, w_in, w_all, conv_a_w, a_log, dt_bias, gdn_norm_g, conv_b_w, conv_b_b,
           ln_b_g, ln_b_b, pool_w, pool_scale, w_branch, w_out, norm2_g, wg, bg, we, be, w_up, w_down,
           next_g, next_dtype):
    t = x.shape[0]
    pool_bd = jnp.zeros((WIDTH, WIDTH), F32)
    for gi in range(len(POOL_WINDOWS)):
        pool_bd = lax.dynamic_update_slice(pool_bd, pool_w[gi], (gi * POOL_GROUP, gi * POOL_GROUP))
    w_router = jnp.pad(jnp.concatenate([wg, we], axis=1), ((0, 0), (0, HEAD_DIM - N_GROUPS - N_EXPERTS)))
    b_router = jnp.pad(jnp.concatenate([bg, be]), (0, HEAD_DIM - N_GROUPS - N_EXPERTS)).reshape(1, HEAD_DIM)

    u = project(h, w_all, layer, U_WIDTH)
    bd = project(h, w_in, layer, HEAD_DIM, col0=W_BETA)
    ya = gated_deltanet(u, bd, conv_a_w, a_log, dt_bias, gdn_norm_g)
    yb, yd = local_mixers(u, conv_b_w, conv_b_b, ln_b_g, ln_b_b, pool_bd.astype(BF16), pool_scale)
    yc = lse = None
    for gi, (_, dilation) in enumerate(ATT_PATTERNS):
        yc, lse = attention_group(u, cos_t, sin_t, gi, dilation, yc, lse)
    x1, rl = merge_project(h, ya, yb, yc, yd, w_all, w_branch, w_out, layer, x, norm2_g, w_router, b_router)
    eid, wts = route(rl)
    block_e, n_used, slot_tok, slot_dst, n_rows = dispatch_tables(eid[:, :TOP_K], t)
    y2 = experts(x1, norm2_g, w_up, w_down, layer, block_e, n_used, slot_tok, slot_dst, n_rows)
    return combine(x1, y2, wts, next_g, next_dtype)


def kernel(x, positions, norm1_g, w_in, conv_a_w, a_log, dt_bias, gdn_norm_g, conv_b_w, conv_b_b, ln_b_g, ln_b_b,
           pool_w, pool_scale, w_branch, w_out, norm2_g, router_group_w, router_group_b, router_expert_w,
           router_expert_b, w_up, w_down, final_norm_g):
    b_, s_, d = x.shape
    depth = w_in.shape[0]
    outs = []
    w_all = prepare_w_in(w_in)
    w_branch_b = w_branch.astype(BF16)
    w_out_b = w_out.astype(BF16)
    for bi in range(b_):
        xb = x[bi]
        cos_t, sin_t = rope_tables(positions[bi])
        h = rmsnorm(xb, norm1_g[0], BF16)
        for layer in range(depth):
            last = layer == depth - 1
            next_g = final_norm_g if last else norm1_g[layer + 1]
            xb, h = _layer(xb, h, cos_t, sin_t, layer, w_in, w_all, conv_a_w[layer], a_log[layer], dt_bias[layer],
                           gdn_norm_g[layer], conv_b_w[layer], conv_b_b[layer], ln_b_g[layer], ln_b_b[layer],
                           pool_w[layer], pool_scale[layer], w_branch_b, w_out_b, norm2_g[layer],
                           router_group_w[layer], router_group_b[layer], router_expert_w[layer],
                           router_expert_b[layer], w_up, w_down, next_g, F32 if last else BF16)
        outs.append(h)
    return jnp.stack(outs, axis=0)
```

```python
import functools

import jax
import jax.numpy as jnp
from jax import lax
from jax.experimental import pallas as pl
from jax.experimental.pallas import tpu as pltpu

F32 = jnp.float32
BF16 = jnp.bfloat16

D_MODEL = 2048
N_HEADS = 6
HEAD_DIM = 128
WIDTH = N_HEADS * HEAD_DIM
GDN_CONV = 4
GDN_CHUNK = 64
GDN_SUPER = 256
CONF_CONV = 31
ATT_PATTERNS = ((128, 1), (512, 4), (2048, 16))
ATT_BLOCK = 128
ATT_HEADS_PER_STEP = {1: 6, 4: 1, 16: 1}
ROPE_THETA = 10000.0
POOL_WINDOWS = (2, 4, 8, 16)
POOL_GROUP = 192
N_BRANCHES = 4
N_GROUPS = 8
PER_GROUP = 8
N_EXPERTS = 64
TOP_K = 2
EXPERT_HIDDEN = 512
MOE_BLOCK = 128
MOE_SPARE = 2 * MOE_BLOCK
ROW_TILES = D_MODEL // HEAD_DIM
LOCAL_ROWS = 32
NORM_EPS = 1e-6
LN_EPS = 1e-5

U_QKV_A = 0
U_Z_A = 2304
U_B = 3072
U_C = 4608
U_D = 11520
U_WIDTH = 12288
W_BETA = 3072
W_UB = 3084
W_GATE = 12300

VMEM_LIMIT = 56 * 1024 * 1024


def _cparams(semantics, vmem=VMEM_LIMIT):
    return pltpu.CompilerParams(dimension_semantics=semantics, vmem_limit_bytes=vmem)


def _sigmoid(x):
    return 1.0 / (1.0 + jnp.exp(-x))


def _silu(x):
    return x * _sigmoid(x)


def _dot(a, b):
    return jnp.dot(a, b, preferred_element_type=F32)


def _dot_nt(a, b):
    return lax.dot_general(a, b, (((1,), (1,)), ((), ())), preferred_element_type=F32)


def _dot_tn(a, b):
    return lax.dot_general(a, b, (((0,), (0,)), ((), ())), preferred_element_type=F32)


def _rmsnorm_body(x_ref, g_ref, o_ref):
    x = x_ref[...]
    y = x * lax.rsqrt(jnp.mean(x * x, axis=-1, keepdims=True) + NORM_EPS) * g_ref[...]
    o_ref[...] = y.astype(o_ref.dtype)


def rmsnorm(x, g, out_dtype, tm=512):
    t, d = x.shape
    return pl.pallas_call(
        _rmsnorm_body,
        grid=(t // tm,),
        in_specs=[pl.BlockSpec((tm, d), lambda i: (i, 0)), pl.BlockSpec((1, d), lambda i: (0, 0))],
        out_specs=pl.BlockSpec((tm, d), lambda i: (i, 0)),
        out_shape=jax.ShapeDtypeStruct((t, d), out_dtype),
        compiler_params=_cparams(("parallel",)),
        name="rmsnorm",
    )(x, g.reshape(1, d))


def _wprep_copy(wt_hbm, buf, sem, j, slot, *, tn, n_plain, shift):
    row0 = j * tn + jnp.where(j >= n_plain, shift, 0)
    return pltpu.make_async_copy(wt_hbm.at[pl.ds(row0, tn)], buf.at[slot], sem.at[slot])


def _wprep_body(wt_hbm, o_ref, buf, sem, *, tn, n_plain, shift):
    j = pl.program_id(0)
    slot = j % 2
    cp = functools.partial(_wprep_copy, wt_hbm, buf, sem, tn=tn, n_plain=n_plain, shift=shift)

    @pl.when(j == 0)
    def _():
        cp(j, slot).start()

    cp(j, slot).wait()

    @pl.when(j + 1 < pl.num_programs(0))
    def _():
        cp(j + 1, 1 - slot).start()

    for l in range(o_ref.shape[0]):
        o_ref[l] = buf[slot, :, l, :].T.astype(BF16)


def prepare_w_in(w_in, tn=256):
    depth, k, n = w_in.shape
    w_t = jnp.transpose(w_in, (2, 0, 1))
    n_out = W_BETA + (n - W_UB)
    return pl.pallas_call(
        functools.partial(_wprep_body, tn=tn, n_plain=W_BETA // tn, shift=W_UB - W_BETA),
        grid=(n_out // tn,),
        in_specs=[pl.BlockSpec(memory_space=pl.ANY)],
        out_specs=pl.BlockSpec((depth, k, tn), lambda j: (0, 0, j)),
        out_shape=jax.ShapeDtypeStruct((depth, k, n_out), BF16),
        scratch_shapes=[pltpu.VMEM((2, tn, depth, k), F32), pltpu.SemaphoreType.DMA((2,))],
        compiler_params=_cparams(("arbitrary",)),
        name="prepare_w_in",
    )(w_t)


def _bd_weight_body(w_ref, o_ref):
    for l in range(o_ref.shape[0]):
        o_ref[l] = w_ref[:, l, :].T.astype(BF16)


def beta_decay_weights(w_in):
    depth, k, _ = w_in.shape
    w_t = jnp.transpose(w_in, (2, 0, 1))
    return pl.pallas_call(
        _bd_weight_body,
        grid=(1,),
        in_specs=[pl.BlockSpec((HEAD_DIM, depth, k), lambda i: (W_BETA // HEAD_DIM, 0, 0))],
        out_specs=pl.BlockSpec((depth, k, HEAD_DIM), lambda i: (0, 0, 0)),
        out_shape=jax.ShapeDtypeStruct((depth, k, HEAD_DIM), BF16),
        compiler_params=_cparams(("arbitrary",)),
        name="beta_decay_weights",
    )(w_t)


def _matmul_body(h_ref, w_ref, o_ref):
    o_ref[...] = _dot(h_ref[...], w_ref[...].astype(BF16)).astype(o_ref.dtype)


def project(h, w, layer, n, col0=0, tm=1024, tn=768):
    t, k = h.shape
    tn = min(tn, n)
    return pl.pallas_call(
        _matmul_body,
        grid=(n // tn, t // tm),
        in_specs=[pl.BlockSpec((tm, k), lambda j, i: (i, 0)),
                  pl.BlockSpec((None, k, tn), lambda j, i: (layer, 0, col0 // tn + j))],
        out_specs=pl.BlockSpec((tm, tn), lambda j, i: (i, j)),
        out_shape=jax.ShapeDtypeStruct((t, n), F32),
        compiler_params=_cparams(("parallel", "parallel")),
        name="in_proj",
    )(h, w)


def _rope_body(pos_ref, inv_ref, cos_ref, sin_ref):
    ang = pos_ref[...].astype(F32) * inv_ref[...]
    lane = lax.broadcasted_iota(jnp.int32, ang.shape, 1)
    cos_ref[...] = jnp.cos(ang)
    s = jnp.sin(ang)
    sin_ref[...] = jnp.where(lane < HEAD_DIM // 2, -s, s)


def rope_tables(pos, tm=1024):
    t = pos.shape[0]
    half = HEAD_DIM // 2
    inv = ROPE_THETA ** (-jnp.arange(half, dtype=F32) * 2.0 / HEAD_DIM)
    inv = jnp.concatenate([inv, inv]).reshape(1, HEAD_DIM)
    spec = pl.BlockSpec((tm, HEAD_DIM), lambda i: (i, 0))
    return pl.pallas_call(
        _rope_body,
        grid=(t // tm,),
        in_specs=[pl.BlockSpec((tm, 1), lambda i: (i, 0)), pl.BlockSpec((1, HEAD_DIM), lambda i: (0, 0))],
        out_specs=[spec, spec],
        out_shape=[jax.ShapeDtypeStruct((t, HEAD_DIM), F32)] * 2,
        compiler_params=_cparams(("parallel",)),
        name="rope_tables",
    )(pos.reshape(t, 1), inv)


def _rope(x, cos, sin):
    return x * cos + pltpu.roll(x, HEAD_DIM // 2, axis=1) * sin


def _attn_body(*refs, first, d, hw):
    if first:
        q_ref, kc_ref, kp_ref, vc_ref, vp_ref, cc_ref, sc_ref, cp_ref, sp_ref, o_out, lse_out = refs
        o_in = lse_in = None
    else:
        (q_ref, kc_ref, kp_ref, vc_ref, vp_ref, cc_ref, sc_ref, cp_ref, sp_ref,
         o_in, lse_in, o_out, lse_out) = refs
    n = pl.program_id(0)
    blk = ATT_BLOCK
    row = lax.broadcasted_iota(jnp.int32, (blk, blk), 0)
    col = lax.broadcasted_iota(jnp.int32, (blk, blk), 1)
    mask_cur = col <= row
    mask_prev = (col >= row) & (n > 0)
    scale = HEAD_DIM ** -0.5
    neg = -jnp.inf
    for r in range(d):
        rows = pl.ds(r, blk, stride=d) if d > 1 else pl.ds(0, blk)
        cosc, sinc, cosp, sinp = cc_ref[rows, :], sc_ref[rows, :], cp_ref[rows, :], sp_ref[rows, :]
        for h in range(hw):
            sl = slice(h * HEAD_DIM, (h + 1) * HEAD_DIM)
            q = (_rope(q_ref[rows, sl], cosc, sinc) * scale).astype(BF16)
            kc = _rope(kc_ref[rows, sl], cosc, sinc).astype(BF16)
            kp = _rope(kp_ref[rows, sl], cosp, sinp).astype(BF16)
            s_c = jnp.where(mask_cur, _dot_nt(q, kc), neg)
            s_p = jnp.where(mask_prev, _dot_nt(q, kp), neg)
            m = jnp.maximum(jnp.max(s_c, axis=-1, keepdims=True), jnp.max(s_p, axis=-1, keepdims=True))
            if not first:
                m_old = lse_in[rows, sl][:, 0:1]
                m = jnp.maximum(m, m_old)
            p_c = jnp.exp(s_c - m)
            p_p = jnp.exp(s_p - m)
            l = jnp.sum(p_c, axis=-1, keepdims=True) + jnp.sum(p_p, axis=-1, keepdims=True)
            acc = (_dot(p_c.astype(BF16), vc_ref[rows, sl].astype(BF16))
                   + _dot(p_p.astype(BF16), vp_ref[rows, sl].astype(BF16)))
            if not first:
                w_old = jnp.exp(m_old - m)
                l = l + w_old
                acc = acc + w_old * o_in[rows, sl]
            o_out[rows, sl] = acc / l
            lse_out[rows, sl] = jnp.broadcast_to(m + jnp.log(l), (blk, HEAD_DIM))


def attention_group(u, cos_t, sin_t, group, dilation, o_prev, lse_prev):
    t = u.shape[0]
    d = dilation
    span = ATT_BLOCK * d
    hw = ATT_HEADS_PER_STEP[d]
    cw = hw * HEAD_DIM
    first = o_prev is None
    qoff = (U_C + group * WIDTH) // cw
    koff = (U_C + 3 * WIDTH + group * WIDTH) // cw
    voff = (U_C + 6 * WIDTH + group * WIDTH) // cw

    def cur(off):
        return pl.BlockSpec((span, cw), lambda n, g: (n, off + g))

    def prev(off):
        return pl.BlockSpec((span, cw), lambda n, g: (jnp.maximum(n - 1, 0), off + g))

    tab_cur = pl.BlockSpec((span, HEAD_DIM), lambda n, g: (n, 0))
    tab_prev = pl.BlockSpec((span, HEAD_DIM), lambda n, g: (jnp.maximum(n - 1, 0), 0))
    o_spec = cur(0)
    in_specs = [cur(qoff), cur(koff), prev(koff), cur(voff), prev(voff), tab_cur, tab_cur, tab_prev, tab_prev]
    args = [u, u, u, u, u, cos_t, sin_t, cos_t, sin_t]
    if not first:
        in_specs += [o_spec, o_spec]
        args += [o_prev, lse_prev]
    return pl.pallas_call(
        functools.partial(_attn_body, first=first, d=d, hw=hw),
        grid=(t // span, N_HEADS // hw),
        in_specs=in_specs,
        out_specs=[o_spec, o_spec],
        out_shape=[jax.ShapeDtypeStruct((t, WIDTH), F32), jax.ShapeDtypeStruct((t, WIDTH), F32)],
        compiler_params=_cparams(("parallel", "parallel")),
        name=f"attn_d{d}",
    )(*args)


def _local_body(ub_ref, ud_ref, cw_ref, cb_ref, lg_ref, lb_ref, pw_ref, ps_ref, yb_ref, yd_ref, extb, extd, pool_buf, *, tm):
    i = pl.program_id(0)
    hb, hd = 32, 16

    @pl.when(i == 0)
    def _():
        extb[0:hb, :] = jnp.zeros((hb, WIDTH), F32)
        extd[0:hd, :] = jnp.zeros((hd, WIDTH), F32)

    @pl.when(i > 0)
    def _():
        extb[0:hb, :] = extb[tm:tm + hb, :]
        extd[0:hd, :] = extd[tm:tm + hd, :]

    ub = ub_ref[...]
    extb[hb:hb + tm, :] = ub[:, :WIDTH] * _sigmoid(ub[:, WIDTH:])
    extd[hd:hd + tm, :] = ud_ref[...]

    rc = LOCAL_ROWS
    ch = lax.broadcasted_iota(jnp.int32, (rc, WIDTH), 1)
    rows = lax.broadcasted_iota(jnp.int32, (rc, WIDTH), 0)
    for c0 in range(0, tm, rc):
        acc = jnp.zeros((rc, WIDTH), F32)
        for j in range(CONF_CONV):
            off = c0 + hb - (CONF_CONV - 1) + j
            acc = acc + cw_ref[j:j + 1, :] * extb[off:off + rc, :]
        cb = acc + cb_ref[...]
        mu = jnp.mean(cb, axis=-1, keepdims=True)
        cen = cb - mu
        var = jnp.mean(cen * cen, axis=-1, keepdims=True)
        yb = _silu(cen * lax.rsqrt(var + LN_EPS) * lg_ref[...] + lb_ref[...])
        yb_ref[c0:c0 + rc, :] = yb.astype(yb_ref.dtype)

        xd = extd[c0 + hd:c0 + hd + rc, :]
        tpos = i * tm + c0 + rows + 1
        run = xd
        pooled = None
        shift = 1
        for gi, w in enumerate(POOL_WINDOWS):
            while shift < w:
                run = run + extd[c0 + hd - shift:c0 + hd - shift + rc, :]
                shift += 1
            val = run / jnp.minimum(tpos, w).astype(F32) - xd
            pooled = val if pooled is None else jnp.where(ch >= gi * POOL_GROUP, val, pooled)
        pool_buf[c0:c0 + rc, :] = pooled.astype(pool_buf.dtype)
    yd = _dot(pool_buf[...], pw_ref[...]) * ps_ref[...]
    yd_ref[...] = yd.astype(yd_ref.dtype)


def local_mixers(u, conv_w, conv_b, ln_g, ln_b, pool_w_bd, pool_scale, tm=256):
    t = u.shape[0]
    row = lambda a: a.reshape(1, WIDTH)
    vec = pl.BlockSpec((1, WIDTH), lambda i: (0, 0))
    out = pl.BlockSpec((tm, WIDTH), lambda i: (i, 0))
    return pl.pallas_call(
        functools.partial(_local_body, tm=tm),
        grid=(t // tm,),
        in_specs=[pl.BlockSpec((tm, 2 * WIDTH), lambda i: (i, U_B // (2 * WIDTH))),
                  pl.BlockSpec((tm, WIDTH), lambda i: (i, U_D // WIDTH)),
                  pl.BlockSpec((CONF_CONV, WIDTH), lambda i: (0, 0)), vec, vec, vec,
                  pl.BlockSpec((WIDTH, WIDTH), lambda i: (0, 0)), vec],
        out_specs=[out, out],
        out_shape=[jax.ShapeDtypeStruct((t, WIDTH), BF16)] * 2,
        scratch_shapes=[pltpu.VMEM((tm + 32, WIDTH), F32), pltpu.VMEM((tm + 16, WIDTH), F32),
                        pltpu.VMEM((tm, WIDTH), BF16)],
        compiler_params=_cparams(("arbitrary",)),
        name="conv_pool",
    )(u, u, conv_w, row(conv_b), row(ln_g), row(ln_b), pool_w_bd, row(pool_scale))


def _gdn_body(qkv_ref, z_ref, bd_ref, cw_ref, alog_ref, dtb_ref, gn_ref, ya_ref, ext, state):
    s = pl.program_id(0)
    sup, c = GDN_SUPER, GDN_CHUNK
    halo = 8

    @pl.when(s == 0)
    def _():
        ext[0:halo, :] = jnp.zeros((halo, 3 * WIDTH), F32)
        state[...] = jnp.zeros(state.shape, F32)

    @pl.when(s > 0)
    def _():
        ext[0:halo, :] = ext[sup:sup + halo, :]

    ext[halo:halo + sup, :] = qkv_ref[...]
    acc = jnp.zeros((sup, 3 * WIDTH), F32)
    for j in range(GDN_CONV):
        off = halo - (GDN_CONV - 1) + j
        acc = acc + cw_ref[j:j + 1, :] * ext[off:off + sup, :]
    qkv = _silu(acc)

    bd = bd_ref[...]
    beta_all = _sigmoid(bd)
    xs = bd + dtb_ref[...]
    softplus = jnp.maximum(xs, 0.0) + jnp.log(1.0 + jnp.exp(-jnp.abs(xs)))
    g_all = -jnp.exp(alog_ref[...]) * softplus
    rowi = lax.broadcasted_iota(jnp.int32, (sup, HEAD_DIM), 0)
    in_chunk = rowi & (c - 1)
    gc = g_all
    sh = 1
    while sh < c:
        gc = gc + jnp.where(in_chunk >= sh, pltpu.roll(gc, sh, axis=0), 0.0)
        sh *= 2
    gct = gc.T

    ri = lax.broadcasted_iota(jnp.int32, (sup, sup), 0)
    ci = lax.broadcasted_iota(jnp.int32, (sup, sup), 1)
    same = (ri & -c) == (ci & -c)
    incl = same & (ri >= ci)
    strict = same & (ri > ci)
    eye = (ri == ci).astype(F32)

    for h in range(N_HEADS):
        sl = slice(h * HEAD_DIM, (h + 1) * HEAD_DIM)
        q = qkv[:, sl]
        k = qkv[:, WIDTH + h * HEAD_DIM:WIDTH + (h + 1) * HEAD_DIM]
        v = qkv[:, 2 * WIDTH + h * HEAD_DIM:2 * WIDTH + (h + 1) * HEAD_DIM]
        q = q * lax.rsqrt(jnp.sum(q * q, axis=-1, keepdims=True) + 1e-6) * (HEAD_DIM ** -0.5)
        k = k * lax.rsqrt(jnp.sum(k * k, axis=-1, keepdims=True) + 1e-6)
        b_col = beta_all[:, h:h + 1]
        gc_col = gc[:, N_HEADS + h:N_HEADS + h + 1]
        gc_row = gct[N_HEADS + h:N_HEADS + h + 1, :]
        dm = jnp.exp(jnp.where(incl, gc_col - gc_row, -jnp.inf))
        kb = k.astype(BF16)
        kk = _dot_nt(kb, kb)
        nmat = jnp.where(strict, -(b_col * kk * dm), 0.0)
        pmat = eye + nmat
        mb = nmat.astype(BF16)
        step = 2
        while step < c:
            m2 = _dot(mb, mb).astype(BF16)
            pmat = pmat + _dot(pmat.astype(BF16), m2)
            mb = m2
            step *= 2
        eg = jnp.exp(gc_col)
        rhs = jnp.concatenate([v * b_col, k * (b_col * eg)], axis=1).astype(BF16)
        sol = _dot(pmat.astype(BF16), rhs)
        u_c, w_c = sol[:, :HEAD_DIM], sol[:, HEAD_DIM:]
        qk = _dot_nt(q.astype(BF16), kb) * dm
        q_dec = (q * eg).astype(BF16)
        gl_col = jnp.concatenate(
            [jnp.broadcast_to(gc_col[(n + 1) * c - 1:(n + 1) * c, :], (c, 1)) for n in range(sup // c)], axis=0)
        k_dec = k * jnp.exp(gl_col - gc_col)
        w_b = w_c.astype(BF16)
        qk_b = qk.astype(BF16)
        st = state[h]
        outs = []
        for n in range(sup // c):
            rs = slice(n * c, (n + 1) * c)
            sb = st.astype(BF16)
            v_new = u_c[rs] - _dot(w_b[rs], sb)
            vb = v_new.astype(BF16)
            outs.append(_dot(q_dec[rs], sb) + _dot(qk_b[rs, rs], vb))
            g_last = jnp.exp(gc_col[(n + 1) * c - 1:(n + 1) * c, :])
            st = st * g_last + _dot(k_dec[rs].T.astype(BF16), vb)
        state[h] = st
        o = jnp.concatenate(outs, axis=0)
        o = o * lax.rsqrt(jnp.mean(o * o, axis=-1, keepdims=True) + NORM_EPS) * gn_ref[...]
        ya_ref[:, sl] = (o * _silu(z_ref[:, sl])).astype(ya_ref.dtype)


def gated_deltanet(u, bd, conv_w, a_log, dt_bias, norm_g):
    t = u.shape[0]
    sup = GDN_SUPER
    pad = jnp.zeros((N_HEADS,), F32)
    lane_row = lambda a: jnp.concatenate([pad, a, jnp.zeros((HEAD_DIM - 2 * N_HEADS,), F32)]).reshape(1, HEAD_DIM)
    vec = pl.BlockSpec((1, HEAD_DIM), lambda s: (0, 0))
    return pl.pallas_call(
        _gdn_body,
        grid=(t // sup,),
        in_specs=[pl.BlockSpec((sup, 3 * WIDTH), lambda s: (s, U_QKV_A // (3 * WIDTH))),
                  pl.BlockSpec((sup, WIDTH), lambda s: (s, U_Z_A // WIDTH)),
                  pl.BlockSpec((sup, HEAD_DIM), lambda s: (s, 0)),
                  pl.BlockSpec((GDN_CONV, 3 * WIDTH), lambda s: (0, 0)), vec, vec, vec],
        out_specs=pl.BlockSpec((sup, WIDTH), lambda s: (s, 0)),
        out_shape=jax.ShapeDtypeStruct((t, WIDTH), BF16),
        scratch_shapes=[pltpu.VMEM((sup + 8, 3 * WIDTH), F32), pltpu.VMEM((N_HEADS, HEAD_DIM, HEAD_DIM), F32)],
        compiler_params=_cparams(("arbitrary",)),
        name="gated_deltanet",
    )(u, u, bd, conv_w, lane_row(a_log), lane_row(dt_bias), norm_g.reshape(1, HEAD_DIM))


def _merge_body(h_ref, ya_ref, yb_ref, yc_ref, yd_ref, g0_ref, g1_ref, g2_ref, g3_ref,
                b0_ref, b1_ref, b2_ref, b3_ref, wo_ref, x_ref, n2_ref, wr_ref, br_ref,
                x1_ref, rl_ref, acc):
    j = pl.program_id(1)

    @pl.when(j == 0)
    def _():
        acc[...] = jnp.zeros(acc.shape, F32)

    hb = h_ref[...]
    merged = None
    for y_ref, g_ref, b_ref in ((ya_ref, g0_ref, b0_ref), (yb_ref, g1_ref, b1_ref),
                                (yc_ref, g2_ref, b2_ref), (yd_ref, g3_ref, b3_ref)):
        gate = _sigmoid(_dot(hb, g_ref[...]))
        term = gate * _dot(y_ref[...].astype(BF16), b_ref[...])
        merged = term if merged is None else merged + term
    acc[...] += _dot(merged.astype(BF16), wo_ref[...])

    @pl.when(j == pl.num_programs(1) - 1)
    def _():
        x1 = x_ref[...] + acc[...]
        tm = x1.shape[0]
        for a in range(ROW_TILES):
            x1_ref[pl.ds(a, tm, stride=ROW_TILES), :] = x1[:, a * HEAD_DIM:(a + 1) * HEAD_DIM]
        h2 =x1 * lax.rsqrt(jnp.mean(x1 * x1, axis=-1, keepdims=True) + NORM_EPS) * n2_ref[...]
        rl_ref[...] = jnp.dot(h2, wr_ref[...], preferred_element_type=F32,
                              precision=lax.Precision.HIGHEST) + br_ref[...]


def merge_project(h, ya, yb, yc, yd, w_all, w_branch, w_out, layer, x, norm2_g, w_router, b_router, tm=512, tn=256):
    t = h.shape[0]
    d = D_MODEL
    nj = d // tn
    g0 = U_WIDTH // tn
    row = pl.BlockSpec((tm, d), lambda i, j: (i, 0))
    ysp = pl.BlockSpec((tm, WIDTH), lambda i, j: (i, 0))
    gate_spec = lambda k: pl.BlockSpec((None, d, tn), lambda i, j: (layer, 0, g0 + k * nj + j))
    br_spec = lambda k: pl.BlockSpec((None, None, WIDTH, tn), lambda i, j: (layer, k, 0, j))
    lanes = pl.BlockSpec((tm, HEAD_DIM), lambda i, j: (i, 0))
    w_gate = w_all
    return pl.pallas_call(
        _merge_body,
        grid=(t // tm, nj),
        in_specs=[row, ysp, ysp, ysp, ysp] + [gate_spec(k) for k in range(4)] + [br_spec(k) for k in range(4)]
        + [pl.BlockSpec((None, tn, d), lambda i, j: (layer, j, 0)), row, pl.BlockSpec((1, d), lambda i, j: (0, 0)),
           pl.BlockSpec((d, HEAD_DIM), lambda i, j: (0, 0)), pl.BlockSpec((1, HEAD_DIM), lambda i, j: (0, 0))],
        out_specs=[pl.BlockSpec((tm * ROW_TILES, HEAD_DIM), lambda i, j: (i, 0)), lanes],
        out_shape=[jax.ShapeDtypeStruct((t * ROW_TILES, HEAD_DIM), F32), jax.ShapeDtypeStruct((t, HEAD_DIM), F32)],
        scratch_shapes=[pltpu.VMEM((tm, d), F32)],
        compiler_params=_cparams(("parallel", "arbitrary")),
        name="merge_project",
    )(h, ya, yb, yc, yd, w_gate, w_gate, w_gate, w_gate, w_branch, w_branch, w_branch, w_branch,
      w_out, x, norm2_g.reshape(1, d), w_router, b_router)


def _router_body(rl_ref, eid_ref, wts_ref):
    rl = rl_ref[...]
    lane_i = lax.broadcasted_iota(jnp.int32, rl.shape, 1)
    lane = lane_i.astype(F32)
    neg = -jnp.inf
    big = 1e9
    gl = jnp.where(lane < N_GROUPS, rl, neg)
    gmax = jnp.max(gl, axis=-1, keepdims=True)
    gi = jnp.min(jnp.where(gl == gmax, lane, big), axis=-1, keepdims=True)
    gp = 1.0 / jnp.sum(jnp.exp(gl - gmax), axis=-1, keepdims=True)
    lo = N_GROUPS + gi * PER_GROUP
    el = jnp.where((lane >= lo) & (lane < lo + PER_GROUP), rl, neg)
    e1 = jnp.max(el, axis=-1, keepdims=True)
    i1 = jnp.min(jnp.where(el == e1, lane, big), axis=-1, keepdims=True)
    el2 = jnp.where(lane == i1, neg, el)
    e2 = jnp.max(el2, axis=-1, keepdims=True)
    i2 = jnp.min(jnp.where(el2 == e2, lane, big), axis=-1, keepdims=True)
    esum = jnp.sum(jnp.exp(el - e1), axis=-1, keepdims=True)
    p1 = 1.0 / esum
    p2 = jnp.exp(e2 - e1) / esum
    w1 = gp * p1 / (p1 + p2)
    w2 = gp * p2 / (p1 + p2)
    eid = jnp.where(lane_i == 0, i1 - N_GROUPS, jnp.where(lane_i == 1, i2 - N_GROUPS, 0.0))
    eid_ref[...] = eid.astype(jnp.int32)
    wts_ref[...] = jnp.where(lane_i == 0, w1, jnp.where(lane_i == 1, w2, 0.0))


def route(rl, tm=512):
    t = rl.shape[0]
    spec = pl.BlockSpec((tm, HEAD_DIM), lambda i: (i, 0))
    return pl.pallas_call(
        _router_body,
        grid=(t // tm,),
        in_specs=[spec],
        out_specs=[spec, spec],
        out_shape=[jax.ShapeDtypeStruct((t, HEAD_DIM), jnp.int32), jax.ShapeDtypeStruct((t, HEAD_DIM), F32)],
        compiler_params=_cparams(("parallel",)),
        name="router",
    )(rl)


def _gather_copy(h_hbm, tok_ref, xbuf, gsem, blk, slot, r):
    row = pl.multiple_of(tok_ref[blk * MOE_BLOCK + r], ROW_TILES)
    return pltpu.make_async_copy(h_hbm.at[pl.ds(row, ROW_TILES), :],
                                 xbuf.at[slot, pl.ds(r * ROW_TILES, ROW_TILES), :], gsem.at[slot])


def _scatter_copy(y_hbm, dst_ref, obuf, ssem, blk, slot, r):
    row = pl.multiple_of(dst_ref[blk * MOE_BLOCK + r], ROW_TILES)
    return pltpu.make_async_copy(obuf.at[slot, pl.ds(r * ROW_TILES, ROW_TILES), :],
                                 y_hbm.at[pl.ds(row, ROW_TILES), :], ssem.at[slot])


def _expert_body(be_ref, nused_ref, tok_ref, dst_ref, h_hbm, g_ref, wup_ref, wdn_ref, y_hbm,
                 xbuf, obuf, wup_b, wdn_b, gsem, ssem):
    b = pl.program_id(0)
    n_used = nused_ref[0]
    slot = b % 2
    other = 1 - slot

    @pl.when(b == 0)
    def _():
        for r in range(MOE_BLOCK):
            _gather_copy(h_hbm, tok_ref, xbuf, gsem, b, slot, r).start()
        obuf[...] = jnp.zeros(obuf.shape, F32)
        half = y_hbm.shape[0] // ROW_TILES // TOP_K
        for q in range(2 * TOP_K):
            row0 = (q // 2) * half + (half - MOE_SPARE) + (q % 2) * MOE_BLOCK
            spare = pltpu.make_async_copy(obuf.at[q % 2],
                                          y_hbm.at[pl.ds(row0 * ROW_TILES, MOE_BLOCK * ROW_TILES), :], ssem.at[q % 2])
            spare.start()
            spare.wait()

    @pl.when(b < n_used)
    def _():
        for r in range(MOE_BLOCK):
            _gather_copy(h_hbm, tok_ref, xbuf, gsem, b, slot, r).wait()

        @pl.when(b + 1 < n_used)
        def _():
            for r in range(MOE_BLOCK):
                _gather_copy(h_hbm, tok_ref, xbuf, gsem, b + 1, other, r).start()

        changed = jnp.logical_or(b == 0, be_ref[b] != be_ref[jnp.maximum(b - 1, 0)])

        @pl.when(changed)
        def _():
            wup_b[...] = wup_ref[...].astype(BF16)
            wdn_b[...] = wdn_ref[...].astype(BF16)

        pieces = [xbuf[slot, pl.ds(a, MOE_BLOCK, stride=ROW_TILES), :] for a in range(ROW_TILES)]
        ssq = pieces[0] * pieces[0]
        for p in pieces[1:]:
            ssq = ssq + p * p
        inv = lax.rsqrt(jnp.sum(ssq, axis=-1, keepdims=True) * (1.0 / D_MODEL) + NORM_EPS)
        xn = jnp.concatenate([(p * inv * g_ref[:, a * HEAD_DIM:(a + 1) * HEAD_DIM]).astype(BF16)
                              for a, p in enumerate(pieces)], axis=1)
        gu = _dot(xn, wup_b[...])
        act = (_silu(gu[:, :EXPERT_HIDDEN]) * gu[:, EXPERT_HIDDEN:]).astype(BF16)
        y = _dot(act, wdn_b[...])

        @pl.when(b >= 2)
        def _():
            for r in range(MOE_BLOCK):
                _scatter_copy(y_hbm, dst_ref, obuf, ssem, b - 2, slot, r).wait()

        for a in range(ROW_TILES):
            obuf[slot, pl.ds(a, MOE_BLOCK, stride=ROW_TILES), :] = y[:, a * HEAD_DIM:(a + 1) * HEAD_DIM]
        for r in range(MOE_BLOCK):
            _scatter_copy(y_hbm, dst_ref, obuf, ssem, b, slot, r).start()

        @pl.when(b == n_used - 1)
        def _():
            @pl.when(b >= 1)
            def _():
                for r in range(MOE_BLOCK):
                    _scatter_copy(y_hbm, dst_ref, obuf, ssem, b - 1, other, r).wait()

            for r in range(MOE_BLOCK):
                _scatter_copy(y_hbm, dst_ref, obuf, ssem, b, slot, r).wait()


def experts(x1, norm_g, w_up, w_down, layer, block_e, n_used, slot_tok, slot_dst, n_rows):
    n_blocks = block_e.shape[0]
    d = D_MODEL
    grid_spec = pltpu.PrefetchScalarGridSpec(
        num_scalar_prefetch=4,
        grid=(n_blocks,),
        in_specs=[pl.BlockSpec(memory_space=pl.ANY),
                  pl.BlockSpec((1, d), lambda b, be, nu, tok, dst: (0, 0)),
                  pl.BlockSpec((None, None, d, 2 * EXPERT_HIDDEN), lambda b, be, nu, tok, dst: (layer, be[b], 0, 0)),
                  pl.BlockSpec((None, None, EXPERT_HIDDEN, d), lambda b, be, nu, tok, dst: (layer, be[b], 0, 0))],
        out_specs=pl.BlockSpec(memory_space=pl.ANY),
        scratch_shapes=[pltpu.VMEM((2, MOE_BLOCK * ROW_TILES, HEAD_DIM), F32),
                        pltpu.VMEM((2, MOE_BLOCK * ROW_TILES, HEAD_DIM), F32),
                        pltpu.VMEM((d, 2 * EXPERT_HIDDEN), BF16), pltpu.VMEM((EXPERT_HIDDEN, d), BF16),
                        pltpu.SemaphoreType.DMA((2,)), pltpu.SemaphoreType.DMA((2,))],
    )
    return pl.pallas_call(
        _expert_body,
        grid_spec=grid_spec,
        out_shape=jax.ShapeDtypeStruct((n_rows * ROW_TILES, HEAD_DIM), F32),
        compiler_params=_cparams(("arbitrary",)),
        name="experts",
    )(block_e, n_used, slot_tok, slot_dst, x1, norm_g.reshape(1, d), w_up, w_down)


def dispatch_tables(eid, t):
    n_assign = t * TOP_K
    flat_e = eid.reshape(n_assign)
    onehot = (flat_e[:, None] == jnp.arange(N_EXPERTS, dtype=jnp.int32)[None, :]).astype(jnp.int32)
    csum = jnp.cumsum(onehot, axis=0)
    counts = csum[-1]
    rank = jnp.sum(csum * onehot, axis=1) - 1
    padded = (counts + MOE_BLOCK - 1) // MOE_BLOCK * MOE_BLOCK
    pad_end = jnp.cumsum(padded)
    pad_start = pad_end - padded
    dest = pad_start[flat_e] + rank
    n_blocks = -(-(n_assign + N_EXPERTS * (MOE_BLOCK - 1)) // MOE_BLOCK)
    n_slots = n_blocks * MOE_BLOCK
    flat_a = jnp.arange(n_assign, dtype=jnp.int32)
    slot_tok = jnp.zeros((n_slots,), jnp.int32).at[dest].set(flat_a // TOP_K)
    half = t + MOE_SPARE
    sp = jnp.arange(n_slots, dtype=jnp.int32) % (TOP_K * MOE_SPARE)
    spare = (sp // MOE_SPARE) * half + t + sp % MOE_SPARE
    slot_dst = spare.at[dest].set((flat_a % TOP_K) * half + flat_a // TOP_K)
    n_used = pad_end[-1] // MOE_BLOCK
    starts = jnp.arange(n_blocks, dtype=jnp.int32) * MOE_BLOCK
    block_e = jnp.sum((pad_end[None, :] <= starts[:, None]).astype(jnp.int32), axis=1)
    block_e = jnp.minimum(block_e, N_EXPERTS - 1)
    last_e = block_e[jnp.maximum(n_used - 1, 0)]
    block_e = jnp.where(jnp.arange(n_blocks) < n_used, block_e, last_e)
    return (block_e, n_used.reshape(1).astype(jnp.int32), slot_tok * ROW_TILES, slot_dst * ROW_TILES,
            TOP_K * half)


def _combine_body(x_ref, y0_ref, y1_ref, w_ref, g_ref, x2_ref, hn_ref):
    w = w_ref[...]
    w0, w1 = w[:, 0:1], w[:, 1:2]
    tm = w.shape[0]
    ssq = None
    for a in range(ROW_TILES):
        rows = pl.ds(a, tm, stride=ROW_TILES)
        piece = x_ref[rows, :] + (w0 * y0_ref[rows, :] + w1 * y1_ref[rows, :])
        x2_ref[:, a * HEAD_DIM:(a + 1) * HEAD_DIM] = piece
        ssq = piece * piece if ssq is None else ssq + piece * piece
    inv = lax.rsqrt(jnp.sum(ssq, axis=-1, keepdims=True) * (1.0 / D_MODEL) + NORM_EPS)
    hn_ref[...] = (x2_ref[...] * inv * g_ref[...]).astype(hn_ref.dtype)


def combine(x1, y2, wts, next_g, next_dtype, tm=MOE_SPARE):
    t = x1.shape[0] // ROW_TILES
    d = D_MODEL
    half_blocks = y2.shape[0] // ROW_TILES // TOP_K // tm
    row = pl.BlockSpec((tm, d), lambda i: (i, 0))
    slab = pl.BlockSpec((tm * ROW_TILES, HEAD_DIM), lambda i: (i, 0))
    return pl.pallas_call(
        _combine_body,
        grid=(t // tm,),
        in_specs=[slab, slab, pl.BlockSpec((tm * ROW_TILES, HEAD_DIM), lambda i: (i + half_blocks, 0)),
                  pl.BlockSpec((tm, HEAD_DIM), lambda i: (i, 0)), pl.BlockSpec((1, d), lambda i: (0, 0))],
        out_specs=[row, row],
        out_shape=[jax.ShapeDtypeStruct((t, d), F32), jax.ShapeDtypeStruct((t, d), next_dtype)],
        compiler_params=_cparams(("parallel",)),
        name="moe_combine",
    )(x1, y2, y2, wts, next_g.reshape(1, d))


def _layer(x, h, cos_t, sin_t, layer, w_bd, w_all, conv_a_w, a_log, dt_bias, gdn_norm_g, conv_b_w, conv_b_b,
           ln_b_g, ln_b_b, pool_w, pool_scale, w_branch, w_out, norm2_g, wg, bg, we, be, w_up, w_down,
           next_g, next_dtype):
    t = x.shape[0]
    pool_bd = jnp.zeros((WIDTH, WIDTH), F32)
    for gi in range(len(POOL_WINDOWS)):
        pool_bd = lax.dynamic_update_slice(pool_bd, pool_w[gi], (gi * POOL_GROUP, gi * POOL_GROUP))
    w_router = jnp.pad(jnp.concatenate([wg, we], axis=1), ((0, 0), (0, HEAD_DIM - N_GROUPS - N_EXPERTS)))
    b_router = jnp.pad(jnp.concatenate([bg, be]), (0, HEAD_DIM - N_GROUPS - N_EXPERTS)).reshape(1, HEAD_DIM)

    u = project(h, w_all, layer, U_WIDTH)
    bd = project(h, w_bd, layer, HEAD_DIM)
    ya = gated_deltanet(u, bd, conv_a_w, a_log, dt_bias, gdn_norm_g)
    yb, yd = local_mixers(u, conv_b_w, conv_b_b, ln_b_g, ln_b_b, pool_bd.astype(BF16), pool_scale)
    yc = lse = None
    for gi, (_, dilation) in enumerate(ATT_PATTERNS):
        yc, lse = attention_group(u, cos_t, sin_t, gi, dilation, yc, lse)
    x1, rl = merge_project(h, ya, yb, yc, yd, w_all, w_branch, w_out, layer, x, norm2_g, w_router, b_router)
    eid, wts = route(rl)
    block_e, n_used, slot_tok, slot_dst, n_rows = dispatch_tables(eid[:, :TOP_K], t)
    y2 = experts(x1, norm2_g, w_up, w_down, layer, block_e, n_used, slot_tok, slot_dst, n_rows)
    return combine(x1, y2, wts, next_g, next_dtype)


def kernel(x, positions, norm1_g, w_in, conv_a_w, a_log, dt_bias, gdn_norm_g, conv_b_w, conv_b_b, ln_b_g, ln_b_b,
           pool_w, pool_scale, w_branch, w_out, norm2_g, router_group_w, router_group_b, router_expert_w,
           router_expert_b, w_up, w_down, final_norm_g):
    b_, s_, d = x.shape
    depth = w_in.shape[0]
    outs = []
    w_all = prepare_w_in(w_in)
    w_bd = beta_decay_weights(w_in)
    w_branch_b = w_branch.astype(BF16)
    w_out_b = w_out.astype(BF16)
    for bi in range(b_):
        xb = x[bi]
        cos_t, sin_t = rope_tables(positions[bi])
        h = rmsnorm(xb, norm1_g[0], BF16)
        for layer in range(depth):
            last = layer == depth - 1
            next_g = final_norm_g if last else norm1_g[layer + 1]
            xb, h = _layer(xb, h, cos_t, sin_t, layer, w_bd, w_all, conv_a_w[layer], a_log[layer], dt_bias[layer],
                           gdn_norm_g[layer], conv_b_w[layer], conv_b_b[layer], ln_b_g[layer], ln_b_b[layer],
                           pool_w[layer], pool_scale[layer], w_branch_b, w_out_b, norm2_g[layer],
                           router_group_w[layer], router_group_b[layer], router_expert_w[layer],
                           router_expert_b[layer], w_up, w_down, next_g, F32 if last else BF16)
        outs.append(h)
    return jnp.stack(outs, axis=0)
```

```python
import functools

import jax
import jax.numpy as jnp
from jax import lax
from jax.experimental import pallas as pl
from jax.experimental.pallas import tpu as pltpu

F32 = jnp.float32
BF16 = jnp.bfloat16

D_MODEL = 2048
N_HEADS = 6
HEAD_DIM = 128
WIDTH = N_HEADS * HEAD_DIM
GDN_CONV = 4
GDN_CHUNK = 64
GDN_SUPER = 256
CONF_CONV = 31
ATT_PATTERNS = ((128, 1), (512, 4), (2048, 16))
ATT_BLOCK = 128
ATT_HEADS_PER_STEP = {1: 6, 4: 1, 16: 1}
ROPE_THETA = 10000.0
POOL_WINDOWS = (2, 4, 8, 16)
POOL_GROUP = 192
N_BRANCHES = 4
N_GROUPS = 8
PER_GROUP = 8
N_EXPERTS = 64
TOP_K = 2
EXPERT_HIDDEN = 512
MOE_BLOCK = 128
MOE_SPARE = 2 * MOE_BLOCK
ROW_TILES = D_MODEL // HEAD_DIM
LOCAL_ROWS = 32
NORM_EPS = 1e-6
LN_EPS = 1e-5

U_QKV_A = 0
U_Z_A = 2304
U_B = 3072
U_C = 4608
U_D = 11520
U_WIDTH = 12288
W_BETA = 3072
W_UB = 3084
W_GATE = 12300

VMEM_LIMIT = 56 * 1024 * 1024


def _cparams(semantics, vmem=VMEM_LIMIT):
    return pltpu.CompilerParams(dimension_semantics=semantics, vmem_limit_bytes=vmem)


def _sigmoid(x):
    return 1.0 / (1.0 + jnp.exp(-x))


def _silu(x):
    return x * _sigmoid(x)


def _dot(a, b):
    return jnp.dot(a, b, preferred_element_type=F32)


def _dot_nt(a, b):
    return lax.dot_general(a, b, (((1,), (1,)), ((), ())), preferred_element_type=F32)


def _dot_tn(a, b):
    return lax.dot_general(a, b, (((0,), (0,)), ((), ())), preferred_element_type=F32)


def _rmsnorm_body(x_ref, g_ref, o_ref):
    x = x_ref[...]
    y = x * lax.rsqrt(jnp.mean(x * x, axis=-1, keepdims=True) + NORM_EPS) * g_ref[...]
    o_ref[...] = y.astype(o_ref.dtype)


def rmsnorm(x, g, out_dtype, tm=512):
    t, d = x.shape
    return pl.pallas_call(
        _rmsnorm_body,
        grid=(t // tm,),
        in_specs=[pl.BlockSpec((tm, d), lambda i: (i, 0)), pl.BlockSpec((1, d), lambda i: (0, 0))],
        out_specs=pl.BlockSpec((tm, d), lambda i: (i, 0)),
        out_shape=jax.ShapeDtypeStruct((t, d), out_dtype),
        compiler_params=_cparams(("parallel",)),
        name="rmsnorm",
    )(x, g.reshape(1, d))


def _wprep_copy(wt_hbm, buf, sem, j, slot, *, tn, n_plain, shift):
    row0 = j * tn + jnp.where(j >= n_plain, shift, 0)
    return pltpu.make_async_copy(wt_hbm.at[pl.ds(row0, tn)], buf.at[slot], sem.at[slot])


def _wprep_body(wt_hbm, o_ref, buf, sem, *, tn, n_plain, shift):
    j = pl.program_id(0)
    slot = j % 2
    cp = functools.partial(_wprep_copy, wt_hbm, buf, sem, tn=tn, n_plain=n_plain, shift=shift)

    @pl.when(j == 0)
    def _():
        cp(j, slot).start()

    cp(j, slot).wait()

    @pl.when(j + 1 < pl.num_programs(0))
    def _():
        cp(j + 1, 1 - slot).start()

    for l in range(o_ref.shape[0]):
        o_ref[l] = buf[slot, :, l, :].T.astype(BF16)


def prepare_w_in(w_in, tn=256):
    depth, k, n = w_in.shape
    w_t = jnp.transpose(w_in, (2, 0, 1))
    n_out = W_BETA + (n - W_UB)
    return pl.pallas_call(
        functools.partial(_wprep_body, tn=tn, n_plain=W_BETA // tn, shift=W_UB - W_BETA),
        grid=(n_out // tn,),
        in_specs=[pl.BlockSpec(memory_space=pl.ANY)],
        out_specs=pl.BlockSpec((depth, k, tn), lambda j: (0, 0, j)),
        out_shape=jax.ShapeDtypeStruct((depth, k, n_out), BF16),
        scratch_shapes=[pltpu.VMEM((2, tn, depth, k), F32), pltpu.SemaphoreType.DMA((2,))],
        compiler_params=_cparams(("arbitrary",)),
        name="prepare_w_in",
    )(w_t)


def _bd_weight_body(w_ref, o_ref):
    for l in range(o_ref.shape[0]):
        o_ref[l] = w_ref[:, l, :].T.astype(BF16)


def beta_decay_weights(w_in):
    depth, k, _ = w_in.shape
    w_t = jnp.transpose(w_in, (2, 0, 1))
    return pl.pallas_call(
        _bd_weight_body,
        grid=(1,),
        in_specs=[pl.BlockSpec((HEAD_DIM, depth, k), lambda i: (W_BETA // HEAD_DIM, 0, 0))],
        out_specs=pl.BlockSpec((depth, k, HEAD_DIM), lambda i: (0, 0, 0)),
        out_shape=jax.ShapeDtypeStruct((depth, k, HEAD_DIM), BF16),
        compiler_params=_cparams(("arbitrary",)),
        name="beta_decay_weights",
    )(w_t)


def _matmul_body(h_ref, w_ref, o_ref):
    o_ref[...] = _dot(h_ref[...], w_ref[...].astype(BF16)).astype(o_ref.dtype)


def project(h, w, layer, n, col0=0, tm=1024, tn=768):
    t, k = h.shape
    tn = min(tn, n)
    return pl.pallas_call(
        _matmul_body,
        grid=(n // tn, t // tm),
        in_specs=[pl.BlockSpec((tm, k), lambda j, i: (i, 0)),
                  pl.BlockSpec((None, k, tn), lambda j, i: (layer, 0, col0 // tn + j))],
        out_specs=pl.BlockSpec((tm, tn), lambda j, i: (i, j)),
        out_shape=jax.ShapeDtypeStruct((t, n), F32),
        compiler_params=_cparams(("parallel", "parallel")),
        name="in_proj",
    )(h, w)


def _rope_body(pos_ref, inv_ref, cos_ref, sin_ref):
    ang = pos_ref[...].astype(F32) * inv_ref[...]
    lane = lax.broadcasted_iota(jnp.int32, ang.shape, 1)
    cos_ref[...] = jnp.cos(ang)
    s = jnp.sin(ang)
    sin_ref[...] = jnp.where(lane < HEAD_DIM // 2, -s, s)


def rope_tables(pos, tm=1024):
    t = pos.shape[0]
    half = HEAD_DIM // 2
    inv = ROPE_THETA ** (-jnp.arange(half, dtype=F32) * 2.0 / HEAD_DIM)
    inv = jnp.concatenate([inv, inv]).reshape(1, HEAD_DIM)
    spec = pl.BlockSpec((tm, HEAD_DIM), lambda i: (i, 0))
    return pl.pallas_call(
        _rope_body,
        grid=(t // tm,),
        in_specs=[pl.BlockSpec((tm, 1), lambda i: (i, 0)), pl.BlockSpec((1, HEAD_DIM), lambda i: (0, 0))],
        out_specs=[spec, spec],
        out_shape=[jax.ShapeDtypeStruct((t, HEAD_DIM), F32)] * 2,
        compiler_params=_cparams(("parallel",)),
        name="rope_tables",
    )(pos.reshape(t, 1), inv)


def _rope(x, cos, sin):
    return x * cos + pltpu.roll(x, HEAD_DIM // 2, axis=1) * sin


def _attn_body(*refs, first, d, hw):
    if first:
        q_ref, kc_ref, kp_ref, vc_ref, vp_ref, cc_ref, sc_ref, cp_ref, sp_ref, o_out, lse_out = refs
        o_in = lse_in = None
    else:
        (q_ref, kc_ref, kp_ref, vc_ref, vp_ref, cc_ref, sc_ref, cp_ref, sp_ref,
         o_in, lse_in, o_out, lse_out) = refs
    n = pl.program_id(0)
    blk = ATT_BLOCK
    row = lax.broadcasted_iota(jnp.int32, (blk, blk), 0)
    col = lax.broadcasted_iota(jnp.int32, (blk, blk), 1)
    mask_cur = col <= row
    mask_prev = (col >= row) & (n > 0)
    scale = HEAD_DIM ** -0.5
    neg = -jnp.inf
    for r in range(d):
        rows = pl.ds(r, blk, stride=d) if d > 1 else pl.ds(0, blk)
        cosc, sinc, cosp, sinp = cc_ref[rows, :], sc_ref[rows, :], cp_ref[rows, :], sp_ref[rows, :]
        for h in range(hw):
            sl = slice(h * HEAD_DIM, (h + 1) * HEAD_DIM)
            q = (_rope(q_ref[rows, sl], cosc, sinc) * scale).astype(BF16)
            kc = _rope(kc_ref[rows, sl], cosc, sinc).astype(BF16)
            kp = _rope(kp_ref[rows, sl], cosp, sinp).astype(BF16)
            s_c = jnp.where(mask_cur, _dot_nt(q, kc), neg)
            s_p = jnp.where(mask_prev, _dot_nt(q, kp), neg)
            m = jnp.maximum(jnp.max(s_c, axis=-1, keepdims=True), jnp.max(s_p, axis=-1, keepdims=True))
            if not first:
                m_old = lse_in[rows, sl][:, 0:1]
                m = jnp.maximum(m, m_old)
            p_c = jnp.exp(s_c - m)
            p_p = jnp.exp(s_p - m)
            l = jnp.sum(p_c, axis=-1, keepdims=True) + jnp.sum(p_p, axis=-1, keepdims=True)
            acc = (_dot(p_c.astype(BF16), vc_ref[rows, sl].astype(BF16))
                   + _dot(p_p.astype(BF16), vp_ref[rows, sl].astype(BF16)))
            if not first:
                w_old = jnp.exp(m_old - m)
                l = l + w_old
                acc = acc + w_old * o_in[rows, sl]
            o_out[rows, sl] = acc / l
            lse_out[rows, sl] = jnp.broadcast_to(m + jnp.log(l), (blk, HEAD_DIM))


def attention_group(u, cos_t, sin_t, group, dilation, o_prev, lse_prev):
    t = u.shape[0]
    d = dilation
    span = ATT_BLOCK * d
    hw = ATT_HEADS_PER_STEP[d]
    cw = hw * HEAD_DIM
    first = o_prev is None
    qoff = (U_C + group * WIDTH) // cw
    koff = (U_C + 3 * WIDTH + group * WIDTH) // cw
    voff = (U_C + 6 * WIDTH + group * WIDTH) // cw

    def cur(off):
        return pl.BlockSpec((span, cw), lambda n, g: (n, off + g))

    def prev(off):
        return pl.BlockSpec((span, cw), lambda n, g: (jnp.maximum(n - 1, 0), off + g))

    tab_cur = pl.BlockSpec((span, HEAD_DIM), lambda n, g: (n, 0))
    tab_prev = pl.BlockSpec((span, HEAD_DIM), lambda n, g: (jnp.maximum(n - 1, 0), 0))
    o_spec = cur(0)
    in_specs = [cur(qoff), cur(koff), prev(koff), cur(voff), prev(voff), tab_cur, tab_cur, tab_prev, tab_prev]
    args = [u, u, u, u, u, cos_t, sin_t, cos_t, sin_t]
    if not first:
        in_specs += [o_spec, o_spec]
        args += [o_prev, lse_prev]
    return pl.pallas_call(
        functools.partial(_attn_body, first=first, d=d, hw=hw),
        grid=(t // span, N_HEADS // hw),
        in_specs=in_specs,
        out_specs=[o_spec, o_spec],
        out_shape=[jax.ShapeDtypeStruct((t, WIDTH), F32), jax.ShapeDtypeStruct((t, WIDTH), F32)],
        compiler_params=_cparams(("parallel", "parallel")),
        name=f"attn_d{d}",
    )(*args)


def _local_body(ub_ref, ud_ref, cw_ref, cb_ref, lg_ref, lb_ref, pw_ref, ps_ref, yb_ref, yd_ref, extb, extd, pool_buf, *, tm):
    i = pl.program_id(0)
    hb, hd = 32, 16

    @pl.when(i == 0)
    def _():
        extb[0:hb, :] = jnp.zeros((hb, WIDTH), F32)
        extd[0:hd, :] = jnp.zeros((hd, WIDTH), F32)

    @pl.when(i > 0)
    def _():
        extb[0:hb, :] = extb[tm:tm + hb, :]
        extd[0:hd, :] = extd[tm:tm + hd, :]

    ub = ub_ref[...]
    extb[hb:hb + tm, :] = ub[:, :WIDTH] * _sigmoid(ub[:, WIDTH:])
    extd[hd:hd + tm, :] = ud_ref[...]

    rc = LOCAL_ROWS
    ch = lax.broadcasted_iota(jnp.int32, (rc, WIDTH), 1)
    rows = lax.broadcasted_iota(jnp.int32, (rc, WIDTH), 0)
    for c0 in range(0, tm, rc):
        acc = jnp.zeros((rc, WIDTH), F32)
        for j in range(CONF_CONV):
            off = c0 + hb - (CONF_CONV - 1) + j
            acc = acc + cw_ref[j:j + 1, :] * extb[off:off + rc, :]
        cb = acc + cb_ref[...]
        mu = jnp.mean(cb, axis=-1, keepdims=True)
        cen = cb - mu
        var = jnp.mean(cen * cen, axis=-1, keepdims=True)
        yb = _silu(cen * lax.rsqrt(var + LN_EPS) * lg_ref[...] + lb_ref[...])
        yb_ref[c0:c0 + rc, :] = yb.astype(yb_ref.dtype)

        xd = extd[c0 + hd:c0 + hd + rc, :]
        tpos = i * tm + c0 + rows + 1
        run = xd
        pooled = None
        shift = 1
        for gi, w in enumerate(POOL_WINDOWS):
            while shift < w:
                run = run + extd[c0 + hd - shift:c0 + hd - shift + rc, :]
                shift += 1
            val = run / jnp.minimum(tpos, w).astype(F32) - xd
            pooled = val if pooled is None else jnp.where(ch >= gi * POOL_GROUP, val, pooled)
        pool_buf[c0:c0 + rc, :] = pooled.astype(pool_buf.dtype)
    yd = _dot(pool_buf[...], pw_ref[...]) * ps_ref[...]
    yd_ref[...] = yd.astype(yd_ref.dtype)


def local_mixers(u, conv_w, conv_b, ln_g, ln_b, pool_w_bd, pool_scale, tm=256):
    t = u.shape[0]
    row = lambda a: a.reshape(1, WIDTH)
    vec = pl.BlockSpec((1, WIDTH), lambda i: (0, 0))
    out = pl.BlockSpec((tm, WIDTH), lambda i: (i, 0))
    return pl.pallas_call(
        functools.partial(_local_body, tm=tm),
        grid=(t // tm,),
        in_specs=[pl.BlockSpec((tm, 2 * WIDTH), lambda i: (i, U_B // (2 * WIDTH))),
                  pl.BlockSpec((tm, WIDTH), lambda i: (i, U_D // WIDTH)),
                  pl.BlockSpec((CONF_CONV, WIDTH), lambda i: (0, 0)), vec, vec, vec,
                  pl.BlockSpec((WIDTH, WIDTH), lambda i: (0, 0)), vec],
        out_specs=[out, out],
        out_shape=[jax.ShapeDtypeStruct((t, WIDTH), BF16)] * 2,
        scratch_shapes=[pltpu.VMEM((tm + 32, WIDTH), F32), pltpu.VMEM((tm + 16, WIDTH), F32),
                        pltpu.VMEM((tm, WIDTH), BF16)],
        compiler_params=_cparams(("arbitrary",)),
        name="conv_pool",
    )(u, u, conv_w, row(conv_b), row(ln_g), row(ln_b), pool_w_bd, row(pool_scale))


def _gdn_body(qkv_ref, z_ref, bd_ref, cw_ref, alog_ref, dtb_ref, gn_ref, ya_ref, ext, state):
    s = pl.program_id(0)
    sup, c = GDN_SUPER, GDN_CHUNK
    halo = 8

    @pl.when(s == 0)
    def _():
        ext[0:halo, :] = jnp.zeros((halo, 3 * WIDTH), F32)
        state[...] = jnp.zeros(state.shape, F32)

    @pl.when(s > 0)
    def _():
        ext[0:halo, :] = ext[sup:sup + halo, :]

    ext[halo:halo + sup, :] = qkv_ref[...]
    acc = jnp.zeros((sup, 3 * WIDTH), F32)
    for j in range(GDN_CONV):
        off = halo - (GDN_CONV - 1) + j
        acc = acc + cw_ref[j:j + 1, :] * ext[off:off + sup, :]
    qkv = _silu(acc)

    bd = bd_ref[...]
    beta_all = _sigmoid(bd)
    xs = bd + dtb_ref[...]
    softplus = jnp.maximum(xs, 0.0) + jnp.log(1.0 + jnp.exp(-jnp.abs(xs)))
    g_all = -jnp.exp(alog_ref[...]) * softplus
    rowi = lax.broadcasted_iota(jnp.int32, (sup, HEAD_DIM), 0)
    in_chunk = rowi & (c - 1)
    gc = g_all
    sh = 1
    while sh < c:
        gc = gc + jnp.where(in_chunk >= sh, pltpu.roll(gc, sh, axis=0), 0.0)
        sh *= 2
    gct = gc.T

    ri = lax.broadcasted_iota(jnp.int32, (sup, sup), 0)
    ci = lax.broadcasted_iota(jnp.int32, (sup, sup), 1)
    same = (ri & -c) == (ci & -c)
    incl = same & (ri >= ci)
    strict = same & (ri > ci)
    eye = (ri == ci).astype(F32)

    for h in range(N_HEADS):
        sl = slice(h * HEAD_DIM, (h + 1) * HEAD_DIM)
        q = qkv[:, sl]
        k = qkv[:, WIDTH + h * HEAD_DIM:WIDTH + (h + 1) * HEAD_DIM]
        v = qkv[:, 2 * WIDTH + h * HEAD_DIM:2 * WIDTH + (h + 1) * HEAD_DIM]
        q = q * lax.rsqrt(jnp.sum(q * q, axis=-1, keepdims=True) + 1e-6) * (HEAD_DIM ** -0.5)
        k = k * lax.rsqrt(jnp.sum(k * k, axis=-1, keepdims=True) + 1e-6)
        b_col = beta_all[:, h:h + 1]
        gc_col = gc[:, N_HEADS + h:N_HEADS + h + 1]
        gc_row = gct[N_HEADS + h:N_HEADS + h + 1, :]
        dm = jnp.exp(jnp.where(incl, gc_col - gc_row, -jnp.inf))
        kb = k.astype(BF16)
        kk = _dot_nt(kb, kb)
        nmat = jnp.where(strict, -(b_col * kk * dm), 0.0)
        pmat = eye + nmat
        mb = nmat.astype(BF16)
        step = 2
        while step < c:
            m2 = _dot(mb, mb).astype(BF16)
            pmat = pmat + _dot(pmat.astype(BF16), m2)
            mb = m2
            step *= 2
        eg = jnp.exp(gc_col)
        rhs = jnp.concatenate([v * b_col, k * (b_col * eg)], axis=1).astype(BF16)
        sol = _dot(pmat.astype(BF16), rhs)
        u_c, w_c = sol[:, :HEAD_DIM], sol[:, HEAD_DIM:]
        qk = _dot_nt(q.astype(BF16), kb) * dm
        q_dec = (q * eg).astype(BF16)
        gl_col = jnp.concatenate(
            [jnp.broadcast_to(gc_col[(n + 1) * c - 1:(n + 1) * c, :], (c, 1)) for n in range(sup // c)], axis=0)
        k_dec = k * jnp.exp(gl_col - gc_col)
        w_b = w_c.astype(BF16)
        qk_b = qk.astype(BF16)
        st = state[h]
        outs = []
        for n in range(sup // c):
            rs = slice(n * c, (n + 1) * c)
            sb = st.astype(BF16)
            v_new = u_c[rs] - _dot(w_b[rs], sb)
            vb = v_new.astype(BF16)
            outs.append(_dot(q_dec[rs], sb) + _dot(qk_b[rs, rs], vb))
            g_last = jnp.exp(gc_col[(n + 1) * c - 1:(n + 1) * c, :])
            st = st * g_last + _dot(k_dec[rs].T.astype(BF16), vb)
        state[h] = st
        o = jnp.concatenate(outs, axis=0)
        o = o * lax.rsqrt(jnp.mean(o * o, axis=-1, keepdims=True) + NORM_EPS) * gn_ref[...]
        ya_ref[:, sl] = (o * _silu(z_ref[:, sl])).astype(ya_ref.dtype)


def gated_deltanet(u, bd, conv_w, a_log, dt_bias, norm_g):
    t = u.shape[0]
    sup = GDN_SUPER
    pad = jnp.zeros((N_HEADS,), F32)
    lane_row = lambda a: jnp.concatenate([pad, a, jnp.zeros((HEAD_DIM - 2 * N_HEADS,), F32)]).reshape(1, HEAD_DIM)
    vec = pl.BlockSpec((1, HEAD_DIM), lambda s: (0, 0))
    return pl.pallas_call(
        _gdn_body,
        grid=(t // sup,),
        in_specs=[pl.BlockSpec((sup, 3 * WIDTH), lambda s: (s, U_QKV_A // (3 * WIDTH))),
                  pl.BlockSpec((sup, WIDTH), lambda s: (s, U_Z_A // WIDTH)),
                  pl.BlockSpec((sup, HEAD_DIM), lambda s: (s, 0)),
                  pl.BlockSpec((GDN_CONV, 3 * WIDTH), lambda s: (0, 0)), vec, vec, vec],
        out_specs=pl.BlockSpec((sup, WIDTH), lambda s: (s, 0)),
        out_shape=jax.ShapeDtypeStruct((t, WIDTH), BF16),
        scratch_shapes=[pltpu.VMEM((sup + 8, 3 * WIDTH), F32), pltpu.VMEM((N_HEADS, HEAD_DIM, HEAD_DIM), F32)],
        compiler_params=_cparams(("arbitrary",)),
        name="gated_deltanet",
    )(u, u, bd, conv_w, lane_row(a_log), lane_row(dt_bias), norm_g.reshape(1, HEAD_DIM))


def _merge_body(h_ref, ya_ref, yb_ref, yc_ref, yd_ref, g0_ref, g1_ref, g2_ref, g3_ref,
                b0_ref, b1_ref, b2_ref, b3_ref, wo_ref, x_ref, n2_ref, wr_ref, br_ref,
                x1_ref, rl_ref, acc):
    j = pl.program_id(1)

    @pl.when(j == 0)
    def _():
        acc[...] = jnp.zeros(acc.shape, F32)

    hb = h_ref[...]
    merged = None
    for y_ref, g_ref, b_ref in ((ya_ref, g0_ref, b0_ref), (yb_ref, g1_ref, b1_ref),
                                (yc_ref, g2_ref, b2_ref), (yd_ref, g3_ref, b3_ref)):
        gate = _sigmoid(_dot(hb, g_ref[...]))
        term = gate * _dot(y_ref[...].astype(BF16), b_ref[...])
        merged = term if merged is None else merged + term
    acc[...] += _dot(merged.astype(BF16), wo_ref[...])

    @pl.when(j == pl.num_programs(1) - 1)
    def _():
        x1 = x_ref[...] + acc[...]
        tm = x1.shape[0]
        for a in range(ROW_TILES):
            x1_ref[pl.ds(a, tm, stride=ROW_TILES), :] = x1[:, a * HEAD_DIM:(a + 1) * HEAD_DIM]
        h2 =x1 * lax.rsqrt(jnp.mean(x1 * x1, axis=-1, keepdims=True) + NORM_EPS) * n2_ref[...]
        rl_ref[...] = jnp.dot(h2, wr_ref[...], preferred_element_type=F32,
                              precision=lax.Precision.HIGHEST) + br_ref[...]


def merge_project(h, ya, yb, yc, yd, w_all, w_branch, w_out, layer, x, norm2_g, w_router, b_router, tm=512, tn=256):
    t = h.shape[0]
    d = D_MODEL
    nj = d // tn
    g0 = U_WIDTH // tn
    row = pl.BlockSpec((tm, d), lambda i, j: (i, 0))
    ysp = pl.BlockSpec((tm, WIDTH), lambda i, j: (i, 0))
    gate_spec = lambda k: pl.BlockSpec((None, d, tn), lambda i, j: (layer, 0, g0 + k * nj + j))
    br_spec = lambda k: pl.BlockSpec((None, None, WIDTH, tn), lambda i, j: (layer, k, 0, j))
    lanes = pl.BlockSpec((tm, HEAD_DIM), lambda i, j: (i, 0))
    w_gate = w_all
    return pl.pallas_call(
        _merge_body,
        grid=(t // tm, nj),
        in_specs=[row, ysp, ysp, ysp, ysp] + [gate_spec(k) for k in range(4)] + [br_spec(k) for k in range(4)]
        + [pl.BlockSpec((None, tn, d), lambda i, j: (layer, j, 0)), row, pl.BlockSpec((1, d), lambda i, j: (0, 0)),
           pl.BlockSpec((d, HEAD_DIM), lambda i, j: (0, 0)), pl.BlockSpec((1, HEAD_DIM), lambda i, j: (0, 0))],
        out_specs=[pl.BlockSpec((tm * ROW_TILES, HEAD_DIM), lambda i, j: (i, 0)), lanes],
        out_shape=[jax.ShapeDtypeStruct((t * ROW_TILES, HEAD_DIM), F32), jax.ShapeDtypeStruct((t, HEAD_DIM), F32)],
        scratch_shapes=[pltpu.VMEM((tm, d), F32)],
        compiler_params=_cparams(("parallel", "arbitrary")),
        name="merge_project",
    )(h, ya, yb, yc, yd, w_gate, w_gate, w_gate, w_gate, w_branch, w_branch, w_branch, w_branch,
      w_out, x, norm2_g.reshape(1, d), w_router, b_router)


def _router_body(rl_ref, eid_ref, wts_ref):
    rl = rl_ref[...]
    lane_i = lax.broadcasted_iota(jnp.int32, rl.shape, 1)
    lane = lane_i.astype(F32)
    neg = -jnp.inf
    big = 1e9
    gl = jnp.where(lane < N_GROUPS, rl, neg)
    gmax = jnp.max(gl, axis=-1, keepdims=True)
    gi = jnp.min(jnp.where(gl == gmax, lane, big), axis=-1, keepdims=True)
    gp = 1.0 / jnp.sum(jnp.exp(gl - gmax), axis=-1, keepdims=True)
    lo = N_GROUPS + gi * PER_GROUP
    el = jnp.where((lane >= lo) & (lane < lo + PER_GROUP), rl, neg)
    e1 = jnp.max(el, axis=-1, keepdims=True)
    i1 = jnp.min(jnp.where(el == e1, lane, big), axis=-1, keepdims=True)
    el2 = jnp.where(lane == i1, neg, el)
    e2 = jnp.max(el2, axis=-1, keepdims=True)
    i2 = jnp.min(jnp.where(el2 == e2, lane, big), axis=-1, keepdims=True)
    esum = jnp.sum(jnp.exp(el - e1), axis=-1, keepdims=True)
    p1 = 1.0 / esum
    p2 = jnp.exp(e2 - e1) / esum
    w1 = gp * p1 / (p1 + p2)
    w2 = gp * p2 / (p1 + p2)
    eid = jnp.where(lane_i == 0, i1 - N_GROUPS, jnp.where(lane_i == 1, i2 - N_GROUPS, 0.0))
    eid_ref[...] = eid.astype(jnp.int32)
    wts_ref[...] = jnp.where(lane_i == 0, w1, jnp.where(lane_i == 1, w2, 0.0))


def route(rl, tm=512):
    t = rl.shape[0]
    spec = pl.BlockSpec((tm, HEAD_DIM), lambda i: (i, 0))
    return pl.pallas_call(
        _router_body,
        grid=(t // tm,),
        in_specs=[spec],
        out_specs=[spec, spec],
        out_shape=[jax.ShapeDtypeStruct((t, HEAD_DIM), jnp.int32), jax.ShapeDtypeStruct((t, HEAD_DIM), F32)],
        compiler_params=_cparams(("parallel",)),
        name="router",
    )(rl)


def _row_copy(src, src_row, dst, dst_row, sem):
    return pltpu.make_async_copy(src.at[pl.ds(pl.multiple_of(src_row, ROW_TILES), ROW_TILES), :],
                                 dst.at[pl.ds(pl.multiple_of(dst_row, ROW_TILES), ROW_TILES), :], sem)


def _zero_copy(zbuf, zrow_ref, xs_hbm, zsem, e):
    start = pl.multiple_of(jnp.maximum(zrow_ref[e], 0), MOE_BLOCK * ROW_TILES)
    return pltpu.make_async_copy(zbuf, xs_hbm.at[pl.ds(start, MOE_BLOCK * ROW_TILES), :], zsem)


def _zero_block(zbuf, xs_hbm, zsem, blk):
    start = pl.multiple_of(blk * (MOE_BLOCK * ROW_TILES), MOE_BLOCK * ROW_TILES)
    return pltpu.make_async_copy(zbuf, xs_hbm.at[pl.ds(start, MOE_BLOCK * ROW_TILES), :], zsem)


def _sort_rows_body(dest_ref, zrow_ref, nused_ref, x_ref, xs_hbm, zbuf, sem, zsem, *, tm):
    i = pl.program_id(0)
    sem, zsem = sem.at[0], zsem.at[0]

    @pl.when(i == 0)
    def _():
        zbuf[...] = jnp.zeros(zbuf.shape, F32)
        n_blocks = xs_hbm.shape[0] // (MOE_BLOCK * ROW_TILES)
        for e in range(N_EXPERTS):
            @pl.when(zrow_ref[e] >= 0)
            def _():
                _zero_copy(zbuf, zrow_ref, xs_hbm, zsem, e).start()

        def z_start(blk, c):
            _zero_block(zbuf, xs_hbm, zsem, blk).start()
            return c

        def z_wait(blk, c):
            _zero_block(zbuf, xs_hbm, zsem, blk).wait()
            return c

        lax.fori_loop(nused_ref[0], n_blocks, z_start, 0)
        for e in range(N_EXPERTS):
            @pl.when(zrow_ref[e] >= 0)
            def _():
                _zero_copy(zbuf, zrow_ref, xs_hbm, zsem, e).wait()
        lax.fori_loop(nused_ref[0], n_blocks, z_wait, 0)

    def copies(r):
        a0 = (i * tm + r) * TOP_K
        return [_row_copy(x_ref, r * ROW_TILES, xs_hbm, dest_ref[a0 + k], sem) for k in range(TOP_K)]

    def start(r, c):
        for cp in copies(r):
            cp.start()
        return c

    def wait(r, c):
        for cp in copies(r):
            cp.wait()
        return c

    lax.fori_loop(0, tm, start, 0, unroll=8)
    lax.fori_loop(0, tm, wait, 0, unroll=8)


def sort_rows(x1, dest, zrow, n_used, n_slots, tm=512):
    t = x1.shape[0] // ROW_TILES
    grid_spec = pltpu.PrefetchScalarGridSpec(
        num_scalar_prefetch=3,
        grid=(t // tm,),
        in_specs=[pl.BlockSpec((tm * ROW_TILES, HEAD_DIM), lambda i, dest, zrow, nu: (i, 0))],
        out_specs=pl.BlockSpec(memory_space=pl.ANY),
        scratch_shapes=[pltpu.VMEM((MOE_BLOCK * ROW_TILES, HEAD_DIM), F32),
                        pltpu.SemaphoreType.DMA((1,)), pltpu.SemaphoreType.DMA((1,))],
    )
    return pl.pallas_call(
        functools.partial(_sort_rows_body, tm=tm),
        grid_spec=grid_spec,
        out_shape=jax.ShapeDtypeStruct((n_slots * ROW_TILES, HEAD_DIM), F32),
        compiler_params=_cparams(("arbitrary",)),
        name="moe_sort_rows",
    )(dest, zrow, n_used, x1)


def _expert_body(be_ref, nused_ref, xs_ref, g_ref, wup_ref, wdn_ref, ys_ref, wup_b, wdn_b):
    b = pl.program_id(0)

    @pl.when(b >= nused_ref[0])
    def _():
        ys_ref[...] = jnp.zeros(ys_ref.shape, F32)

    @pl.when(b < nused_ref[0])
    def _():
        changed = jnp.logical_or(b == 0, be_ref[b] != be_ref[jnp.maximum(b - 1, 0)])

        @pl.when(changed)
        def _():
            wup_b[...] = wup_ref[...].astype(BF16)
            wdn_b[...] = wdn_ref[...].astype(BF16)

        pieces = [xs_ref[pl.ds(a, MOE_BLOCK, stride=ROW_TILES), :] for a in range(ROW_TILES)]
        ssq = pieces[0] * pieces[0]
        for p in pieces[1:]:
            ssq = ssq + p * p
        inv = lax.rsqrt(jnp.sum(ssq, axis=-1, keepdims=True) * (1.0 / D_MODEL) + NORM_EPS)
        xn = jnp.concatenate([(p * inv * g_ref[:, a * HEAD_DIM:(a + 1) * HEAD_DIM]).astype(BF16)
                              for a, p in enumerate(pieces)], axis=1)
        gu = _dot(xn, wup_b[...])
        act = (_silu(gu[:, :EXPERT_HIDDEN]) * gu[:, EXPERT_HIDDEN:]).astype(BF16)
        y = _dot(act, wdn_b[...])
        for a in range(ROW_TILES):
            ys_ref[pl.ds(a, MOE_BLOCK, stride=ROW_TILES), :] = y[:, a * HEAD_DIM:(a + 1) * HEAD_DIM]


def experts(xs, norm_g, w_up, w_down, layer, block_e, n_used):
    n_blocks = block_e.shape[0]
    d = D_MODEL
    rows = lambda b, be, nu: (jnp.minimum(b, nu[0] - 1), 0)
    out_rows = lambda b, be, nu: (b, 0)
    grid_spec = pltpu.PrefetchScalarGridSpec(
        num_scalar_prefetch=2,
        grid=(n_blocks,),
        in_specs=[pl.BlockSpec((MOE_BLOCK * ROW_TILES, HEAD_DIM), rows),
                  pl.BlockSpec((1, d), lambda b, be, nu: (0, 0)),
                  pl.BlockSpec((None, None, d, 2 * EXPERT_HIDDEN), lambda b, be, nu: (layer, be[b], 0, 0)),
                  pl.BlockSpec((None, None, EXPERT_HIDDEN, d), lambda b, be, nu: (layer, be[b], 0, 0))],
        out_specs=pl.BlockSpec((MOE_BLOCK * ROW_TILES, HEAD_DIM), out_rows),
        scratch_shapes=[pltpu.VMEM((d, 2 * EXPERT_HIDDEN), BF16), pltpu.VMEM((EXPERT_HIDDEN, d), BF16)],
    )
    return pl.pallas_call(
        _expert_body,
        grid_spec=grid_spec,
        out_shape=jax.ShapeDtypeStruct(xs.shape, F32),
        compiler_params=_cparams(("arbitrary",)),
        name="experts",
    )(block_e, n_used, xs, norm_g.reshape(1, d), w_up, w_down)


def dispatch_tables(eid, t):
    n_assign = t * TOP_K
    flat_e = eid.reshape(n_assign)
    onehot = (flat_e[:, None] == jnp.arange(N_EXPERTS, dtype=jnp.int32)[None, :]).astype(jnp.int32)
    csum = jnp.cumsum(onehot, axis=0)
    counts = csum[-1]
    rank = jnp.sum(csum * onehot, axis=1) - 1
    padded = (counts + MOE_BLOCK - 1) // MOE_BLOCK * MOE_BLOCK
    pad_end = jnp.cumsum(padded)
    pad_start = pad_end - padded
    dest = pad_start[flat_e] + rank
    n_blocks = -(-(n_assign + N_EXPERTS * (MOE_BLOCK - 1)) // MOE_BLOCK)
    n_slots = n_blocks * MOE_BLOCK
    zrow = jnp.where(padded > 0, (pad_end - MOE_BLOCK) * ROW_TILES, -1).astype(jnp.int32)
    n_used = pad_end[-1] // MOE_BLOCK
    starts = jnp.arange(n_blocks, dtype=jnp.int32) * MOE_BLOCK
    block_e = jnp.sum((pad_end[None, :] <= starts[:, None]).astype(jnp.int32), axis=1)
    block_e = jnp.minimum(block_e, N_EXPERTS - 1)
    last_e = block_e[jnp.maximum(n_used - 1, 0)]
    block_e = jnp.where(jnp.arange(n_blocks) < n_used, block_e, last_e)
    return block_e, n_used.reshape(1).astype(jnp.int32), (dest * ROW_TILES).astype(jnp.int32), zrow, n_slots


def _combine_rows(dest_ref, ys_hbm, ybuf, sem, tile, slot, tm, fn):
    def body(r, c):
        a0 = (tile * tm + r) * TOP_K
        for k in range(TOP_K):
            fn(_row_copy(ys_hbm, dest_ref[a0 + k], ybuf.at[slot, k], r * ROW_TILES, sem.at[slot]))
        return c

    lax.fori_loop(0, tm, body, 0, unroll=8)


def _combine_body(dest_ref, x_ref, w_ref, g_ref, ys_hbm, x2_ref, hn_ref, ybuf, sem, *, tm):
    i = pl.program_id(0)
    slot = i % 2

    @pl.when(i == 0)
    def _():
        _combine_rows(dest_ref, ys_hbm, ybuf, sem, i, slot, tm, lambda cp: cp.start())

    _combine_rows(dest_ref, ys_hbm, ybuf, sem, i, slot, tm, lambda cp: cp.wait())

    @pl.when(i + 1 < pl.num_programs(0))
    def _():
        _combine_rows(dest_ref, ys_hbm, ybuf, sem, i + 1, 1 - slot, tm, lambda cp: cp.start())

    w = w_ref[...]
    w0, w1 = w[:, 0:1], w[:, 1:2]
    ssq = None
    for a in range(ROW_TILES):
        rows = pl.ds(a, tm, stride=ROW_TILES)
        piece = x_ref[rows, :] + (w0 * ybuf[slot, 0, rows, :] + w1 * ybuf[slot, 1, rows, :])
        x2_ref[:, a * HEAD_DIM:(a + 1) * HEAD_DIM] = piece
        ssq = piece * piece if ssq is None else ssq + piece * piece
    inv = lax.rsqrt(jnp.sum(ssq, axis=-1, keepdims=True) * (1.0 / D_MODEL) + NORM_EPS)
    hn_ref[...] = (x2_ref[...] * inv * g_ref[...]).astype(hn_ref.dtype)


def combine(x1, ys, dest, wts, next_g, next_dtype, tm=256):
    t = x1.shape[0] // ROW_TILES
    d = D_MODEL
    row = pl.BlockSpec((tm, d), lambda i, dest: (i, 0))
    grid_spec = pltpu.PrefetchScalarGridSpec(
        num_scalar_prefetch=1,
        grid=(t // tm,),
        in_specs=[pl.BlockSpec((tm * ROW_TILES, HEAD_DIM), lambda i, dest: (i, 0)),
                  pl.BlockSpec((tm, HEAD_DIM), lambda i, dest: (i, 0)),
                  pl.BlockSpec((1, d), lambda i, dest: (0, 0)),
                  pl.BlockSpec(memory_space=pl.ANY)],
        out_specs=[row, row],
        scratch_shapes=[pltpu.VMEM((2, TOP_K, tm * ROW_TILES, HEAD_DIM), F32), pltpu.SemaphoreType.DMA((2,))],
    )
    return pl.pallas_call(
        functools.partial(_combine_body, tm=tm),
        grid_spec=grid_spec,
        out_shape=[jax.ShapeDtypeStruct((t, d), F32), jax.ShapeDtypeStruct((t, d), next_dtype)],
        compiler_params=_cparams(("arbitrary",)),
        name="moe_combine",
    )(dest, x1, wts, next_g.reshape(1, d), ys)


def _layer(x, h, cos_t, sin_t, layer, w_bd, w_all, conv_a_w, a_log, dt_bias, gdn_norm_g, conv_b_w, conv_b_b,
           ln_b_g, ln_b_b, pool_w, pool_scale, w_branch, w_out, norm2_g, wg, bg, we, be, w_up, w_down,
           next_g, next_dtype):
    t = x.shape[0]
    pool_bd = jnp.zeros((WIDTH, WIDTH), F32)
    for gi in range(len(POOL_WINDOWS)):
        pool_bd = lax.dynamic_update_slice(pool_bd, pool_w[gi], (gi * POOL_GROUP, gi * POOL_GROUP))
    w_router = jnp.pad(jnp.concatenate([wg, we], axis=1), ((0, 0), (0, HEAD_DIM - N_GROUPS - N_EXPERTS)))
    b_router = jnp.pad(jnp.concatenate([bg, be]), (0, HEAD_DIM - N_GROUPS - N_EXPERTS)).reshape(1, HEAD_DIM)

    u = project(h, w_all, layer, U_WIDTH)
    bd = project(h, w_bd, layer, HEAD_DIM)
    ya = gated_deltanet(u, bd, conv_a_w, a_log, dt_bias, gdn_norm_g)
    yb, yd = local_mixers(u, conv_b_w, conv_b_b, ln_b_g, ln_b_b, pool_bd.astype(BF16), pool_scale)
    yc = lse = None
    for gi, (_, dilation) in enumerate(ATT_PATTERNS):
        yc, lse = attention_group(u, cos_t, sin_t, gi, dilation, yc, lse)
    x1, rl = merge_project(h, ya, yb, yc, yd, w_all, w_branch, w_out, layer, x, norm2_g, w_router, b_router)
    eid, wts = route(rl)
    block_e, n_used, dest, zrow, n_slots = dispatch_tables(eid[:, :TOP_K], t)
    xs = sort_rows(x1, dest, zrow, n_used, n_slots)
    ys = experts(xs, norm2_g, w_up, w_down, layer, block_e, n_used)
    return combine(x1, ys, dest, wts, next_g, next_dtype)


def kernel(x, positions, norm1_g, w_in, conv_a_w, a_log, dt_bias, gdn_norm_g, conv_b_w, conv_b_b, ln_b_g, ln_b_b,
           pool_w, pool_scale, w_branch, w_out, norm2_g, router_group_w, router_group_b, router_expert_w,
           router_expert_b, w_up, w_down, final_norm_g):
    b_, s_, d = x.shape
    depth = w_in.shape[0]
    outs = []
    w_all = prepare_w_in(w_in)
    w_bd = beta_decay_weights(w_in)
    w_branch_b = w_branch.astype(BF16)
    w_out_b = w_out.astype(BF16)
    for bi in range(b_):
        xb = x[bi]
        cos_t, sin_t = rope_tables(positions[bi])
        h = rmsnorm(xb, norm1_g[0], BF16)
        for layer in range(depth):
            last = layer == depth - 1
            next_g = final_norm_g if last else norm1_g[layer + 1]
            xb, h = _layer(xb, h, cos_t, sin_t, layer, w_bd, w_all, conv_a_w[layer], a_log[layer], dt_bias[layer],
                           gdn_norm_g[layer], conv_b_w[layer], conv_b_b[layer], ln_b_g[layer], ln_b_b[layer],
                           pool_w[layer], pool_scale[layer], w_branch_b, w_out_b, norm2_g[layer],
                           router_group_w[layer], router_group_b[layer], router_expert_w[layer],
                           router_expert_b[layer], w_up, w_down, next_g, F32 if last else BF16)
        outs.append(h)
    return jnp.stack(outs, axis=0)
```

```python
import functools

import jax
import jax.numpy as jnp
from jax import lax
from jax.experimental import pallas as pl
from jax.experimental.pallas import tpu as pltpu

F32 = jnp.float32
BF16 = jnp.bfloat16

D_MODEL = 2048
N_HEADS = 6
HEAD_DIM = 128
WIDTH = N_HEADS * HEAD_DIM
GDN_CONV = 4
GDN_CHUNK = 64
GDN_SUPER = 256
CONF_CONV = 31
ATT_PATTERNS = ((128, 1), (512, 4), (2048, 16))
ATT_BLOCK = 128
ATT_UNITS_IN_FLIGHT = 6
ATT_HEADS_PER_STEP = {1: 6, 4: 1, 16: 1}
ROPE_THETA = 10000.0
POOL_WINDOWS = (2, 4, 8, 16)
POOL_GROUP = 192
N_BRANCHES = 4
N_GROUPS = 8
PER_GROUP = 8
N_EXPERTS = 64
TOP_K = 2
EXPERT_HIDDEN = 512
MOE_BLOCK = 128
MOE_SPARE = 2 * MOE_BLOCK
ROW_TILES = D_MODEL // HEAD_DIM
LOCAL_ROWS = 32
NORM_EPS = 1e-6
LN_EPS = 1e-5

U_QKV_A = 0
U_Z_A = 2304
U_B = 3072
U_C = 4608
U_D = 11520
U_WIDTH = 12288
W_BETA = 3072
W_UB = 3084
W_GATE = 12300

VMEM_LIMIT = 56 * 1024 * 1024


def _cparams(semantics, vmem=VMEM_LIMIT):
    return pltpu.CompilerParams(dimension_semantics=semantics, vmem_limit_bytes=vmem)


def _sigmoid(x):
    return 1.0 / (1.0 + jnp.exp(-x))


def _silu(x):
    return x * _sigmoid(x)


def _dot(a, b):
    return jnp.dot(a, b, preferred_element_type=F32)


def _dot_nt(a, b):
    return lax.dot_general(a, b, (((1,), (1,)), ((), ())), preferred_element_type=F32)


def _dot_tn(a, b):
    return lax.dot_general(a, b, (((0,), (0,)), ((), ())), preferred_element_type=F32)


def _rmsnorm_body(x_ref, g_ref, o_ref):
    x = x_ref[...]
    y = x * lax.rsqrt(jnp.mean(x * x, axis=-1, keepdims=True) + NORM_EPS) * g_ref[...]
    o_ref[...] = y.astype(o_ref.dtype)


def rmsnorm(x, g, out_dtype, tm=512):
    t, d = x.shape
    return pl.pallas_call(
        _rmsnorm_body,
        grid=(t // tm,),
        in_specs=[pl.BlockSpec((tm, d), lambda i: (i, 0)), pl.BlockSpec((1, d), lambda i: (0, 0))],
        out_specs=pl.BlockSpec((tm, d), lambda i: (i, 0)),
        out_shape=jax.ShapeDtypeStruct((t, d), out_dtype),
        compiler_params=_cparams(("parallel",)),
        name="rmsnorm",
    )(x, g.reshape(1, d))


def _wprep_copy(wt_hbm, buf, sem, j, slot, *, tn, n_plain, shift):
    row0 = j * tn + jnp.where(j >= n_plain, shift, 0)
    return pltpu.make_async_copy(wt_hbm.at[pl.ds(row0, tn)], buf.at[slot], sem.at[slot])


def _wprep_body(wt_hbm, o_ref, buf, sem, *, tn, n_plain, shift):
    j = pl.program_id(0)
    slot = j % 2
    cp = functools.partial(_wprep_copy, wt_hbm, buf, sem, tn=tn, n_plain=n_plain, shift=shift)

    @pl.when(j == 0)
    def _():
        cp(j, slot).start()

    cp(j, slot).wait()

    @pl.when(j + 1 < pl.num_programs(0))
    def _():
        cp(j + 1, 1 - slot).start()

    for l in range(o_ref.shape[0]):
        o_ref[l] = buf[slot, :, l, :].T.astype(BF16)


def prepare_w_in(w_in, tn=256):
    depth, k, n = w_in.shape
    w_t = jnp.transpose(w_in, (2, 0, 1))
    n_out = W_BETA + (n - W_UB)
    return pl.pallas_call(
        functools.partial(_wprep_body, tn=tn, n_plain=W_BETA // tn, shift=W_UB - W_BETA),
        grid=(n_out // tn,),
        in_specs=[pl.BlockSpec(memory_space=pl.ANY)],
        out_specs=pl.BlockSpec((depth, k, tn), lambda j: (0, 0, j)),
        out_shape=jax.ShapeDtypeStruct((depth, k, n_out), BF16),
        scratch_shapes=[pltpu.VMEM((2, tn, depth, k), F32), pltpu.SemaphoreType.DMA((2,))],
        compiler_params=_cparams(("arbitrary",)),
        name="prepare_w_in",
    )(w_t)


def _bd_weight_body(w_ref, o_ref):
    for l in range(o_ref.shape[0]):
        o_ref[l] = w_ref[:, l, :].T.astype(BF16)


def beta_decay_weights(w_in):
    depth, k, _ = w_in.shape
    w_t = jnp.transpose(w_in, (2, 0, 1))
    return pl.pallas_call(
        _bd_weight_body,
        grid=(1,),
        in_specs=[pl.BlockSpec((HEAD_DIM, depth, k), lambda i: (W_BETA // HEAD_DIM, 0, 0))],
        out_specs=pl.BlockSpec((depth, k, HEAD_DIM), lambda i: (0, 0, 0)),
        out_shape=jax.ShapeDtypeStruct((depth, k, HEAD_DIM), BF16),
        compiler_params=_cparams(("arbitrary",)),
        name="beta_decay_weights",
    )(w_t)


def _matmul_body(h_ref, w_ref, o_ref):
    o_ref[...] = _dot(h_ref[...], w_ref[...].astype(BF16)).astype(o_ref.dtype)


def project(h, w, layer, n, col0=0, tm=1024, tn=768):
    t, k = h.shape
    tn = min(tn, n)
    return pl.pallas_call(
        _matmul_body,
        grid=(n // tn, t // tm),
        in_specs=[pl.BlockSpec((tm, k), lambda j, i: (i, 0)),
                  pl.BlockSpec((None, k, tn), lambda j, i: (layer, 0, col0 // tn + j))],
        out_specs=pl.BlockSpec((tm, tn), lambda j, i: (i, j)),
        out_shape=jax.ShapeDtypeStruct((t, n), F32),
        compiler_params=_cparams(("parallel", "parallel")),
        name="in_proj",
    )(h, w)


def _rope(x, cos, sin):
    return x * cos + pltpu.roll(x, HEAD_DIM // 2, axis=1) * sin


def _matmul_rope_body(h_ref, w_ref, cos_ref, sin_ref, o_ref, *, q_tiles, k_tiles):
    j = pl.program_id(0)
    acc = _dot(h_ref[...], w_ref[...])
    is_q = (j >= q_tiles[0]) & (j < q_tiles[1])
    is_k = (j >= k_tiles[0]) & (j < k_tiles[1])

    @pl.when(jnp.logical_not(is_q | is_k))
    def _():
        o_ref[...] = acc

    def rotated(scale):
        cos, sin = cos_ref[...], sin_ref[...]
        for c in range(o_ref.shape[1] // HEAD_DIM):
            sl = slice(c * HEAD_DIM, (c + 1) * HEAD_DIM)
            o_ref[:, sl] = _rope(acc[:, sl], cos, sin) * scale

    @pl.when(is_q)
    def _():
        rotated(HEAD_DIM ** -0.5)

    @pl.when(is_k)
    def _():
        rotated(1.0)


def project_mixers(h, w, layer, cos_t, sin_t, tm=1024, tn=WIDTH):
    t, k = h.shape
    q0 = U_C // tn
    nq = 3 * WIDTH // tn
    tab = pl.BlockSpec((tm, HEAD_DIM), lambda j, i: (i, 0))
    return pl.pallas_call(
        functools.partial(_matmul_rope_body, q_tiles=(q0, q0 + nq), k_tiles=(q0 + nq, q0 + 2 * nq)),
        grid=(U_WIDTH // tn, t // tm),
        in_specs=[pl.BlockSpec((tm, k), lambda j, i: (i, 0)),
                  pl.BlockSpec((None, k, tn), lambda j, i: (layer, 0, j)), tab, tab],
        out_specs=pl.BlockSpec((tm, tn), lambda j, i: (i, j)),
        out_shape=jax.ShapeDtypeStruct((t, U_WIDTH), F32),
        compiler_params=_cparams(("parallel", "parallel")),
        name="in_proj",
    )(h, w, cos_t, sin_t)


def _rope_body(pos_ref, inv_ref, cos_ref, sin_ref):
    ang = pos_ref[...].astype(F32) * inv_ref[...]
    lane = lax.broadcasted_iota(jnp.int32, ang.shape, 1)
    cos_ref[...] = jnp.cos(ang)
    s = jnp.sin(ang)
    sin_ref[...] = jnp.where(lane < HEAD_DIM // 2, -s, s)


def rope_tables(pos, tm=1024):
    t = pos.shape[0]
    half = HEAD_DIM // 2
    inv = ROPE_THETA ** (-jnp.arange(half, dtype=F32) * 2.0 / HEAD_DIM)
    inv = jnp.concatenate([inv, inv]).reshape(1, HEAD_DIM)
    spec = pl.BlockSpec((tm, HEAD_DIM), lambda i: (i, 0))
    return pl.pallas_call(
        _rope_body,
        grid=(t // tm,),
        in_specs=[pl.BlockSpec((tm, 1), lambda i: (i, 0)), pl.BlockSpec((1, HEAD_DIM), lambda i: (0, 0))],
        out_specs=[spec, spec],
        out_shape=[jax.ShapeDtypeStruct((t, HEAD_DIM), F32)] * 2,
        compiler_params=_cparams(("parallel",)),
        name="rope_tables",
    )(pos.reshape(t, 1), inv)


def _attn_body(*refs, first, d, hw):
    if first:
        q_ref, kc_ref, kp_ref, vc_ref, vp_ref, o_out, lse_out = refs
        o_in = lse_in = None
    else:
        q_ref, kc_ref, kp_ref, vc_ref, vp_ref, o_in, lse_in, o_out, lse_out = refs
    n = pl.program_id(0)
    blk = ATT_BLOCK
    row = lax.broadcasted_iota(jnp.int32, (blk, blk), 0)
    col = lax.broadcasted_iota(jnp.int32, (blk, blk), 1)
    in_cur = col <= row
    has_prev = n > 0
    neg = -jnp.inf
    units = [(pl.ds(r, blk, stride=d) if d > 1 else pl.ds(0, blk), slice(h * HEAD_DIM, (h + 1) * HEAD_DIM))
             for r in range(d) for h in range(hw)]
    for g0 in range(0, len(units), ATT_UNITS_IN_FLIGHT):
        group = units[g0:g0 + ATT_UNITS_IN_FLIGHT]
        scores = []
        for rows, sl in group:
            q = q_ref[rows, sl]
            kp = kp_ref[rows, sl]
            qb = q.astype(BF16)
            s_c = _dot_nt(qb, kc_ref[rows, sl].astype(BF16))
            s_p = _dot_nt(qb, kp.astype(BF16))
            s_d = jnp.where(has_prev, jnp.sum(q * kp, axis=-1, keepdims=True), neg)
            scores.append((jnp.where(in_cur, s_c, jnp.where(has_prev, s_p, neg)), s_d))
        probs = []
        for (rows, sl), (s_w, s_d) in zip(group, scores):
            m = jnp.maximum(jnp.max(s_w, axis=-1, keepdims=True), s_d)
            m_old = None
            if not first:
                m_old = lse_in[rows, sl][:, 0:1]
                m = jnp.maximum(m, m_old)
            p_w = jnp.exp(s_w - m)
            p_d = jnp.exp(s_d - m)
            probs.append((m, m_old, p_w, p_d, jnp.sum(p_w, axis=-1, keepdims=True) + p_d))
        for (rows, sl), (m, m_old, p_w, p_d, l) in zip(group, probs):
            vp = vp_ref[rows, sl]
            acc = (_dot(jnp.where(in_cur, p_w, 0.0).astype(BF16), vc_ref[rows, sl].astype(BF16))
                   + _dot(jnp.where(in_cur, 0.0, p_w).astype(BF16), vp.astype(BF16)) + p_d * vp)
            if not first:
                w_old = jnp.exp(m_old - m)
                l = l + w_old
                acc = acc + w_old * o_in[rows, sl]
            o_out[rows, sl] = acc / l
            lse_out[rows, sl] = jnp.broadcast_to(m + jnp.log(l), (blk, HEAD_DIM))


def attention_group(u, group, dilation, o_prev, lse_prev):
    t = u.shape[0]
    d = dilation
    span = ATT_BLOCK * d
    hw = ATT_HEADS_PER_STEP[d]
    cw = hw * HEAD_DIM
    first = o_prev is None
    qoff = (U_C + group * WIDTH) // cw
    koff = (U_C + 3 * WIDTH + group * WIDTH) // cw
    voff = (U_C + 6 * WIDTH + group * WIDTH) // cw

    def cur(off):
        return pl.BlockSpec((span, cw), lambda n, g: (n, off + g))

    def prev(off):
        return pl.BlockSpec((span, cw), lambda n, g: (jnp.maximum(n - 1, 0), off + g))

    o_spec = cur(0)
    in_specs = [cur(qoff), cur(koff), prev(koff), cur(voff), prev(voff)]
    args = [u, u, u, u, u]
    if not first:
        in_specs += [o_spec, o_spec]
        args += [o_prev, lse_prev]
    return pl.pallas_call(
        functools.partial(_attn_body, first=first, d=d, hw=hw),
        grid=(t // span, N_HEADS // hw),
        in_specs=in_specs,
        out_specs=[o_spec, o_spec],
        out_shape=[jax.ShapeDtypeStruct((t, WIDTH), F32), jax.ShapeDtypeStruct((t, WIDTH), F32)],
        compiler_params=_cparams(("parallel", "parallel")),
        name=f"attn_d{d}",
    )(*args)


def _local_body(ub_ref, ud_ref, cw_ref, cb_ref, lg_ref, lb_ref, pw_ref, ps_ref, yb_ref, yd_ref, extb, extd, pool_buf, *, tm):
    i = pl.program_id(0)
    hb, hd = 32, 16

    @pl.when(i == 0)
    def _():
        extb[0:hb, :] = jnp.zeros((hb, WIDTH), F32)
        extd[0:hd, :] = jnp.zeros((hd, WIDTH), F32)

    @pl.when(i > 0)
    def _():
        extb[0:hb, :] = extb[tm:tm + hb, :]
        extd[0:hd, :] = extd[tm:tm + hd, :]

    ub = ub_ref[...]
    extb[hb:hb + tm, :] = ub[:, :WIDTH] * _sigmoid(ub[:, WIDTH:])
    extd[hd:hd + tm, :] = ud_ref[...]

    rc = LOCAL_ROWS
    ch = lax.broadcasted_iota(jnp.int32, (rc, WIDTH), 1)
    rows = lax.broadcasted_iota(jnp.int32, (rc, WIDTH), 0)
    for c0 in range(0, tm, rc):
        acc = None
        for s in range(8):
            win = rc + (8 if s else 0)
            part = None
            for j in range(CONF_CONV):
                off = hb - (CONF_CONV - 1) + j
                if off % 8 != s:
                    continue
                base = c0 + off - s
                term = cw_ref[j:j + 1, :] * extb[base:base + win, :]
                part = term if part is None else part + term
            if part is None:
                continue
            part = part[s:s + rc, :]
            acc = part if acc is None else acc + part
        cb = acc + cb_ref[...]
        mu = jnp.mean(cb, axis=-1, keepdims=True)
        cen = cb - mu
        var = jnp.mean(cen * cen, axis=-1, keepdims=True)
        yb = _silu(cen * lax.rsqrt(var + LN_EPS) * lg_ref[...] + lb_ref[...])
        yb_ref[c0:c0 + rc, :] = yb.astype(yb_ref.dtype)

        xd = extd[c0 + hd:c0 + hd + rc, :]
        tpos = i * tm + c0 + rows + 1
        run = xd
        pooled = None
        shift = 1
        for gi, w in enumerate(POOL_WINDOWS):
            while shift < w:
                run = run + extd[c0 + hd - shift:c0 + hd - shift + rc, :]
                shift += 1
            val = run / jnp.minimum(tpos, w).astype(F32) - xd
            pooled = val if pooled is None else jnp.where(ch >= gi * POOL_GROUP, val, pooled)
        pool_buf[c0:c0 + rc, :] = pooled.astype(pool_buf.dtype)
    yd = _dot(pool_buf[...], pw_ref[...]) * ps_ref[...]
    yd_ref[...] = yd.astype(yd_ref.dtype)


def local_mixers(u, conv_w, conv_b, ln_g, ln_b, pool_w_bd, pool_scale, tm=256):
    t = u.shape[0]
    row = lambda a: a.reshape(1, WIDTH)
    vec = pl.BlockSpec((1, WIDTH), lambda i: (0, 0))
    out = pl.BlockSpec((tm, WIDTH), lambda i: (i, 0))
    return pl.pallas_call(
        functools.partial(_local_body, tm=tm),
        grid=(t // tm,),
        in_specs=[pl.BlockSpec((tm, 2 * WIDTH), lambda i: (i, U_B // (2 * WIDTH))),
                  pl.BlockSpec((tm, WIDTH), lambda i: (i, U_D // WIDTH)),
                  pl.BlockSpec((CONF_CONV, WIDTH), lambda i: (0, 0)), vec, vec, vec,
                  pl.BlockSpec((WIDTH, WIDTH), lambda i: (0, 0)), vec],
        out_specs=[out, out],
        out_shape=[jax.ShapeDtypeStruct((t, WIDTH), BF16)] * 2,
        scratch_shapes=[pltpu.VMEM((tm + 32, WIDTH), F32), pltpu.VMEM((tm + 16, WIDTH), F32),
                        pltpu.VMEM((tm, WIDTH), BF16)],
        compiler_params=_cparams(("arbitrary",)),
        name="conv_pool",
    )(u, u, conv_w, row(conv_b), row(ln_g), row(ln_b), pool_w_bd, row(pool_scale))


def _gdn_body(qkv_ref, z_ref, bd_ref, cw_ref, alog_ref, dtb_ref, gn_ref, ya_ref, ext, state):
    s = pl.program_id(0)
    sup, c = GDN_SUPER, GDN_CHUNK
    halo = 8

    @pl.when(s == 0)
    def _():
        ext[0:halo, :] = jnp.zeros((halo, 3 * WIDTH), F32)
        state[...] = jnp.zeros(state.shape, F32)

    @pl.when(s > 0)
    def _():
        ext[0:halo, :] = ext[sup:sup + halo, :]

    ext[halo:halo + sup, :] = qkv_ref[...]
    acc = jnp.zeros((sup, 3 * WIDTH), F32)
    for j in range(GDN_CONV):
        off = halo - (GDN_CONV - 1) + j
        acc = acc + cw_ref[j:j + 1, :] * ext[off:off + sup, :]
    qkv = _silu(acc)

    bd = bd_ref[...]
    beta_all = _sigmoid(bd)
    xs = bd + dtb_ref[...]
    softplus = jnp.maximum(xs, 0.0) + jnp.log(1.0 + jnp.exp(-jnp.abs(xs)))
    g_all = -jnp.exp(alog_ref[...]) * softplus
    rowi = lax.broadcasted_iota(jnp.int32, (sup, HEAD_DIM), 0)
    in_chunk = rowi & (c - 1)
    gc = g_all
    sh = 1
    while sh < c:
        gc = gc + jnp.where(in_chunk >= sh, pltpu.roll(gc, sh, axis=0), 0.0)
        sh *= 2
    gct = gc.T

    ri = lax.broadcasted_iota(jnp.int32, (sup, sup), 0)
    ci = lax.broadcasted_iota(jnp.int32, (sup, sup), 1)
    same = (ri & -c) == (ci & -c)
    incl = same & (ri >= ci)
    strict = same & (ri > ci)
    eye = (ri == ci).astype(F32)

    heads = range(N_HEADS)
    hs = [slice(h * HEAD_DIM, (h + 1) * HEAD_DIM) for h in heads]
    prep = []
    for h in heads:
        q = qkv[:, hs[h]]
        k = qkv[:, WIDTH + h * HEAD_DIM:WIDTH + (h + 1) * HEAD_DIM]
        v = qkv[:, 2 * WIDTH + h * HEAD_DIM:2 * WIDTH + (h + 1) * HEAD_DIM]
        q = q * lax.rsqrt(jnp.sum(q * q, axis=-1, keepdims=True) + 1e-6) * (HEAD_DIM ** -0.5)
        k = k * lax.rsqrt(jnp.sum(k * k, axis=-1, keepdims=True) + 1e-6)
        b_col = beta_all[:, h:h + 1]
        gc_col = gc[:, N_HEADS + h:N_HEADS + h + 1]
        gc_row = gct[N_HEADS + h:N_HEADS + h + 1, :]
        dm = jnp.exp(jnp.where(incl, gc_col - gc_row, -jnp.inf))
        kb = k.astype(BF16)
        nmat = jnp.where(strict, -(b_col * _dot_nt(kb, kb) * dm), 0.0)
        prep.append((q, k, v, b_col, gc_col, dm, kb, nmat))

    pmats = [eye + p[7] for p in prep]
    mbs = [p[7].astype(BF16) for p in prep]
    step = 2
    while step < c:
        m2s = [_dot(mb, mb).astype(BF16) for mb in mbs]
        pmats = [pm + _dot(pm.astype(BF16), m2) for pm, m2 in zip(pmats, m2s)]
        mbs = m2s
        step *= 2

    chunked = []
    for h in heads:
        q, k, v, b_col, gc_col, dm, kb, _ = prep[h]
        eg = jnp.exp(gc_col)
        rhs = jnp.concatenate([v * b_col, k * (b_col * eg)], axis=1).astype(BF16)
        sol = _dot(pmats[h].astype(BF16), rhs)
        qk_b = (_dot_nt(q.astype(BF16), kb) * dm).astype(BF16)
        q_dec = (q * eg).astype(BF16)
        gl_col = jnp.concatenate(
            [jnp.broadcast_to(gc_col[(n + 1) * c - 1:(n + 1) * c, :], (c, 1)) for n in range(sup // c)], axis=0)
        k_dec = k * jnp.exp(gl_col - gc_col)
        chunked.append((sol[:, :HEAD_DIM], sol[:, HEAD_DIM:].astype(BF16), qk_b, q_dec, k_dec, gc_col))

    states = [state[h] for h in heads]
    outs = [[] for _ in heads]
    for n in range(sup // c):
        rs = slice(n * c, (n + 1) * c)
        for h in heads:
            u_c, w_b, qk_b, q_dec, k_dec, gc_col = chunked[h]
            sb = states[h].astype(BF16)
            v_new = u_c[rs] - _dot(w_b[rs], sb)
            vb = v_new.astype(BF16)
            outs[h].append(_dot(q_dec[rs], sb) + _dot(qk_b[rs, rs], vb))
            g_last = jnp.exp(gc_col[(n + 1) * c - 1:(n + 1) * c, :])
            states[h] = states[h] * g_last + _dot(k_dec[rs].T.astype(BF16), vb)

    for h in heads:
        state[h] = states[h]
        o = jnp.concatenate(outs[h], axis=0)
        o = o * lax.rsqrt(jnp.mean(o * o, axis=-1, keepdims=True) + NORM_EPS) * gn_ref[...]
        ya_ref[:, hs[h]] = (o * _silu(z_ref[:, hs[h]])).astype(ya_ref.dtype)


def gated_deltanet(u, bd, conv_w, a_log, dt_bias, norm_g):
    t = u.shape[0]
    sup = GDN_SUPER
    pad = jnp.zeros((N_HEADS,), F32)
    lane_row = lambda a: jnp.concatenate([pad, a, jnp.zeros((HEAD_DIM - 2 * N_HEADS,), F32)]).reshape(1, HEAD_DIM)
    vec = pl.BlockSpec((1, HEAD_DIM), lambda s: (0, 0))
    return pl.pallas_call(
        _gdn_body,
        grid=(t // sup,),
        in_specs=[pl.BlockSpec((sup, 3 * WIDTH), lambda s: (s, U_QKV_A // (3 * WIDTH))),
                  pl.BlockSpec((sup, WIDTH), lambda s: (s, U_Z_A // WIDTH)),
                  pl.BlockSpec((sup, HEAD_DIM), lambda s: (s, 0)),
                  pl.BlockSpec((GDN_CONV, 3 * WIDTH), lambda s: (0, 0)), vec, vec, vec],
        out_specs=pl.BlockSpec((sup, WIDTH), lambda s: (s, 0)),
        out_shape=jax.ShapeDtypeStruct((t, WIDTH), BF16),
        scratch_shapes=[pltpu.VMEM((sup + 8, 3 * WIDTH), F32), pltpu.VMEM((N_HEADS, HEAD_DIM, HEAD_DIM), F32)],
        compiler_params=_cparams(("arbitrary",)),
        name="gated_deltanet",
    )(u, u, bd, conv_w, lane_row(a_log), lane_row(dt_bias), norm_g.reshape(1, HEAD_DIM))


def _merge_body(h_ref, ya_ref, yb_ref, yc_ref, yd_ref, g0_ref, g1_ref, g2_ref, g3_ref,
                b0_ref, b1_ref, b2_ref, b3_ref, wo_ref, x_ref, n2_ref, wr_ref, br_ref,
                x1_ref, rl_ref, acc):
    j = pl.program_id(1)

    @pl.when(j == 0)
    def _():
        acc[...] = jnp.zeros(acc.shape, F32)

    hb = h_ref[...]
    merged = None
    for y_ref, g_ref, b_ref in ((ya_ref, g0_ref, b0_ref), (yb_ref, g1_ref, b1_ref),
                                (yc_ref, g2_ref, b2_ref), (yd_ref, g3_ref, b3_ref)):
        gate = _sigmoid(_dot(hb, g_ref[...]))
        term = gate * _dot(y_ref[...].astype(BF16), b_ref[...])
        merged = term if merged is None else merged + term
    acc[...] += _dot(merged.astype(BF16), wo_ref[...])

    @pl.when(j == pl.num_programs(1) - 1)
    def _():
        x1 = x_ref[...] + acc[...]
        tm = x1.shape[0]
        for a in range(ROW_TILES):
            x1_ref[pl.ds(a, tm, stride=ROW_TILES), :] = x1[:, a * HEAD_DIM:(a + 1) * HEAD_DIM]
        h2 =x1 * lax.rsqrt(jnp.mean(x1 * x1, axis=-1, keepdims=True) + NORM_EPS) * n2_ref[...]
        rl_ref[...] = jnp.dot(h2, wr_ref[...], preferred_element_type=F32,
                              precision=lax.Precision.HIGHEST) + br_ref[...]


def merge_project(h, ya, yb, yc, yd, w_all, w_branch, w_out, layer, x, norm2_g, w_router, b_router, tm=512, tn=256):
    t = h.shape[0]
    d = D_MODEL
    nj = d // tn
    g0 = U_WIDTH // tn
    row = pl.BlockSpec((tm, d), lambda i, j: (i, 0))
    ysp = pl.BlockSpec((tm, WIDTH), lambda i, j: (i, 0))
    gate_spec = lambda k: pl.BlockSpec((None, d, tn), lambda i, j: (layer, 0, g0 + k * nj + j))
    br_spec = lambda k: pl.BlockSpec((None, None, WIDTH, tn), lambda i, j: (layer, k, 0, j))
    lanes = pl.BlockSpec((tm, HEAD_DIM), lambda i, j: (i, 0))
    w_gate = w_all
    return pl.pallas_call(
        _merge_body,
        grid=(t // tm, nj),
        in_specs=[row, ysp, ysp, ysp, ysp] + [gate_spec(k) for k in range(4)] + [br_spec(k) for k in range(4)]
        + [pl.BlockSpec((None, tn, d), lambda i, j: (layer, j, 0)), row, pl.BlockSpec((1, d), lambda i, j: (0, 0)),
           pl.BlockSpec((d, HEAD_DIM), lambda i, j: (0, 0)), pl.BlockSpec((1, HEAD_DIM), lambda i, j: (0, 0))],
        out_specs=[pl.BlockSpec((tm * ROW_TILES, HEAD_DIM), lambda i, j: (i, 0)), lanes],
        out_shape=[jax.ShapeDtypeStruct((t * ROW_TILES, HEAD_DIM), F32), jax.ShapeDtypeStruct((t, HEAD_DIM), F32)],
        scratch_shapes=[pltpu.VMEM((tm, d), F32)],
        compiler_params=_cparams(("parallel", "arbitrary")),
        name="merge_project",
    )(h, ya, yb, yc, yd, w_gate, w_gate, w_gate, w_gate, w_branch, w_branch, w_branch, w_branch,
      w_out, x, norm2_g.reshape(1, d), w_router, b_router)


def _router_body(rl_ref, eid_ref, wts_ref):
    rl = rl_ref[...]
    lane_i = lax.broadcasted_iota(jnp.int32, rl.shape, 1)
    lane = lane_i.astype(F32)
    neg = -jnp.inf
    big = 1e9
    gl = jnp.where(lane < N_GROUPS, rl, neg)
    gmax = jnp.max(gl, axis=-1, keepdims=True)
    gi = jnp.min(jnp.where(gl == gmax, lane, big), axis=-1, keepdims=True)
    gp = 1.0 / jnp.sum(jnp.exp(gl - gmax), axis=-1, keepdims=True)
    lo = N_GROUPS + gi * PER_GROUP
    el = jnp.where((lane >= lo) & (lane < lo + PER_GROUP), rl, neg)
    e1 = jnp.max(el, axis=-1, keepdims=True)
    i1 = jnp.min(jnp.where(el == e1, lane, big), axis=-1, keepdims=True)
    el2 = jnp.where(lane == i1, neg, el)
    e2 = jnp.max(el2, axis=-1, keepdims=True)
    i2 = jnp.min(jnp.where(el2 == e2, lane, big), axis=-1, keepdims=True)
    esum = jnp.sum(jnp.exp(el - e1), axis=-1, keepdims=True)
    p1 = 1.0 / esum
    p2 = jnp.exp(e2 - e1) / esum
    w1 = gp * p1 / (p1 + p2)
    w2 = gp * p2 / (p1 + p2)
    eid = jnp.where(lane_i == 0, i1 - N_GROUPS, jnp.where(lane_i == 1, i2 - N_GROUPS, 0.0))
    eid_ref[...] = eid.astype(jnp.int32)
    wts_ref[...] = jnp.where(lane_i == 0, w1, jnp.where(lane_i == 1, w2, 0.0))


def route(rl, tm=512):
    t = rl.shape[0]
    spec = pl.BlockSpec((tm, HEAD_DIM), lambda i: (i, 0))
    return pl.pallas_call(
        _router_body,
        grid=(t // tm,),
        in_specs=[spec],
        out_specs=[spec, spec],
        out_shape=[jax.ShapeDtypeStruct((t, HEAD_DIM), jnp.int32), jax.ShapeDtypeStruct((t, HEAD_DIM), F32)],
        compiler_params=_cparams(("parallel",)),
        name="router",
    )(rl)


def _row_copy(src, src_row, dst, dst_row, sem):
    return pltpu.make_async_copy(src.at[pl.ds(pl.multiple_of(src_row, ROW_TILES), ROW_TILES), :],
                                 dst.at[pl.ds(pl.multiple_of(dst_row, ROW_TILES), ROW_TILES), :], sem)


def _zero_copy(zbuf, zrow_ref, xs_hbm, zsem, e):
    start = pl.multiple_of(jnp.maximum(zrow_ref[e], 0), MOE_BLOCK * ROW_TILES)
    return pltpu.make_async_copy(zbuf, xs_hbm.at[pl.ds(start, MOE_BLOCK * ROW_TILES), :], zsem)


def _zero_block(zbuf, xs_hbm, zsem, blk):
    start = pl.multiple_of(blk * (MOE_BLOCK * ROW_TILES), MOE_BLOCK * ROW_TILES)
    return pltpu.make_async_copy(zbuf, xs_hbm.at[pl.ds(start, MOE_BLOCK * ROW_TILES), :], zsem)


def _sort_rows_body(dest_ref, zrow_ref, nused_ref, x_ref, xs_hbm, zbuf, sem, zsem, *, tm):
    i = pl.program_id(0)
    sem, zsem = sem.at[0], zsem.at[0]

    @pl.when(i == 0)
    def _():
        zbuf[...] = jnp.zeros(zbuf.shape, F32)
        n_blocks = xs_hbm.shape[0] // (MOE_BLOCK * ROW_TILES)
        for e in range(N_EXPERTS):
            @pl.when(zrow_ref[e] >= 0)
            def _():
                _zero_copy(zbuf, zrow_ref, xs_hbm, zsem, e).start()

        def z_start(blk, c):
            _zero_block(zbuf, xs_hbm, zsem, blk).start()
            return c

        def z_wait(blk, c):
            _zero_block(zbuf, xs_hbm, zsem, blk).wait()
            return c

        lax.fori_loop(nused_ref[0], n_blocks, z_start, 0)
        for e in range(N_EXPERTS):
            @pl.when(zrow_ref[e] >= 0)
            def _():
                _zero_copy(zbuf, zrow_ref, xs_hbm, zsem, e).wait()
        lax.fori_loop(nused_ref[0], n_blocks, z_wait, 0)

    def copies(r):
        a0 = (i * tm + r) * TOP_K
        return [_row_copy(x_ref, r * ROW_TILES, xs_hbm, dest_ref[a0 + k], sem) for k in range(TOP_K)]

    def start(r, c):
        for cp in copies(r):
            cp.start()
        return c

    def wait(r, c):
        for cp in copies(r):
            cp.wait()
        return c

    lax.fori_loop(0, tm, start, 0, unroll=8)
    lax.fori_loop(0, tm, wait, 0, unroll=8)


def sort_rows(x1, dest, zrow, n_used, n_slots, tm=512):
    t = x1.shape[0] // ROW_TILES
    grid_spec = pltpu.PrefetchScalarGridSpec(
        num_scalar_prefetch=3,
        grid=(t // tm,),
        in_specs=[pl.BlockSpec((tm * ROW_TILES, HEAD_DIM), lambda i, dest, zrow, nu: (i, 0))],
        out_specs=pl.BlockSpec(memory_space=pl.ANY),
        scratch_shapes=[pltpu.VMEM((MOE_BLOCK * ROW_TILES, HEAD_DIM), F32),
                        pltpu.SemaphoreType.DMA((1,)), pltpu.SemaphoreType.DMA((1,))],
    )
    return pl.pallas_call(
        functools.partial(_sort_rows_body, tm=tm),
        grid_spec=grid_spec,
        out_shape=jax.ShapeDtypeStruct((n_slots * ROW_TILES, HEAD_DIM), F32),
        compiler_params=_cparams(("arbitrary",)),
        name="moe_sort_rows",
    )(dest, zrow, n_used, x1)


def _weight_copies(wup_hbm, wdn_hbm, wup_f, wdn_f, wsem, layer, e, slot):
    return (pltpu.make_async_copy(wup_hbm.at[layer, e], wup_f.at[slot], wsem.at[0, slot]),
            pltpu.make_async_copy(wdn_hbm.at[layer, e], wdn_f.at[slot], wsem.at[1, slot]))


def _expert_body(be_ref, nused_ref, nxt_ref, par_ref, xs_ref, g_ref, wup_hbm, wdn_hbm, ys_ref,
                 wup_f, wdn_f, wup_b, wdn_b, wsem, *, layer):
    b = pl.program_id(0)
    copies = functools.partial(_weight_copies, wup_hbm, wdn_hbm, wup_f, wdn_f, wsem, layer)

    @pl.when(b == 0)
    def _():
        for cp in copies(be_ref[0], par_ref[0]):
            cp.start()

    @pl.when(b >= nused_ref[0])
    def _():
        ys_ref[...] = jnp.zeros(ys_ref.shape, F32)

    @pl.when(b < nused_ref[0])
    def _():
        changed = jnp.logical_or(b == 0, be_ref[b] != be_ref[jnp.maximum(b - 1, 0)])

        @pl.when(changed)
        def _():
            slot = par_ref[b]
            for cp in copies(be_ref[b], slot):
                cp.wait()

            @pl.when(nxt_ref[b] >= 0)
            def _():
                for cp in copies(nxt_ref[b], 1 - slot):
                    cp.start()

            wup_b[...] = wup_f[slot].astype(BF16)
            wdn_b[...] = wdn_f[slot].astype(BF16)

        pieces = [xs_ref[pl.ds(a, MOE_BLOCK, stride=ROW_TILES), :] for a in range(ROW_TILES)]
        ssq = pieces[0] * pieces[0]
        for p in pieces[1:]:
            ssq = ssq + p * p
        inv = lax.rsqrt(jnp.sum(ssq, axis=-1, keepdims=True) * (1.0 / D_MODEL) + NORM_EPS)
        xn = jnp.concatenate([(p * inv * g_ref[:, a * HEAD_DIM:(a + 1) * HEAD_DIM]).astype(BF16)
                              for a, p in enumerate(pieces)], axis=1)
        gu = _dot(xn, wup_b[...])
        act = (_silu(gu[:, :EXPERT_HIDDEN]) * gu[:, EXPERT_HIDDEN:]).astype(BF16)
        y = _dot(act, wdn_b[...])
        for a in range(ROW_TILES):
            ys_ref[pl.ds(a, MOE_BLOCK, stride=ROW_TILES), :] = y[:, a * HEAD_DIM:(a + 1) * HEAD_DIM]


def experts(xs, norm_g, w_up, w_down, layer, block_e, n_used, next_e, parity):
    n_blocks = block_e.shape[0]
    d = D_MODEL
    rows = lambda b, be, nu, nx, pa: (jnp.minimum(b, nu[0] - 1), 0)
    grid_spec = pltpu.PrefetchScalarGridSpec(
        num_scalar_prefetch=4,
        grid=(n_blocks,),
        in_specs=[pl.BlockSpec((MOE_BLOCK * ROW_TILES, HEAD_DIM), rows),
                  pl.BlockSpec((1, d), lambda b, be, nu, nx, pa: (0, 0)),
                  pl.BlockSpec(memory_space=pl.ANY), pl.BlockSpec(memory_space=pl.ANY)],
        out_specs=pl.BlockSpec((MOE_BLOCK * ROW_TILES, HEAD_DIM), lambda b, be, nu, nx, pa: (b, 0)),
        scratch_shapes=[pltpu.VMEM((2, d, 2 * EXPERT_HIDDEN), F32), pltpu.VMEM((2, EXPERT_HIDDEN, d), F32),
                        pltpu.VMEM((d, 2 * EXPERT_HIDDEN), BF16), pltpu.VMEM((EXPERT_HIDDEN, d), BF16),
                        pltpu.SemaphoreType.DMA((2, 2))],
    )
    return pl.pallas_call(
        functools.partial(_expert_body, layer=layer),
        grid_spec=grid_spec,
        out_shape=jax.ShapeDtypeStruct(xs.shape, F32),
        compiler_params=_cparams(("arbitrary",)),
        name="experts",
    )(block_e, n_used, next_e, parity, xs, norm_g.reshape(1, d), w_up, w_down)


def dispatch_tables(eid, t):
    n_assign = t * TOP_K
    flat_e = eid.reshape(n_assign)
    onehot = (flat_e[:, None] == jnp.arange(N_EXPERTS, dtype=jnp.int32)[None, :]).astype(jnp.int32)
    csum = jnp.cumsum(onehot, axis=0)
    counts = csum[-1]
    rank = jnp.sum(csum * onehot, axis=1) - 1
    padded = (counts + MOE_BLOCK - 1) // MOE_BLOCK * MOE_BLOCK
    pad_end = jnp.cumsum(padded)
    pad_start = pad_end - padded
    dest = pad_start[flat_e] + rank
    n_blocks = -(-(n_assign + N_EXPERTS * (MOE_BLOCK - 1)) // MOE_BLOCK)
    n_slots = n_blocks * MOE_BLOCK
    zrow = jnp.where(padded > 0, (pad_end - MOE_BLOCK) * ROW_TILES, -1).astype(jnp.int32)
    n_used = pad_end[-1] // MOE_BLOCK
    starts = jnp.arange(n_blocks, dtype=jnp.int32) * MOE_BLOCK
    block_e = jnp.sum((pad_end[None, :] <= starts[:, None]).astype(jnp.int32), axis=1)
    block_e = jnp.minimum(block_e, N_EXPERTS - 1)
    last_e = block_e[jnp.maximum(n_used - 1, 0)]
    block_e = jnp.where(jnp.arange(n_blocks) < n_used, block_e, last_e)
    ids = jnp.arange(N_EXPERTS, dtype=jnp.int32)
    later = (ids[None, :] > ids[:, None]) & (counts[None, :] > 0)
    next_nonempty = jnp.min(jnp.where(later, ids[None, :], N_EXPERTS), axis=1)
    next_nonempty = jnp.where(next_nonempty == N_EXPERTS, -1, next_nonempty).astype(jnp.int32)
    rank_nonempty = jnp.cumsum((counts > 0).astype(jnp.int32)) - 1
    return (block_e, n_used.reshape(1).astype(jnp.int32), next_nonempty[block_e],
            (rank_nonempty[block_e] % 2).astype(jnp.int32), (dest * ROW_TILES).astype(jnp.int32), zrow, n_slots)


def _combine_rows(dest_ref, ys_hbm, ybuf, sem, tile, slot, tm, fn):
    def body(r, c):
        a0 = (tile * tm + r) * TOP_K
        for k in range(TOP_K):
            fn(_row_copy(ys_hbm, dest_ref[a0 + k], ybuf.at[slot, k], r * ROW_TILES, sem.at[slot]))
        return c

    lax.fori_loop(0, tm, body, 0, unroll=8)


def _combine_body(dest_ref, x_ref, w_ref, g_ref, ys_hbm, x2_ref, hn_ref, ybuf, sem, *, tm):
    i = pl.program_id(0)
    slot = i % 2

    @pl.when(i == 0)
    def _():
        _combine_rows(dest_ref, ys_hbm, ybuf, sem, i, slot, tm, lambda cp: cp.start())

    _combine_rows(dest_ref, ys_hbm, ybuf, sem, i, slot, tm, lambda cp: cp.wait())

    @pl.when(i + 1 < pl.num_programs(0))
    def _():
        _combine_rows(dest_ref, ys_hbm, ybuf, sem, i + 1, 1 - slot, tm, lambda cp: cp.start())

    w = w_ref[...]
    w0, w1 = w[:, 0:1], w[:, 1:2]
    ssq = None
    for a in range(ROW_TILES):
        rows = pl.ds(a, tm, stride=ROW_TILES)
        piece = x_ref[rows, :] + (w0 * ybuf[slot, 0, rows, :] + w1 * ybuf[slot, 1, rows, :])
        x2_ref[:, a * HEAD_DIM:(a + 1) * HEAD_DIM] = piece
        ssq = piece * piece if ssq is None else ssq + piece * piece
    inv = lax.rsqrt(jnp.sum(ssq, axis=-1, keepdims=True) * (1.0 / D_MODEL) + NORM_EPS)
    hn_ref[...] = (x2_ref[...] * inv * g_ref[...]).astype(hn_ref.dtype)


def combine(x1, ys, dest, wts, next_g, next_dtype, tm=256):
    t = x1.shape[0] // ROW_TILES
    d = D_MODEL
    row = pl.BlockSpec((tm, d), lambda i, dest: (i, 0))
    grid_spec = pltpu.PrefetchScalarGridSpec(
        num_scalar_prefetch=1,
        grid=(t // tm,),
        in_specs=[pl.BlockSpec((tm * ROW_TILES, HEAD_DIM), lambda i, dest: (i, 0)),
                  pl.BlockSpec((tm, HEAD_DIM), lambda i, dest: (i, 0)),
                  pl.BlockSpec((1, d), lambda i, dest: (0, 0)),
                  pl.BlockSpec(memory_space=pl.ANY)],
        out_specs=[row, row],
        scratch_shapes=[pltpu.VMEM((2, TOP_K, tm * ROW_TILES, HEAD_DIM), F32), pltpu.SemaphoreType.DMA((2,))],
    )
    return pl.pallas_call(
        functools.partial(_combine_body, tm=tm),
        grid_spec=grid_spec,
        out_shape=[jax.ShapeDtypeStruct((t, d), F32), jax.ShapeDtypeStruct((t, d), next_dtype)],
        compiler_params=_cparams(("arbitrary",)),
        name="moe_combine",
    )(dest, x1, wts, next_g.reshape(1, d), ys)


def _layer(x, h, cos_t, sin_t, layer, w_bd, w_all, conv_a_w, a_log, dt_bias, gdn_norm_g, conv_b_w, conv_b_b,
           ln_b_g, ln_b_b, pool_w, pool_scale, w_branch, w_out, norm2_g, wg, bg, we, be, w_up, w_down,
           next_g, next_dtype):
    t = x.shape[0]
    pool_bd = jnp.zeros((WIDTH, WIDTH), F32)
    for gi in range(len(POOL_WINDOWS)):
        pool_bd = lax.dynamic_update_slice(pool_bd, pool_w[gi], (gi * POOL_GROUP, gi * POOL_GROUP))
    w_router = jnp.pad(jnp.concatenate([wg, we], axis=1), ((0, 0), (0, HEAD_DIM - N_GROUPS - N_EXPERTS)))
    b_router = jnp.pad(jnp.concatenate([bg, be]), (0, HEAD_DIM - N_GROUPS - N_EXPERTS)).reshape(1, HEAD_DIM)

    u = project_mixers(h, w_all, layer, cos_t, sin_t)
    bd = project(h, w_bd, layer, HEAD_DIM)
    ya = gated_deltanet(u, bd, conv_a_w, a_log, dt_bias, gdn_norm_g)
    yb, yd = local_mixers(u, conv_b_w, conv_b_b, ln_b_g, ln_b_b, pool_bd.astype(BF16), pool_scale)
    yc = lse = None
    for gi, (_, dilation) in enumerate(ATT_PATTERNS):
        yc, lse = attention_group(u, gi, dilation, yc, lse)
    x1, rl = merge_project(h, ya, yb, yc, yd, w_all, w_branch, w_out, layer, x, norm2_g, w_router, b_router)
    eid, wts = route(rl)
    block_e, n_used, next_e, parity, dest, zrow, n_slots = dispatch_tables(eid[:, :TOP_K], t)
    xs = sort_rows(x1, dest, zrow, n_used, n_slots)
    ys = experts(xs, norm2_g, w_up, w_down, layer, block_e, n_used, next_e, parity)
    return combine(x1, ys, dest, wts, next_g, next_dtype)


def kernel(x, positions, norm1_g, w_in, conv_a_w, a_log, dt_bias, gdn_norm_g, conv_b_w, conv_b_b, ln_b_g, ln_b_b,
           pool_w, pool_scale, w_branch, w_out, norm2_g, router_group_w, router_group_b, router_expert_w,
           router_expert_b, w_up, w_down, final_norm_g):
    b_, s_, d = x.shape
    depth = w_in.shape[0]
    outs = []
    w_all = prepare_w_in(w_in)
    w_bd = beta_decay_weights(w_in)
    w_branch_b = w_branch.astype(BF16)
    w_out_b = w_out.astype(BF16)
    for bi in range(b_):
        xb = x[bi]
        cos_t, sin_t = rope_tables(positions[bi])
        h = rmsnorm(xb, norm1_g[0], BF16)
        for layer in range(depth):
            last = layer == depth - 1
            next_g = final_norm_g if last else norm1_g[layer + 1]
            xb, h = _layer(xb, h, cos_t, sin_t, layer, w_bd, w_all, conv_a_w[layer], a_log[layer], dt_bias[layer],
                           gdn_norm_g[layer], conv_b_w[layer], conv_b_b[layer], ln_b_g[layer], ln_b_b[layer],
                           pool_w[layer], pool_scale[layer], w_branch_b, w_out_b, norm2_g[layer],
                           router_group_w[layer], router_group_b[layer], router_expert_w[layer],
                           router_expert_b[layer], w_up, w_down, next_g, F32 if last else BF16)
        outs.append(h)
    return jnp.stack(outs, axis=0)
```

```python
import functools

import jax
import jax.numpy as jnp
from jax import lax
from jax.experimental import pallas as pl
from jax.experimental.pallas import tpu as pltpu

F32 = jnp.float32
BF16 = jnp.bfloat16

D_MODEL = 2048
N_HEADS = 6
HEAD_DIM = 128
WIDTH = N_HEADS * HEAD_DIM
GDN_CONV = 4
GDN_CHUNK = 64
GDN_SUPER = 256
CONF_CONV = 31
ATT_PATTERNS = ((128, 1), (512, 4), (2048, 16))
ATT_BLOCK = 128
ATT_UNITS_IN_FLIGHT = 6
ATT_HEADS_PER_STEP = {1: 6, 4: 1, 16: 1}
ROPE_THETA = 10000.0
ROPE_ROWS = 256
POOL_WINDOWS = (2, 4, 8, 16)
POOL_GROUP = 192
N_BRANCHES = 4
N_GROUPS = 8
PER_GROUP = 8
N_EXPERTS = 64
TOP_K = 2
EXPERT_HIDDEN = 512
MOE_BLOCK = 128
EXPERT_WEIGHT_SLOTS = 3
MOE_SPARE = 2 * MOE_BLOCK
ROW_TILES = D_MODEL // HEAD_DIM
LOCAL_ROWS = 32
NORM_EPS = 1e-6
LN_EPS = 1e-5

U_QKV_A = 0
U_Z_A = 2304
U_B = 3072
U_C = 4608
U_D = 11520
U_WIDTH = 12288
W_BETA = 3072
W_UB = 3084
W_GATE = 12300

VMEM_LIMIT = 56 * 1024 * 1024


def _cparams(semantics, vmem=VMEM_LIMIT):
    return pltpu.CompilerParams(dimension_semantics=semantics, vmem_limit_bytes=vmem)


def _sigmoid(x):
    return 1.0 / (1.0 + jnp.exp(-x))


def _silu(x):
    return x * _sigmoid(x)


def _dot(a, b):
    return jnp.dot(a, b, preferred_element_type=F32)


def _dot_nt(a, b):
    return lax.dot_general(a, b, (((1,), (1,)), ((), ())), preferred_element_type=F32)


def _dot_tn(a, b):
    return lax.dot_general(a, b, (((0,), (0,)), ((), ())), preferred_element_type=F32)


def _rmsnorm_body(x_ref, g_ref, o_ref):
    x = x_ref[...]
    y = x * lax.rsqrt(jnp.mean(x * x, axis=-1, keepdims=True) + NORM_EPS) * g_ref[...]
    o_ref[...] = y.astype(o_ref.dtype)


def rmsnorm(x, g, out_dtype, tm=512):
    t, d = x.shape
    return pl.pallas_call(
        _rmsnorm_body,
        grid=(t // tm,),
        in_specs=[pl.BlockSpec((tm, d), lambda i: (i, 0)), pl.BlockSpec((1, d), lambda i: (0, 0))],
        out_specs=pl.BlockSpec((tm, d), lambda i: (i, 0)),
        out_shape=jax.ShapeDtypeStruct((t, d), out_dtype),
        compiler_params=_cparams(("parallel",)),
        name="rmsnorm",
    )(x, g.reshape(1, d))


def _wprep_copy(wt_hbm, buf, sem, j, slot, *, tn, n_plain, shift):
    row0 = j * tn + jnp.where(j >= n_plain, shift, 0)
    return pltpu.make_async_copy(wt_hbm.at[pl.ds(row0, tn)], buf.at[slot], sem.at[slot])


def _wprep_body(wt_hbm, o_ref, buf, sem, *, tn, n_plain, shift):
    j = pl.program_id(0)
    slot = j % 2
    cp = functools.partial(_wprep_copy, wt_hbm, buf, sem, tn=tn, n_plain=n_plain, shift=shift)

    @pl.when(j == 0)
    def _():
        cp(j, slot).start()

    cp(j, slot).wait()

    @pl.when(j + 1 < pl.num_programs(0))
    def _():
        cp(j + 1, 1 - slot).start()

    for l in range(o_ref.shape[0]):
        o_ref[l] = buf[slot, :, l, :].T.astype(BF16)


def prepare_w_in(w_in, tn=256):
    depth, k, n = w_in.shape
    w_t = jnp.transpose(w_in, (2, 0, 1))
    n_out = W_BETA + (n - W_UB)
    return pl.pallas_call(
        functools.partial(_wprep_body, tn=tn, n_plain=W_BETA // tn, shift=W_UB - W_BETA),
        grid=(n_out // tn,),
        in_specs=[pl.BlockSpec(memory_space=pl.ANY)],
        out_specs=pl.BlockSpec((depth, k, tn), lambda j: (0, 0, j)),
        out_shape=jax.ShapeDtypeStruct((depth, k, n_out), BF16),
        scratch_shapes=[pltpu.VMEM((2, tn, depth, k), F32), pltpu.SemaphoreType.DMA((2,))],
        compiler_params=_cparams(("arbitrary",)),
        name="prepare_w_in",
    )(w_t)


def _bd_weight_body(w_ref, o_ref):
    for l in range(o_ref.shape[0]):
        o_ref[l] = w_ref[:, l, :].T.astype(BF16)


def beta_decay_weights(w_in):
    depth, k, _ = w_in.shape
    w_t = jnp.transpose(w_in, (2, 0, 1))
    return pl.pallas_call(
        _bd_weight_body,
        grid=(1,),
        in_specs=[pl.BlockSpec((HEAD_DIM, depth, k), lambda i: (W_BETA // HEAD_DIM, 0, 0))],
        out_specs=pl.BlockSpec((depth, k, HEAD_DIM), lambda i: (0, 0, 0)),
        out_shape=jax.ShapeDtypeStruct((depth, k, HEAD_DIM), BF16),
        compiler_params=_cparams(("arbitrary",)),
        name="beta_decay_weights",
    )(w_t)


def _matmul_body(h_ref, w_ref, o_ref):
    o_ref[...] = _dot(h_ref[...], w_ref[...].astype(BF16)).astype(o_ref.dtype)


def project(h, w, layer, n, col0=0, tm=1024, tn=768):
    t, k = h.shape
    tn = min(tn, n)
    return pl.pallas_call(
        _matmul_body,
        grid=(n // tn, t // tm),
        in_specs=[pl.BlockSpec((tm, k), lambda j, i: (i, 0)),
                  pl.BlockSpec((None, k, tn), lambda j, i: (layer, 0, col0 // tn + j))],
        out_specs=pl.BlockSpec((tm, tn), lambda j, i: (i, j)),
        out_shape=jax.ShapeDtypeStruct((t, n), F32),
        compiler_params=_cparams(("parallel", "parallel")),
        name="in_proj",
    )(h, w)


def _rope(x, cos, sin):
    return x * cos + pltpu.roll(x, HEAD_DIM // 2, axis=1) * sin


def _matmul_rope_body(h_ref, w_ref, cos_ref, sin_ref, o_ref, *, q_tiles, k_tiles):
    j = pl.program_id(0)
    is_q = (j >= q_tiles[0]) & (j < q_tiles[1])
    is_k = (j >= k_tiles[0]) & (j < k_tiles[1])

    @pl.when(jnp.logical_not(is_q | is_k))
    def _():
        o_ref[...] = _dot(h_ref[...], w_ref[...])

    def rotated(scale):
        for r0 in range(0, o_ref.shape[0], ROPE_ROWS):
            rs = slice(r0, r0 + ROPE_ROWS)
            acc = _dot(h_ref[rs, :], w_ref[...])
            cos, sin = cos_ref[rs, :], sin_ref[rs, :]
            for c in range(o_ref.shape[1] // HEAD_DIM):
                sl = slice(c * HEAD_DIM, (c + 1) * HEAD_DIM)
                o_ref[rs, sl] = _rope(acc[:, sl], cos, sin) * scale

    @pl.when(is_q)
    def _():
        rotated(HEAD_DIM ** -0.5)

    @pl.when(is_k)
    def _():
        rotated(1.0)


def project_mixers(h, w, layer, cos_t, sin_t, tm=1024, tn=WIDTH):
    t, k = h.shape
    q0 = U_C // tn
    nq = 3 * WIDTH // tn
    tab = pl.BlockSpec((tm, HEAD_DIM), lambda j, i: (i, 0))
    return pl.pallas_call(
        functools.partial(_matmul_rope_body, q_tiles=(q0, q0 + nq), k_tiles=(q0 + nq, q0 + 2 * nq)),
        grid=(U_WIDTH // tn, t // tm),
        in_specs=[pl.BlockSpec((tm, k), lambda j, i: (i, 0)),
                  pl.BlockSpec((None, k, tn), lambda j, i: (layer, 0, j)), tab, tab],
        out_specs=pl.BlockSpec((tm, tn), lambda j, i: (i, j)),
        out_shape=jax.ShapeDtypeStruct((t, U_WIDTH), F32),
        compiler_params=_cparams(("parallel", "parallel")),
        name="in_proj",
    )(h, w, cos_t, sin_t)


def _rope_body(pos_ref, inv_ref, cos_ref, sin_ref):
    ang = pos_ref[...].astype(F32) * inv_ref[...]
    lane = lax.broadcasted_iota(jnp.int32, ang.shape, 1)
    cos_ref[...] = jnp.cos(ang)
    s = jnp.sin(ang)
    sin_ref[...] = jnp.where(lane < HEAD_DIM // 2, -s, s)


def rope_tables(pos, tm=1024):
    t = pos.shape[0]
    half = HEAD_DIM // 2
    inv = ROPE_THETA ** (-jnp.arange(half, dtype=F32) * 2.0 / HEAD_DIM)
    inv = jnp.concatenate([inv, inv]).reshape(1, HEAD_DIM)
    spec = pl.BlockSpec((tm, HEAD_DIM), lambda i: (i, 0))
    return pl.pallas_call(
        _rope_body,
        grid=(t // tm,),
        in_specs=[pl.BlockSpec((tm, 1), lambda i: (i, 0)), pl.BlockSpec((1, HEAD_DIM), lambda i: (0, 0))],
        out_specs=[spec, spec],
        out_shape=[jax.ShapeDtypeStruct((t, HEAD_DIM), F32)] * 2,
        compiler_params=_cparams(("parallel",)),
        name="rope_tables",
    )(pos.reshape(t, 1), inv)


def _attn_body(*refs, first, d, hw):
    if first:
        q_ref, kc_ref, kp_ref, vc_ref, vp_ref, o_out, lse_out = refs
        o_in = lse_in = None
    else:
        q_ref, kc_ref, kp_ref, vc_ref, vp_ref, o_in, lse_in, o_out, lse_out = refs
    n = pl.program_id(0)
    blk = ATT_BLOCK
    row = lax.broadcasted_iota(jnp.int32, (blk, blk), 0)
    col = lax.broadcasted_iota(jnp.int32, (blk, blk), 1)
    in_cur = col <= row
    has_prev = n > 0
    neg = -jnp.inf
    units = [(pl.ds(r, blk, stride=d) if d > 1 else pl.ds(0, blk), slice(h * HEAD_DIM, (h + 1) * HEAD_DIM))
             for r in range(d) for h in range(hw)]
    for g0 in range(0, len(units), ATT_UNITS_IN_FLIGHT):
        group = units[g0:g0 + ATT_UNITS_IN_FLIGHT]
        scores = []
        for rows, sl in group:
            q = q_ref[rows, sl]
            kp = kp_ref[rows, sl]
            qb = q.astype(BF16)
            s_c = _dot_nt(qb, kc_ref[rows, sl].astype(BF16))
            s_p = _dot_nt(qb, kp.astype(BF16))
            s_d = jnp.where(has_prev, jnp.sum(q * kp, axis=-1, keepdims=True), neg)
            scores.append((jnp.where(in_cur, s_c, jnp.where(has_prev, s_p, neg)), s_d))
        probs = []
        for (rows, sl), (s_w, s_d) in zip(group, scores):
            m = jnp.maximum(jnp.max(s_w, axis=-1, keepdims=True), s_d)
            m_old = None
            if not first:
                m_old = lse_in[rows, sl][:, 0:1]
                m = jnp.maximum(m, m_old)
            p_w = jnp.exp(s_w - m)
            p_d = jnp.exp(s_d - m)
            probs.append((m, m_old, p_w, p_d, jnp.sum(p_w, axis=-1, keepdims=True) + p_d))
        for (rows, sl), (m, m_old, p_w, p_d, l) in zip(group, probs):
            vp = vp_ref[rows, sl]
            acc = (_dot(jnp.where(in_cur, p_w, 0.0).astype(BF16), vc_ref[rows, sl].astype(BF16))
                   + _dot(jnp.where(in_cur, 0.0, p_w).astype(BF16), vp.astype(BF16)) + p_d * vp)
            if not first:
                w_old = jnp.exp(m_old - m)
                l = l + w_old
                acc = acc + w_old * o_in[rows, sl]
            o_out[rows, sl] = acc / l
            lse_out[rows, sl] = jnp.broadcast_to(m + jnp.log(l), (blk, HEAD_DIM))


def attention_group(u, group, dilation, o_prev, lse_prev):
    t = u.shape[0]
    d = dilation
    span = ATT_BLOCK * d
    hw = ATT_HEADS_PER_STEP[d]
    cw = hw * HEAD_DIM
    first = o_prev is None
    qoff = (U_C + group * WIDTH) // cw
    koff = (U_C + 3 * WIDTH + group * WIDTH) // cw
    voff = (U_C + 6 * WIDTH + group * WIDTH) // cw

    def cur(off):
        return pl.BlockSpec((span, cw), lambda n, g: (n, off + g))

    def prev(off):
        return pl.BlockSpec((span, cw), lambda n, g: (jnp.maximum(n - 1, 0), off + g))

    o_spec = cur(0)
    in_specs = [cur(qoff), cur(koff), prev(koff), cur(voff), prev(voff)]
    args = [u, u, u, u, u]
    if not first:
        in_specs += [o_spec, o_spec]
        args += [o_prev, lse_prev]
    return pl.pallas_call(
        functools.partial(_attn_body, first=first, d=d, hw=hw),
        grid=(t // span, N_HEADS // hw),
        in_specs=in_specs,
        out_specs=[o_spec, o_spec],
        out_shape=[jax.ShapeDtypeStruct((t, WIDTH), F32), jax.ShapeDtypeStruct((t, WIDTH), F32)],
        compiler_params=_cparams(("parallel", "parallel")),
        name=f"attn_d{d}",
    )(*args)


def _local_body(ub_ref, ud_ref, cw_ref, cb_ref, lg_ref, lb_ref, pw_ref, ps_ref, yb_ref, yd_ref, extb, extd, pool_buf, *, tm):
    i = pl.program_id(0)
    hb, hd = 32, 16

    @pl.when(i == 0)
    def _():
        extb[0:hb, :] = jnp.zeros((hb, WIDTH), F32)
        extd[0:hd, :] = jnp.zeros((hd, WIDTH), F32)

    @pl.when(i > 0)
    def _():
        extb[0:hb, :] = extb[tm:tm + hb, :]
        extd[0:hd, :] = extd[tm:tm + hd, :]

    ub = ub_ref[...]
    extb[hb:hb + tm, :] = ub[:, :WIDTH] * _sigmoid(ub[:, WIDTH:])
    extd[hd:hd + tm, :] = ud_ref[...]

    rc = LOCAL_ROWS
    ch = lax.broadcasted_iota(jnp.int32, (rc, WIDTH), 1)
    rows = lax.broadcasted_iota(jnp.int32, (rc, WIDTH), 0)
    for c0 in range(0, tm, rc):
        acc = None
        for s in range(8):
            win = rc + (8 if s else 0)
            part = None
            for j in range(CONF_CONV):
                off = hb - (CONF_CONV - 1) + j
                if off % 8 != s:
                    continue
                base = c0 + off - s
                term = cw_ref[j:j + 1, :] * extb[base:base + win, :]
                part = term if part is None else part + term
            if part is None:
                continue
            part = part[s:s + rc, :]
            acc = part if acc is None else acc + part
        cb = acc + cb_ref[...]
        mu = jnp.mean(cb, axis=-1, keepdims=True)
        cen = cb - mu
        var = jnp.mean(cen * cen, axis=-1, keepdims=True)
        yb = _silu(cen * lax.rsqrt(var + LN_EPS) * lg_ref[...] + lb_ref[...])
        yb_ref[c0:c0 + rc, :] = yb.astype(yb_ref.dtype)

        xd = extd[c0 + hd:c0 + hd + rc, :]
        tpos = i * tm + c0 + rows + 1
        run = xd
        pooled = None
        shift = 1
        for gi, w in enumerate(POOL_WINDOWS):
            while shift < w:
                run = run + extd[c0 + hd - shift:c0 + hd - shift + rc, :]
                shift += 1
            val = run / jnp.minimum(tpos, w).astype(F32) - xd
            pooled = val if pooled is None else jnp.where(ch >= gi * POOL_GROUP, val, pooled)
        pool_buf[c0:c0 + rc, :] = pooled.astype(pool_buf.dtype)
    yd = _dot(pool_buf[...], pw_ref[...]) * ps_ref[...]
    yd_ref[...] = yd.astype(yd_ref.dtype)


def local_mixers(u, conv_w, conv_b, ln_g, ln_b, pool_w_bd, pool_scale, tm=256):
    t = u.shape[0]
    row = lambda a: a.reshape(1, WIDTH)
    vec = pl.BlockSpec((1, WIDTH), lambda i: (0, 0))
    out = pl.BlockSpec((tm, WIDTH), lambda i: (i, 0))
    return pl.pallas_call(
        functools.partial(_local_body, tm=tm),
        grid=(t // tm,),
        in_specs=[pl.BlockSpec((tm, 2 * WIDTH), lambda i: (i, U_B // (2 * WIDTH))),
                  pl.BlockSpec((tm, WIDTH), lambda i: (i, U_D // WIDTH)),
                  pl.BlockSpec((CONF_CONV, WIDTH), lambda i: (0, 0)), vec, vec, vec,
                  pl.BlockSpec((WIDTH, WIDTH), lambda i: (0, 0)), vec],
        out_specs=[out, out],
        out_shape=[jax.ShapeDtypeStruct((t, WIDTH), BF16)] * 2,
        scratch_shapes=[pltpu.VMEM((tm + 32, WIDTH), F32), pltpu.VMEM((tm + 16, WIDTH), F32),
                        pltpu.VMEM((tm, WIDTH), BF16)],
        compiler_params=_cparams(("arbitrary",)),
        name="conv_pool",
    )(u, u, conv_w, row(conv_b), row(ln_g), row(ln_b), pool_w_bd, row(pool_scale))


def _gdn_body(qkv_ref, z_ref, bd_ref, cw_ref, alog_ref, dtb_ref, gn_ref, ya_ref, ext, state):
    s = pl.program_id(0)
    sup, c = GDN_SUPER, GDN_CHUNK
    halo = 8

    @pl.when(s == 0)
    def _():
        ext[0:halo, :] = jnp.zeros((halo, 3 * WIDTH), F32)
        state[...] = jnp.zeros(state.shape, F32)

    @pl.when(s > 0)
    def _():
        ext[0:halo, :] = ext[sup:sup + halo, :]

    ext[halo:halo + sup, :] = qkv_ref[...]
    acc = jnp.zeros((sup, 3 * WIDTH), F32)
    for j in range(GDN_CONV):
        off = halo - (GDN_CONV - 1) + j
        acc = acc + cw_ref[j:j + 1, :] * ext[off:off + sup, :]
    qkv = _silu(acc)

    bd = bd_ref[...]
    beta_all = _sigmoid(bd)
    xs = bd + dtb_ref[...]
    softplus = jnp.maximum(xs, 0.0) + jnp.log(1.0 + jnp.exp(-jnp.abs(xs)))
    g_all = -jnp.exp(alog_ref[...]) * softplus
    rowi = lax.broadcasted_iota(jnp.int32, (sup, HEAD_DIM), 0)
    in_chunk = rowi & (c - 1)
    gc = g_all
    sh = 1
    while sh < c:
        gc = gc + jnp.where(in_chunk >= sh, pltpu.roll(gc, sh, axis=0), 0.0)
        sh *= 2
    gct = gc.T

    ri = lax.broadcasted_iota(jnp.int32, (sup, sup), 0)
    ci = lax.broadcasted_iota(jnp.int32, (sup, sup), 1)
    same = (ri & -c) == (ci & -c)
    incl = same & (ri >= ci)
    strict = same & (ri > ci)
    eye = (ri == ci).astype(F32)

    heads = range(N_HEADS)
    hs = [slice(h * HEAD_DIM, (h + 1) * HEAD_DIM) for h in heads]
    prep = []
    for h in heads:
        q = qkv[:, hs[h]]
        k = qkv[:, WIDTH + h * HEAD_DIM:WIDTH + (h + 1) * HEAD_DIM]
        v = qkv[:, 2 * WIDTH + h * HEAD_DIM:2 * WIDTH + (h + 1) * HEAD_DIM]
        q = q * lax.rsqrt(jnp.sum(q * q, axis=-1, keepdims=True) + 1e-6) * (HEAD_DIM ** -0.5)
        k = k * lax.rsqrt(jnp.sum(k * k, axis=-1, keepdims=True) + 1e-6)
        b_col = beta_all[:, h:h + 1]
        gc_col = gc[:, N_HEADS + h:N_HEADS + h + 1]
        gc_row = gct[N_HEADS + h:N_HEADS + h + 1, :]
        dm = jnp.exp(jnp.where(incl, gc_col - gc_row, -jnp.inf))
        kb = k.astype(BF16)
        nmat = jnp.where(strict, -(b_col * _dot_nt(kb, kb) * dm), 0.0)
        prep.append((q, k, v, b_col, gc_col, dm, kb, nmat))

    pmats = [eye + p[7] for p in prep]
    mbs = [p[7].astype(BF16) for p in prep]
    step = 2
    while step < c:
        m2s = [_dot(mb, mb).astype(BF16) for mb in mbs]
        pmats = [pm + _dot(pm.astype(BF16), m2) for pm, m2 in zip(pmats, m2s)]
        mbs = m2s
        step *= 2

    chunked = []
    for h in heads:
        q, k, v, b_col, gc_col, dm, kb, _ = prep[h]
        eg = jnp.exp(gc_col)
        rhs = jnp.concatenate([v * b_col, k * (b_col * eg)], axis=1).astype(BF16)
        sol = _dot(pmats[h].astype(BF16), rhs)
        qk_b = (_dot_nt(q.astype(BF16), kb) * dm).astype(BF16)
        q_dec = (q * eg).astype(BF16)
        gl_col = jnp.concatenate(
            [jnp.broadcast_to(gc_col[(n + 1) * c - 1:(n + 1) * c, :], (c, 1)) for n in range(sup // c)], axis=0)
        k_dec = k * jnp.exp(gl_col - gc_col)
        chunked.append((sol[:, :HEAD_DIM], sol[:, HEAD_DIM:].astype(BF16), qk_b, q_dec, k_dec, gc_col))

    states = [state[h] for h in heads]
    outs = [[] for _ in heads]
    for n in range(sup // c):
        rs = slice(n * c, (n + 1) * c)
        for h in heads:
            u_c, w_b, qk_b, q_dec, k_dec, gc_col = chunked[h]
            sb = states[h].astype(BF16)
            v_new = u_c[rs] - _dot(w_b[rs], sb)
            vb = v_new.astype(BF16)
            outs[h].append(_dot(q_dec[rs], sb) + _dot(qk_b[rs, rs], vb))
            g_last = jnp.exp(gc_col[(n + 1) * c - 1:(n + 1) * c, :])
            states[h] = states[h] * g_last + _dot(k_dec[rs].T.astype(BF16), vb)

    for h in heads:
        state[h] = states[h]
        o = jnp.concatenate(outs[h], axis=0)
        o = o * lax.rsqrt(jnp.mean(o * o, axis=-1, keepdims=True) + NORM_EPS) * gn_ref[...]
        ya_ref[:, hs[h]] = (o * _silu(z_ref[:, hs[h]])).astype(ya_ref.dtype)


def gated_deltanet(u, bd, conv_w, a_log, dt_bias, norm_g):
    t = u.shape[0]
    sup = GDN_SUPER
    pad = jnp.zeros((N_HEADS,), F32)
    lane_row = lambda a: jnp.concatenate([pad, a, jnp.zeros((HEAD_DIM - 2 * N_HEADS,), F32)]).reshape(1, HEAD_DIM)
    vec = pl.BlockSpec((1, HEAD_DIM), lambda s: (0, 0))
    return pl.pallas_call(
        _gdn_body,
        grid=(t // sup,),
        in_specs=[pl.BlockSpec((sup, 3 * WIDTH), lambda s: (s, U_QKV_A // (3 * WIDTH))),
                  pl.BlockSpec((sup, WIDTH), lambda s: (s, U_Z_A // WIDTH)),
                  pl.BlockSpec((sup, HEAD_DIM), lambda s: (s, 0)),
                  pl.BlockSpec((GDN_CONV, 3 * WIDTH), lambda s: (0, 0)), vec, vec, vec],
        out_specs=pl.BlockSpec((sup, WIDTH), lambda s: (s, 0)),
        out_shape=jax.ShapeDtypeStruct((t, WIDTH), BF16),
        scratch_shapes=[pltpu.VMEM((sup + 8, 3 * WIDTH), F32), pltpu.VMEM((N_HEADS, HEAD_DIM, HEAD_DIM), F32)],
        compiler_params=_cparams(("arbitrary",)),
        name="gated_deltanet",
    )(u, u, bd, conv_w, lane_row(a_log), lane_row(dt_bias), norm_g.reshape(1, HEAD_DIM))


def _merge_body(h_ref, ya_ref, yb_ref, yc_ref, yd_ref, g0_ref, g1_ref, g2_ref, g3_ref,
                b0_ref, b1_ref, b2_ref, b3_ref, wo_ref, x_ref, n2_ref, wr_ref, br_ref,
                x1_ref, rl_ref, acc):
    j = pl.program_id(1)

    @pl.when(j == 0)
    def _():
        acc[...] = jnp.zeros(acc.shape, F32)

    hb = h_ref[...]
    merged = None
    for y_ref, g_ref, b_ref in ((ya_ref, g0_ref, b0_ref), (yb_ref, g1_ref, b1_ref),
                                (yc_ref, g2_ref, b2_ref), (yd_ref, g3_ref, b3_ref)):
        gate = _sigmoid(_dot(hb, g_ref[...]))
        term = gate * _dot(y_ref[...].astype(BF16), b_ref[...])
        merged = term if merged is None else merged + term
    acc[...] += _dot(merged.astype(BF16), wo_ref[...])

    @pl.when(j == pl.num_programs(1) - 1)
    def _():
        x1 = x_ref[...] + acc[...]
        tm = x1.shape[0]
        for a in range(ROW_TILES):
            x1_ref[pl.ds(a, tm, stride=ROW_TILES), :] = x1[:, a * HEAD_DIM:(a + 1) * HEAD_DIM]
        h2 =x1 * lax.rsqrt(jnp.mean(x1 * x1, axis=-1, keepdims=True) + NORM_EPS) * n2_ref[...]
        rl_ref[...] = jnp.dot(h2, wr_ref[...], preferred_element_type=F32,
                              precision=lax.Precision.HIGHEST) + br_ref[...]


def merge_project(h, ya, yb, yc, yd, w_all, w_branch, w_out, layer, x, norm2_g, w_router, b_router, tm=512, tn=256):
    t = h.shape[0]
    d = D_MODEL
    nj = d // tn
    g0 = U_WIDTH // tn
    row = pl.BlockSpec((tm, d), lambda i, j: (i, 0))
    ysp = pl.BlockSpec((tm, WIDTH), lambda i, j: (i, 0))
    gate_spec = lambda k: pl.BlockSpec((None, d, tn), lambda i, j: (layer, 0, g0 + k * nj + j))
    br_spec = lambda k: pl.BlockSpec((None, None, WIDTH, tn), lambda i, j: (layer, k, 0, j))
    lanes = pl.BlockSpec((tm, HEAD_DIM), lambda i, j: (i, 0))
    w_gate = w_all
    return pl.pallas_call(
        _merge_body,
        grid=(t // tm, nj),
        in_specs=[row, ysp, ysp, ysp, ysp] + [gate_spec(k) for k in range(4)] + [br_spec(k) for k in range(4)]
        + [pl.BlockSpec((None, tn, d), lambda i, j: (layer, j, 0)), row, pl.BlockSpec((1, d), lambda i, j: (0, 0)),
           pl.BlockSpec((d, HEAD_DIM), lambda i, j: (0, 0)), pl.BlockSpec((1, HEAD_DIM), lambda i, j: (0, 0))],
        out_specs=[pl.BlockSpec((tm * ROW_TILES, HEAD_DIM), lambda i, j: (i, 0)), lanes],
        out_shape=[jax.ShapeDtypeStruct((t * ROW_TILES, HEAD_DIM), F32), jax.ShapeDtypeStruct((t, HEAD_DIM), F32)],
        scratch_shapes=[pltpu.VMEM((tm, d), F32)],
        compiler_params=_cparams(("parallel", "arbitrary")),
        name="merge_project",
    )(h, ya, yb, yc, yd, w_gate, w_gate, w_gate, w_gate, w_branch, w_branch, w_branch, w_branch,
      w_out, x, norm2_g.reshape(1, d), w_router, b_router)


def _router_body(rl_ref, eid_ref, wts_ref):
    rl = rl_ref[...]
    lane_i = lax.broadcasted_iota(jnp.int32, rl.shape, 1)
    lane = lane_i.astype(F32)
    neg = -jnp.inf
    big = 1e9
    gl = jnp.where(lane < N_GROUPS, rl, neg)
    gmax = jnp.max(gl, axis=-1, keepdims=True)
    gi = jnp.min(jnp.where(gl == gmax, lane, big), axis=-1, keepdims=True)
    gp = 1.0 / jnp.sum(jnp.exp(gl - gmax), axis=-1, keepdims=True)
    lo = N_GROUPS + gi * PER_GROUP
    el = jnp.where((lane >= lo) & (lane < lo + PER_GROUP), rl, neg)
    e1 = jnp.max(el, axis=-1, keepdims=True)
    i1 = jnp.min(jnp.where(el == e1, lane, big), axis=-1, keepdims=True)
    el2 = jnp.where(lane == i1, neg, el)
    e2 = jnp.max(el2, axis=-1, keepdims=True)
    i2 = jnp.min(jnp.where(el2 == e2, lane, big), axis=-1, keepdims=True)
    esum = jnp.sum(jnp.exp(el - e1), axis=-1, keepdims=True)
    p1 = 1.0 / esum
    p2 = jnp.exp(e2 - e1) / esum
    w1 = gp * p1 / (p1 + p2)
    w2 = gp * p2 / (p1 + p2)
    eid = jnp.where(lane_i == 0, i1 - N_GROUPS, jnp.where(lane_i == 1, i2 - N_GROUPS, 0.0))
    eid_ref[...] = eid.astype(jnp.int32)
    wts_ref[...] = jnp.where(lane_i == 0, w1, jnp.where(lane_i == 1, w2, 0.0))


def route(rl, tm=512):
    t = rl.shape[0]
    spec = pl.BlockSpec((tm, HEAD_DIM), lambda i: (i, 0))
    return pl.pallas_call(
        _router_body,
        grid=(t // tm,),
        in_specs=[spec],
        out_specs=[spec, spec],
        out_shape=[jax.ShapeDtypeStruct((t, HEAD_DIM), jnp.int32), jax.ShapeDtypeStruct((t, HEAD_DIM), F32)],
        compiler_params=_cparams(("parallel",)),
        name="router",
    )(rl)


def _row_copy(src, src_row, dst, dst_row, sem):
    return pltpu.make_async_copy(src.at[pl.ds(pl.multiple_of(src_row, ROW_TILES), ROW_TILES), :],
                                 dst.at[pl.ds(pl.multiple_of(dst_row, ROW_TILES), ROW_TILES), :], sem)


def _zero_copy(zbuf, zrow_ref, xs_hbm, zsem, e):
    start = pl.multiple_of(jnp.maximum(zrow_ref[e], 0), MOE_BLOCK * ROW_TILES)
    return pltpu.make_async_copy(zbuf, xs_hbm.at[pl.ds(start, MOE_BLOCK * ROW_TILES), :], zsem)


def _zero_block(zbuf, xs_hbm, zsem, blk):
    start = pl.multiple_of(blk * (MOE_BLOCK * ROW_TILES), MOE_BLOCK * ROW_TILES)
    return pltpu.make_async_copy(zbuf, xs_hbm.at[pl.ds(start, MOE_BLOCK * ROW_TILES), :], zsem)


def _sort_rows_body(dest_ref, zrow_ref, nused_ref, x_ref, xs_hbm, zbuf, sem, zsem, *, tm):
    i = pl.program_id(0)
    sem, zsem = sem.at[0], zsem.at[0]

    @pl.when(i == 0)
    def _():
        zbuf[...] = jnp.zeros(zbuf.shape, F32)
        n_blocks = xs_hbm.shape[0] // (MOE_BLOCK * ROW_TILES)
        for e in range(N_EXPERTS):
            @pl.when(zrow_ref[e] >= 0)
            def _():
                _zero_copy(zbuf, zrow_ref, xs_hbm, zsem, e).start()

        def z_start(blk, c):
            _zero_block(zbuf, xs_hbm, zsem, blk).start()
            return c

        def z_wait(blk, c):
            _zero_block(zbuf, xs_hbm, zsem, blk).wait()
            return c

        lax.fori_loop(nused_ref[0], n_blocks, z_start, 0)
        for e in range(N_EXPERTS):
            @pl.when(zrow_ref[e] >= 0)
            def _():
                _zero_copy(zbuf, zrow_ref, xs_hbm, zsem, e).wait()
        lax.fori_loop(nused_ref[0], n_blocks, z_wait, 0)

    def copies(r):
        a0 = (i * tm + r) * TOP_K
        return [_row_copy(x_ref, r * ROW_TILES, xs_hbm, dest_ref[a0 + k], sem) for k in range(TOP_K)]

    def start(r, c):
        for cp in copies(r):
            cp.start()
        return c

    def wait(r, c):
        for _ in range(TOP_K):
            _row_copy(x_ref, 0, xs_hbm, 0, sem).wait()
        return c

    lax.fori_loop(0, tm, start, 0, unroll=8)
    lax.fori_loop(0, tm, wait, 0, unroll=8)


def sort_rows(x1, dest, zrow, n_used, n_slots, tm=512):
    t = x1.shape[0] // ROW_TILES
    grid_spec = pltpu.PrefetchScalarGridSpec(
        num_scalar_prefetch=3,
        grid=(t // tm,),
        in_specs=[pl.BlockSpec((tm * ROW_TILES, HEAD_DIM), lambda i, dest, zrow, nu: (i, 0))],
        out_specs=pl.BlockSpec(memory_space=pl.ANY),
        scratch_shapes=[pltpu.VMEM((MOE_BLOCK * ROW_TILES, HEAD_DIM), F32),
                        pltpu.SemaphoreType.DMA((1,)), pltpu.SemaphoreType.DMA((1,))],
    )
    return pl.pallas_call(
        functools.partial(_sort_rows_body, tm=tm),
        grid_spec=grid_spec,
        out_shape=jax.ShapeDtypeStruct((n_slots * ROW_TILES, HEAD_DIM), F32),
        compiler_params=_cparams(("arbitrary",)),
        name="moe_sort_rows",
    )(dest, zrow, n_used, x1)


def _weight_copies(wup_hbm, wdn_hbm, wup_f, wdn_f, wsem, layer, e, slot):
    return (pltpu.make_async_copy(wup_hbm.at[layer, e], wup_f.at[slot], wsem.at[0, slot]),
            pltpu.make_async_copy(wdn_hbm.at[layer, e], wdn_f.at[slot], wsem.at[1, slot]))


def _expert_body(be_ref, nused_ref, nxt1_ref, nxt2_ref, par_ref, xs_ref, g_ref, wup_hbm, wdn_hbm, ys_ref,
                 wup_f, wdn_f, wup_b, wdn_b, wsem, *, layer):
    b = pl.program_id(0)
    copies = functools.partial(_weight_copies, wup_hbm, wdn_hbm, wup_f, wdn_f, wsem, layer)

    def start_if_any(e, slot):
        @pl.when(e >= 0)
        def _():
            for cp in copies(e, slot):
                cp.start()

    @pl.when(b == 0)
    def _():
        start_if_any(be_ref[0], par_ref[0])
        start_if_any(nxt1_ref[0], (par_ref[0] + 1) % EXPERT_WEIGHT_SLOTS)

    @pl.when(b >= nused_ref[0])
    def _():
        ys_ref[...] = jnp.zeros(ys_ref.shape, F32)

    @pl.when(b < nused_ref[0])
    def _():
        changed = jnp.logical_or(b == 0, be_ref[b] != be_ref[jnp.maximum(b - 1, 0)])

        @pl.when(changed)
        def _():
            slot = par_ref[b]
            for cp in copies(be_ref[b], slot):
                cp.wait()
            start_if_any(nxt2_ref[b], (slot + 2) % EXPERT_WEIGHT_SLOTS)

            wup_b[...] = wup_f[slot].astype(BF16)
            wdn_b[...] = wdn_f[slot].astype(BF16)

        pieces = [xs_ref[pl.ds(a, MOE_BLOCK, stride=ROW_TILES), :] for a in range(ROW_TILES)]
        ssq = pieces[0] * pieces[0]
        for p in pieces[1:]:
            ssq = ssq + p * p
        inv = lax.rsqrt(jnp.sum(ssq, axis=-1, keepdims=True) * (1.0 / D_MODEL) + NORM_EPS)
        xn = jnp.concatenate([(p * inv * g_ref[:, a * HEAD_DIM:(a + 1) * HEAD_DIM]).astype(BF16)
                              for a, p in enumerate(pieces)], axis=1)
        gu = _dot(xn, wup_b[...])
        act = (_silu(gu[:, :EXPERT_HIDDEN]) * gu[:, EXPERT_HIDDEN:]).astype(BF16)
        y = _dot(act, wdn_b[...])
        for a in range(ROW_TILES):
            ys_ref[pl.ds(a, MOE_BLOCK, stride=ROW_TILES), :] = y[:, a * HEAD_DIM:(a + 1) * HEAD_DIM]


def experts(xs, norm_g, w_up, w_down, layer, block_e, n_used, next1, next2, slot):
    n_blocks = block_e.shape[0]
    d = D_MODEL
    ns = EXPERT_WEIGHT_SLOTS
    rows = lambda b, be, nu, n1, n2, sl: (jnp.minimum(b, nu[0] - 1), 0)
    grid_spec = pltpu.PrefetchScalarGridSpec(
        num_scalar_prefetch=5,
        grid=(n_blocks,),
        in_specs=[pl.BlockSpec((MOE_BLOCK * ROW_TILES, HEAD_DIM), rows),
                  pl.BlockSpec((1, d), lambda b, be, nu, n1, n2, sl: (0, 0)),
                  pl.BlockSpec(memory_space=pl.ANY), pl.BlockSpec(memory_space=pl.ANY)],
        out_specs=pl.BlockSpec((MOE_BLOCK * ROW_TILES, HEAD_DIM), lambda b, be, nu, n1, n2, sl: (b, 0)),
        scratch_shapes=[pltpu.VMEM((ns, d, 2 * EXPERT_HIDDEN), F32), pltpu.VMEM((ns, EXPERT_HIDDEN, d), F32),
                        pltpu.VMEM((d, 2 * EXPERT_HIDDEN), BF16), pltpu.VMEM((EXPERT_HIDDEN, d), BF16),
                        pltpu.SemaphoreType.DMA((2, ns))],
    )
    return pl.pallas_call(
        functools.partial(_expert_body, layer=layer),
        grid_spec=grid_spec,
        out_shape=jax.ShapeDtypeStruct(xs.shape, F32),
        compiler_params=_cparams(("arbitrary",)),
        name="experts",
    )(block_e, n_used, next1, next2, slot, xs, norm_g.reshape(1, d), w_up, w_down)


def dispatch_tables(eid, t):
    n_assign = t * TOP_K
    flat_e = eid.reshape(n_assign)
    onehot = (flat_e[:, None] == jnp.arange(N_EXPERTS, dtype=jnp.int32)[None, :]).astype(jnp.int32)
    csum = jnp.cumsum(onehot, axis=0)
    counts = csum[-1]
    rank = jnp.sum(csum * onehot, axis=1) - 1
    padded = (counts + MOE_BLOCK - 1) // MOE_BLOCK * MOE_BLOCK
    pad_end = jnp.cumsum(padded)
    pad_start = pad_end - padded
    dest = pad_start[flat_e] + rank
    n_blocks = -(-(n_assign + N_EXPERTS * (MOE_BLOCK - 1)) // MOE_BLOCK)
    n_slots = n_blocks * MOE_BLOCK
    zrow = jnp.where(padded > 0, (pad_end - MOE_BLOCK) * ROW_TILES, -1).astype(jnp.int32)
    n_used = pad_end[-1] // MOE_BLOCK
    starts = jnp.arange(n_blocks, dtype=jnp.int32) * MOE_BLOCK
    block_e = jnp.sum((pad_end[None, :] <= starts[:, None]).astype(jnp.int32), axis=1)
    block_e = jnp.minimum(block_e, N_EXPERTS - 1)
    last_e = block_e[jnp.maximum(n_used - 1, 0)]
    block_e = jnp.where(jnp.arange(n_blocks) < n_used, block_e, last_e)
    ids = jnp.arange(N_EXPERTS, dtype=jnp.int32)
    later = (ids[None, :] > ids[:, None]) & (counts[None, :] > 0)
    next_nonempty = jnp.min(jnp.where(later, ids[None, :], N_EXPERTS), axis=1)
    next_nonempty = jnp.where(next_nonempty == N_EXPERTS, -1, next_nonempty).astype(jnp.int32)
    rank_nonempty = jnp.cumsum((counts > 0).astype(jnp.int32)) - 1
    next1 = next_nonempty[block_e]
    next2 = jnp.where(next1 >= 0, next_nonempty[jnp.maximum(next1, 0)], -1)
    slot = (rank_nonempty[block_e] % EXPERT_WEIGHT_SLOTS).astype(jnp.int32)
    return (block_e, n_used.reshape(1).astype(jnp.int32), next1, next2, slot,
            (dest * ROW_TILES).astype(jnp.int32), zrow, n_slots)


def _combine_rows(dest_ref, ys_hbm, ybuf, sem, tile, slot, tm, wait):
    def body(r, c):
        a0 = (tile * tm + r) * TOP_K
        for k in range(TOP_K):
            if wait:
                _row_copy(ys_hbm, 0, ybuf.at[slot, k], 0, sem.at[slot]).wait()
            else:
                _row_copy(ys_hbm, dest_ref[a0 + k], ybuf.at[slot, k], r * ROW_TILES, sem.at[slot]).start()
        return c

    lax.fori_loop(0, tm, body, 0, unroll=8)


def _combine_body(dest_ref, x_ref, w_ref, g_ref, ys_hbm, x2_ref, hn_ref, ybuf, sem, *, tm):
    i = pl.program_id(0)
    slot = i % 2

    @pl.when(i == 0)
    def _():
        _combine_rows(dest_ref, ys_hbm, ybuf, sem, i, slot, tm, wait=False)

    _combine_rows(dest_ref, ys_hbm, ybuf, sem, i, slot, tm, wait=True)

    @pl.when(i + 1 < pl.num_programs(0))
    def _():
        _combine_rows(dest_ref, ys_hbm, ybuf, sem, i + 1, 1 - slot, tm, wait=False)

    w = w_ref[...]
    w0, w1 = w[:, 0:1], w[:, 1:2]
    ssq = None
    for a in range(ROW_TILES):
        rows = pl.ds(a, tm, stride=ROW_TILES)
        piece = x_ref[rows, :] + (w0 * ybuf[slot, 0, rows, :] + w1 * ybuf[slot, 1, rows, :])
        x2_ref[:, a * HEAD_DIM:(a + 1) * HEAD_DIM] = piece
        ssq = piece * piece if ssq is None else ssq + piece * piece
    inv = lax.rsqrt(jnp.sum(ssq, axis=-1, keepdims=True) * (1.0 / D_MODEL) + NORM_EPS)
    hn_ref[...] = (x2_ref[...] * inv * g_ref[...]).astype(hn_ref.dtype)


def combine(x1, ys, dest, wts, next_g, next_dtype, tm=256):
    t = x1.shape[0] // ROW_TILES
    d = D_MODEL
    row = pl.BlockSpec((tm, d), lambda i, dest: (i, 0))
    grid_spec = pltpu.PrefetchScalarGridSpec(
        num_scalar_prefetch=1,
        grid=(t // tm,),
        in_specs=[pl.BlockSpec((tm * ROW_TILES, HEAD_DIM), lambda i, dest: (i, 0)),
                  pl.BlockSpec((tm, HEAD_DIM), lambda i, dest: (i, 0)),
                  pl.BlockSpec((1, d), lambda i, dest: (0, 0)),
                  pl.BlockSpec(memory_space=pl.ANY)],
        out_specs=[row, row],
        scratch_shapes=[pltpu.VMEM((2, TOP_K, tm * ROW_TILES, HEAD_DIM), F32), pltpu.SemaphoreType.DMA((2,))],
    )
    return pl.pallas_call(
        functools.partial(_combine_body, tm=tm),
        grid_spec=grid_spec,
        out_shape=[jax.ShapeDtypeStruct((t, d), F32), jax.ShapeDtypeStruct((t, d), next_dtype)],
        compiler_params=_cparams(("arbitrary",)),
        name="moe_combine",
    )(dest, x1, wts, next_g.reshape(1, d), ys)


def _layer(x, h, cos_t, sin_t, layer, w_bd, w_all, conv_a_w, a_log, dt_bias, gdn_norm_g, conv_b_w, conv_b_b,
           ln_b_g, ln_b_b, pool_w, pool_scale, w_branch, w_out, norm2_g, wg, bg, we, be, w_up, w_down,
           next_g, next_dtype):
    t = x.shape[0]
    pool_bd = jnp.zeros((WIDTH, WIDTH), F32)
    for gi in range(len(POOL_WINDOWS)):
        pool_bd = lax.dynamic_update_slice(pool_bd, pool_w[gi], (gi * POOL_GROUP, gi * POOL_GROUP))
    w_router = jnp.pad(jnp.concatenate([wg, we], axis=1), ((0, 0), (0, HEAD_DIM - N_GROUPS - N_EXPERTS)))
    b_router = jnp.pad(jnp.concatenate([bg, be]), (0, HEAD_DIM - N_GROUPS - N_EXPERTS)).reshape(1, HEAD_DIM)

    u = project_mixers(h, w_all, layer, cos_t, sin_t)
    bd = project(h, w_bd, layer, HEAD_DIM)
    ya = gated_deltanet(u, bd, conv_a_w, a_log, dt_bias, gdn_norm_g)
    yb, yd = local_mixers(u, conv_b_w, conv_b_b, ln_b_g, ln_b_b, pool_bd.astype(BF16), pool_scale)
    yc = lse = None
    for gi, (_, dilation) in enumerate(ATT_PATTERNS):
        yc, lse = attention_group(u, gi, dilation, yc, lse)
    x1, rl = merge_project(h, ya, yb, yc, yd, w_all, w_branch, w_out, layer, x, norm2_g, w_router, b_router)
    eid, wts = route(rl)
    block_e, n_used, next1, next2, slot, dest, zrow, n_slots = dispatch_tables(eid[:, :TOP_K], t)
    xs = sort_rows(x1, dest, zrow, n_used, n_slots)
    ys = experts(xs, norm2_g, w_up, w_down, layer, block_e, n_used, next1, next2, slot)
    return combine(x1, ys, dest, wts, next_g, next_dtype)


def kernel(x, positions, norm1_g, w_in, conv_a_w, a_log, dt_bias, gdn_norm_g, conv_b_w, conv_b_b, ln_b_g, ln_b_b,
           pool_w, pool_scale, w_branch, w_out, norm2_g, router_group_w, router_group_b, router_expert_w,
           router_expert_b, w_up, w_down, final_norm_g):
    b_, s_, d = x.shape
    depth = w_in.shape[0]
    outs = []
    w_all = prepare_w_in(w_in)
    w_bd = beta_decay_weights(w_in)
    w_branch_b = w_branch.astype(BF16)
    w_out_b = w_out.astype(BF16)
    for bi in range(b_):
        xb = x[bi]
        cos_t, sin_t = rope_tables(positions[bi])
        h = rmsnorm(xb, norm1_g[0], BF16)
        for layer in range(depth):
            last = layer == depth - 1
            next_g = final_norm_g if last else norm1_g[layer + 1]
            xb, h = _layer(xb, h, cos_t, sin_t, layer, w_bd, w_all, conv_a_w[layer], a_log[layer], dt_bias[layer],
                           gdn_norm_g[layer], conv_b_w[layer], conv_b_b[layer], ln_b_g[layer], ln_b_b[layer],
                           pool_w[layer], pool_scale[layer], w_branch_b, w_out_b, norm2_g[layer],
                           router_group_w[layer], router_group_b[layer], router_expert_w[layer],
                           router_expert_b[layer], w_up, w_down, next_g, F32 if last else BF16)
        outs.append(h)
    return jnp.stack(outs, axis=0)
```

```python
import functools

import jax
import jax.numpy as jnp
from jax import lax
from jax.experimental import pallas as pl
from jax.experimental.pallas import tpu as pltpu

F32 = jnp.float32
BF16 = jnp.bfloat16

D_MODEL = 2048
N_HEADS = 6
HEAD_DIM = 128
WIDTH = N_HEADS * HEAD_DIM
GDN_CONV = 4
GDN_CHUNK = 64
GDN_SUPER = 256
CONF_CONV = 31
ATT_PATTERNS = ((128, 1), (512, 4), (2048, 16))
ATT_BLOCK = 128
ATT_UNITS_IN_FLIGHT = 6
ATT_HEADS_PER_STEP = {1: 6, 4: 1, 16: 1}
ROPE_THETA = 10000.0
ROPE_ROWS = 256
POOL_WINDOWS = (2, 4, 8, 16)
POOL_GROUP = 192
N_BRANCHES = 4
N_GROUPS = 8
PER_GROUP = 8
N_EXPERTS = 64
TOP_K = 2
EXPERT_HIDDEN = 512
MOE_BLOCK = 128
EXPERT_WEIGHT_SLOTS = 3
LOCAL_ROWS = 32
NORM_EPS = 1e-6
LN_EPS = 1e-5

U_QKV_A = 0
U_Z_A = 2304
U_B = 3072
U_C = 4608
U_D = 11520
U_WIDTH = 12288
W_BETA = 3072
W_UB = 3084
W_GATE = 12300

VMEM_LIMIT = 56 * 1024 * 1024


def _cparams(semantics, vmem=VMEM_LIMIT):
    return pltpu.CompilerParams(dimension_semantics=semantics, vmem_limit_bytes=vmem)


def _sigmoid(x):
    return 1.0 / (1.0 + jnp.exp(-x))


def _silu(x):
    return x * _sigmoid(x)


def _dot(a, b):
    return jnp.dot(a, b, preferred_element_type=F32)


def _dot_nt(a, b):
    return lax.dot_general(a, b, (((1,), (1,)), ((), ())), preferred_element_type=F32)


def _dot_tn(a, b):
    return lax.dot_general(a, b, (((0,), (0,)), ((), ())), preferred_element_type=F32)


def _rmsnorm_body(x_ref, g_ref, o_ref):
    x = x_ref[...]
    y = x * lax.rsqrt(jnp.mean(x * x, axis=-1, keepdims=True) + NORM_EPS) * g_ref[...]
    o_ref[...] = y.astype(o_ref.dtype)


def rmsnorm(x, g, out_dtype, tm=512):
    t, d = x.shape
    return pl.pallas_call(
        _rmsnorm_body,
        grid=(t // tm,),
        in_specs=[pl.BlockSpec((tm, d), lambda i: (i, 0)), pl.BlockSpec((1, d), lambda i: (0, 0))],
        out_specs=pl.BlockSpec((tm, d), lambda i: (i, 0)),
        out_shape=jax.ShapeDtypeStruct((t, d), out_dtype),
        compiler_params=_cparams(("parallel",)),
        name="rmsnorm",
    )(x, g.reshape(1, d))


def _wprep_copy(wt_hbm, buf, sem, j, slot, *, tn, n_plain, shift):
    row0 = j * tn + jnp.where(j >= n_plain, shift, 0)
    return pltpu.make_async_copy(wt_hbm.at[pl.ds(row0, tn)], buf.at[slot], sem.at[slot])


def _wprep_body(wt_hbm, o_ref, buf, sem, *, tn, n_plain, shift):
    j = pl.program_id(0)
    slot = j % 2
    cp = functools.partial(_wprep_copy, wt_hbm, buf, sem, tn=tn, n_plain=n_plain, shift=shift)

    @pl.when(j == 0)
    def _():
        cp(j, slot).start()

    cp(j, slot).wait()

    @pl.when(j + 1 < pl.num_programs(0))
    def _():
        cp(j + 1, 1 - slot).start()

    for l in range(o_ref.shape[0]):
        o_ref[l] = buf[slot, :, l, :].T.astype(BF16)


def prepare_w_in(w_in, tn=256):
    depth, k, n = w_in.shape
    w_t = jnp.transpose(w_in, (2, 0, 1))
    n_out = W_BETA + (n - W_UB)
    return pl.pallas_call(
        functools.partial(_wprep_body, tn=tn, n_plain=W_BETA // tn, shift=W_UB - W_BETA),
        grid=(n_out // tn,),
        in_specs=[pl.BlockSpec(memory_space=pl.ANY)],
        out_specs=pl.BlockSpec((depth, k, tn), lambda j: (0, 0, j)),
        out_shape=jax.ShapeDtypeStruct((depth, k, n_out), BF16),
        scratch_shapes=[pltpu.VMEM((2, tn, depth, k), F32), pltpu.SemaphoreType.DMA((2,))],
        compiler_params=_cparams(("arbitrary",)),
        name="prepare_w_in",
    )(w_t)


def _bd_weight_body(w_ref, o_ref):
    for l in range(o_ref.shape[0]):
        o_ref[l] = w_ref[:, l, :].T.astype(BF16)


def beta_decay_weights(w_in):
    depth, k, _ = w_in.shape
    w_t = jnp.transpose(w_in, (2, 0, 1))
    return pl.pallas_call(
        _bd_weight_body,
        grid=(1,),
        in_specs=[pl.BlockSpec((HEAD_DIM, depth, k), lambda i: (W_BETA // HEAD_DIM, 0, 0))],
        out_specs=pl.BlockSpec((depth, k, HEAD_DIM), lambda i: (0, 0, 0)),
        out_shape=jax.ShapeDtypeStruct((depth, k, HEAD_DIM), BF16),
        compiler_params=_cparams(("arbitrary",)),
        name="beta_decay_weights",
    )(w_t)


def _matmul_body(h_ref, w_ref, o_ref):
    o_ref[...] = _dot(h_ref[...], w_ref[...].astype(BF16)).astype(o_ref.dtype)


def project(h, w, layer, n, col0=0, tm=1024, tn=768):
    t, k = h.shape
    tn = min(tn, n)
    return pl.pallas_call(
        _matmul_body,
        grid=(n // tn, t // tm),
        in_specs=[pl.BlockSpec((tm, k), lambda j, i: (i, 0)),
                  pl.BlockSpec((None, k, tn), lambda j, i: (layer, 0, col0 // tn + j))],
        out_specs=pl.BlockSpec((tm, tn), lambda j, i: (i, j)),
        out_shape=jax.ShapeDtypeStruct((t, n), F32),
        compiler_params=_cparams(("parallel", "parallel")),
        name="in_proj",
    )(h, w)


def _rope(x, cos, sin):
    return x * cos + pltpu.roll(x, HEAD_DIM // 2, axis=1) * sin


def _matmul_rope_body(h_ref, w_ref, cos_ref, sin_ref, o_ref, *, q_tiles, k_tiles):
    j = pl.program_id(0)
    is_q = (j >= q_tiles[0]) & (j < q_tiles[1])
    is_k = (j >= k_tiles[0]) & (j < k_tiles[1])

    @pl.when(jnp.logical_not(is_q | is_k))
    def _():
        o_ref[...] = _dot(h_ref[...], w_ref[...])

    def rotated(scale):
        for r0 in range(0, o_ref.shape[0], ROPE_ROWS):
            rs = slice(r0, r0 + ROPE_ROWS)
            acc = _dot(h_ref[rs, :], w_ref[...])
            cos, sin = cos_ref[rs, :], sin_ref[rs, :]
            for c in range(o_ref.shape[1] // HEAD_DIM):
                sl = slice(c * HEAD_DIM, (c + 1) * HEAD_DIM)
                o_ref[rs, sl] = _rope(acc[:, sl], cos, sin) * scale

    @pl.when(is_q)
    def _():
        rotated(HEAD_DIM ** -0.5)

    @pl.when(is_k)
    def _():
        rotated(1.0)


def project_mixers(h, w, layer, cos_t, sin_t, tm=1024, tn=WIDTH):
    t, k = h.shape
    q0 = U_C // tn
    nq = 3 * WIDTH // tn
    tab = pl.BlockSpec((tm, HEAD_DIM), lambda j, i: (i, 0))
    return pl.pallas_call(
        functools.partial(_matmul_rope_body, q_tiles=(q0, q0 + nq), k_tiles=(q0 + nq, q0 + 2 * nq)),
        grid=(U_WIDTH // tn, t // tm),
        in_specs=[pl.BlockSpec((tm, k), lambda j, i: (i, 0)),
                  pl.BlockSpec((None, k, tn), lambda j, i: (layer, 0, j)), tab, tab],
        out_specs=pl.BlockSpec((tm, tn), lambda j, i: (i, j)),
        out_shape=jax.ShapeDtypeStruct((t, U_WIDTH), F32),
        compiler_params=_cparams(("parallel", "parallel")),
        name="in_proj",
    )(h, w, cos_t, sin_t)


def _rope_body(pos_ref, inv_ref, cos_ref, sin_ref):
    ang = pos_ref[...].astype(F32) * inv_ref[...]
    lane = lax.broadcasted_iota(jnp.int32, ang.shape, 1)
    cos_ref[...] = jnp.cos(ang)
    s = jnp.sin(ang)
    sin_ref[...] = jnp.where(lane < HEAD_DIM // 2, -s, s)


def rope_tables(pos, tm=1024):
    t = pos.shape[0]
    half = HEAD_DIM // 2
    inv = ROPE_THETA ** (-jnp.arange(half, dtype=F32) * 2.0 / HEAD_DIM)
    inv = jnp.concatenate([inv, inv]).reshape(1, HEAD_DIM)
    spec = pl.BlockSpec((tm, HEAD_DIM), lambda i: (i, 0))
    return pl.pallas_call(
        _rope_body,
        grid=(t // tm,),
        in_specs=[pl.BlockSpec((tm, 1), lambda i: (i, 0)), pl.BlockSpec((1, HEAD_DIM), lambda i: (0, 0))],
        out_specs=[spec, spec],
        out_shape=[jax.ShapeDtypeStruct((t, HEAD_DIM), F32)] * 2,
        compiler_params=_cparams(("parallel",)),
        name="rope_tables",
    )(pos.reshape(t, 1), inv)


def _attn_body(*refs, first, d, hw):
    if first:
        q_ref, kc_ref, kp_ref, vc_ref, vp_ref, o_out, lse_out = refs
        o_in = lse_in = None
    else:
        q_ref, kc_ref, kp_ref, vc_ref, vp_ref, o_in, lse_in, o_out, lse_out = refs
    n = pl.program_id(0)
    blk = ATT_BLOCK
    row = lax.broadcasted_iota(jnp.int32, (blk, blk), 0)
    col = lax.broadcasted_iota(jnp.int32, (blk, blk), 1)
    in_cur = col <= row
    has_prev = n > 0
    neg = -jnp.inf
    units = [(pl.ds(r, blk, stride=d) if d > 1 else pl.ds(0, blk), slice(h * HEAD_DIM, (h + 1) * HEAD_DIM))
             for r in range(d) for h in range(hw)]
    for g0 in range(0, len(units), ATT_UNITS_IN_FLIGHT):
        group = units[g0:g0 + ATT_UNITS_IN_FLIGHT]
        scores = []
        for rows, sl in group:
            q = q_ref[rows, sl]
            kp = kp_ref[rows, sl]
            qb = q.astype(BF16)
            s_c = _dot_nt(qb, kc_ref[rows, sl].astype(BF16))
            s_p = _dot_nt(qb, kp.astype(BF16))
            s_d = jnp.where(has_prev, jnp.sum(q * kp, axis=-1, keepdims=True), neg)
            scores.append((jnp.where(in_cur, s_c, jnp.where(has_prev, s_p, neg)), s_d))
        probs = []
        for (rows, sl), (s_w, s_d) in zip(group, scores):
            m = jnp.maximum(jnp.max(s_w, axis=-1, keepdims=True), s_d)
            m_old = None
            if not first:
                m_old = lse_in[rows, sl][:, 0:1]
                m = jnp.maximum(m, m_old)
            p_w = jnp.exp(s_w - m)
            p_d = jnp.exp(s_d - m)
            probs.append((m, m_old, p_w, p_d, jnp.sum(p_w, axis=-1, keepdims=True) + p_d))
        for (rows, sl), (m, m_old, p_w, p_d, l) in zip(group, probs):
            vp = vp_ref[rows, sl]
            acc = (_dot(jnp.where(in_cur, p_w, 0.0).astype(BF16), vc_ref[rows, sl].astype(BF16))
                   + _dot(jnp.where(in_cur, 0.0, p_w).astype(BF16), vp.astype(BF16)) + p_d * vp)
            if not first:
                w_old = jnp.exp(m_old - m)
                l = l + w_old
                acc = acc + w_old * o_in[rows, sl]
            o_out[rows, sl] = acc / l
            lse_out[rows, sl] = jnp.broadcast_to(m + jnp.log(l), (blk, HEAD_DIM))


def attention_group(u, group, dilation, o_prev, lse_prev):
    t = u.shape[0]
    d = dilation
    span = ATT_BLOCK * d
    hw = ATT_HEADS_PER_STEP[d]
    cw = hw * HEAD_DIM
    first = o_prev is None
    qoff = (U_C + group * WIDTH) // cw
    koff = (U_C + 3 * WIDTH + group * WIDTH) // cw
    voff = (U_C + 6 * WIDTH + group * WIDTH) // cw

    def cur(off):
        return pl.BlockSpec((span, cw), lambda n, g: (n, off + g))

    def prev(off):
        return pl.BlockSpec((span, cw), lambda n, g: (jnp.maximum(n - 1, 0), off + g))

    o_spec = cur(0)
    in_specs = [cur(qoff), cur(koff), prev(koff), cur(voff), prev(voff)]
    args = [u, u, u, u, u]
    if not first:
        in_specs += [o_spec, o_spec]
        args += [o_prev, lse_prev]
    return pl.pallas_call(
        functools.partial(_attn_body, first=first, d=d, hw=hw),
        grid=(t // span, N_HEADS // hw),
        in_specs=in_specs,
        out_specs=[o_spec, o_spec],
        out_shape=[jax.ShapeDtypeStruct((t, WIDTH), F32), jax.ShapeDtypeStruct((t, WIDTH), F32)],
        compiler_params=_cparams(("parallel", "parallel")),
        name=f"attn_d{d}",
    )(*args)


def _local_body(ub_ref, ud_ref, cw_ref, cb_ref, lg_ref, lb_ref, pw_ref, ps_ref, yb_ref, yd_ref, extb, extd, pool_buf, *, tm):
    i = pl.program_id(0)
    hb, hd = 32, 16

    @pl.when(i == 0)
    def _():
        extb[0:hb, :] = jnp.zeros((hb, WIDTH), F32)
        extd[0:hd, :] = jnp.zeros((hd, WIDTH), F32)

    @pl.when(i > 0)
    def _():
        extb[0:hb, :] = extb[tm:tm + hb, :]
        extd[0:hd, :] = extd[tm:tm + hd, :]

    ub = ub_ref[...]
    extb[hb:hb + tm, :] = ub[:, :WIDTH] * _sigmoid(ub[:, WIDTH:])
    extd[hd:hd + tm, :] = ud_ref[...]

    rc = LOCAL_ROWS
    ch = lax.broadcasted_iota(jnp.int32, (rc, WIDTH), 1)
    rows = lax.broadcasted_iota(jnp.int32, (rc, WIDTH), 0)
    for c0 in range(0, tm, rc):
        acc = None
        for s in range(8):
            win = rc + (8 if s else 0)
            part = None
            for j in range(CONF_CONV):
                off = hb - (CONF_CONV - 1) + j
                if off % 8 != s:
                    continue
                base = c0 + off - s
                term = cw_ref[j:j + 1, :] * extb[base:base + win, :]
                part = term if part is None else part + term
            if part is None:
                continue
            part = part[s:s + rc, :]
            acc = part if acc is None else acc + part
        cb = acc + cb_ref[...]
        mu = jnp.mean(cb, axis=-1, keepdims=True)
        cen = cb - mu
        var = jnp.mean(cen * cen, axis=-1, keepdims=True)
        yb = _silu(cen * lax.rsqrt(var + LN_EPS) * lg_ref[...] + lb_ref[...])
        yb_ref[c0:c0 + rc, :] = yb.astype(yb_ref.dtype)

        xd = extd[c0 + hd:c0 + hd + rc, :]
        tpos = i * tm + c0 + rows + 1
        run = xd
        pooled = None
        shift = 1
        for gi, w in enumerate(POOL_WINDOWS):
            while shift < w:
                run = run + extd[c0 + hd - shift:c0 + hd - shift + rc, :]
                shift += 1
            val = run / jnp.minimum(tpos, w).astype(F32) - xd
            pooled = val if pooled is None else jnp.where(ch >= gi * POOL_GROUP, val, pooled)
        pool_buf[c0:c0 + rc, :] = pooled.astype(pool_buf.dtype)
    yd = _dot(pool_buf[...], pw_ref[...]) * ps_ref[...]
    yd_ref[...] = yd.astype(yd_ref.dtype)


def local_mixers(u, conv_w, conv_b, ln_g, ln_b, pool_w_bd, pool_scale, tm=256):
    t = u.shape[0]
    row = lambda a: a.reshape(1, WIDTH)
    vec = pl.BlockSpec((1, WIDTH), lambda i: (0, 0))
    out = pl.BlockSpec((tm, WIDTH), lambda i: (i, 0))
    return pl.pallas_call(
        functools.partial(_local_body, tm=tm),
        grid=(t // tm,),
        in_specs=[pl.BlockSpec((tm, 2 * WIDTH), lambda i: (i, U_B // (2 * WIDTH))),
                  pl.BlockSpec((tm, WIDTH), lambda i: (i, U_D // WIDTH)),
                  pl.BlockSpec((CONF_CONV, WIDTH), lambda i: (0, 0)), vec, vec, vec,
                  pl.BlockSpec((WIDTH, WIDTH), lambda i: (0, 0)), vec],
        out_specs=[out, out],
        out_shape=[jax.ShapeDtypeStruct((t, WIDTH), BF16)] * 2,
        scratch_shapes=[pltpu.VMEM((tm + 32, WIDTH), F32), pltpu.VMEM((tm + 16, WIDTH), F32),
                        pltpu.VMEM((tm, WIDTH), BF16)],
        compiler_params=_cparams(("arbitrary",)),
        name="conv_pool",
    )(u, u, conv_w, row(conv_b), row(ln_g), row(ln_b), pool_w_bd, row(pool_scale))


def _gdn_body(qkv_ref, z_ref, bd_ref, cw_ref, alog_ref, dtb_ref, gn_ref, ya_ref, ext, state):
    s = pl.program_id(0)
    sup, c = GDN_SUPER, GDN_CHUNK
    halo = 8

    @pl.when(s == 0)
    def _():
        ext[0:halo, :] = jnp.zeros((halo, 3 * WIDTH), F32)
        state[...] = jnp.zeros(state.shape, F32)

    @pl.when(s > 0)
    def _():
        ext[0:halo, :] = ext[sup:sup + halo, :]

    ext[halo:halo + sup, :] = qkv_ref[...]
    acc = jnp.zeros((sup, 3 * WIDTH), F32)
    for j in range(GDN_CONV):
        off = halo - (GDN_CONV - 1) + j
        acc = acc + cw_ref[j:j + 1, :] * ext[off:off + sup, :]
    qkv = _silu(acc)

    bd = bd_ref[...]
    beta_all = _sigmoid(bd)
    xs = bd + dtb_ref[...]
    softplus = jnp.maximum(xs, 0.0) + jnp.log(1.0 + jnp.exp(-jnp.abs(xs)))
    g_all = -jnp.exp(alog_ref[...]) * softplus
    rowi = lax.broadcasted_iota(jnp.int32, (sup, HEAD_DIM), 0)
    in_chunk = rowi & (c - 1)
    gc = g_all
    sh = 1
    while sh < c:
        gc = gc + jnp.where(in_chunk >= sh, pltpu.roll(gc, sh, axis=0), 0.0)
        sh *= 2
    gct = gc.T

    ri = lax.broadcasted_iota(jnp.int32, (sup, sup), 0)
    ci = lax.broadcasted_iota(jnp.int32, (sup, sup), 1)
    same = (ri & -c) == (ci & -c)
    incl = same & (ri >= ci)
    strict = same & (ri > ci)
    eye = (ri == ci).astype(F32)

    heads = range(N_HEADS)
    hs = [slice(h * HEAD_DIM, (h + 1) * HEAD_DIM) for h in heads]
    prep = []
    for h in heads:
        q = qkv[:, hs[h]]
        k = qkv[:, WIDTH + h * HEAD_DIM:WIDTH + (h + 1) * HEAD_DIM]
        v = qkv[:, 2 * WIDTH + h * HEAD_DIM:2 * WIDTH + (h + 1) * HEAD_DIM]
        q = q * lax.rsqrt(jnp.sum(q * q, axis=-1, keepdims=True) + 1e-6) * (HEAD_DIM ** -0.5)
        k = k * lax.rsqrt(jnp.sum(k * k, axis=-1, keepdims=True) + 1e-6)
        b_col = beta_all[:, h:h + 1]
        gc_col = gc[:, N_HEADS + h:N_HEADS + h + 1]
        gc_row = gct[N_HEADS + h:N_HEADS + h + 1, :]
        dm = jnp.exp(jnp.where(incl, gc_col - gc_row, -jnp.inf))
        kb = k.astype(BF16)
        nmat = jnp.where(strict, -(b_col * _dot_nt(kb, kb) * dm), 0.0)
        prep.append((q, k, v, b_col, gc_col, dm, kb, nmat))

    pmats = [eye + p[7] for p in prep]
    mbs = [p[7].astype(BF16) for p in prep]
    step = 2
    while step < c:
        m2s = [_dot(mb, mb).astype(BF16) for mb in mbs]
        pmats = [pm + _dot(pm.astype(BF16), m2) for pm, m2 in zip(pmats, m2s)]
        mbs = m2s
        step *= 2

    chunked = []
    for h in heads:
        q, k, v, b_col, gc_col, dm, kb, _ = prep[h]
        eg = jnp.exp(gc_col)
        rhs = jnp.concatenate([v * b_col, k * (b_col * eg)], axis=1).astype(BF16)
        sol = _dot(pmats[h].astype(BF16), rhs)
        qk_b = (_dot_nt(q.astype(BF16), kb) * dm).astype(BF16)
        q_dec = (q * eg).astype(BF16)
        gl_col = jnp.concatenate(
            [jnp.broadcast_to(gc_col[(n + 1) * c - 1:(n + 1) * c, :], (c, 1)) for n in range(sup // c)], axis=0)
        k_dec = k * jnp.exp(gl_col - gc_col)
        chunked.append((sol[:, :HEAD_DIM], sol[:, HEAD_DIM:].astype(BF16), qk_b, q_dec, k_dec, gc_col))

    states = [state[h] for h in heads]
    outs = [[] for _ in heads]
    for n in range(sup // c):
        rs = slice(n * c, (n + 1) * c)
        for h in heads:
            u_c, w_b, qk_b, q_dec, k_dec, gc_col = chunked[h]
            sb = states[h].astype(BF16)
            v_new = u_c[rs] - _dot(w_b[rs], sb)
            vb = v_new.astype(BF16)
            outs[h].append(_dot(q_dec[rs], sb) + _dot(qk_b[rs, rs], vb))
            g_last = jnp.exp(gc_col[(n + 1) * c - 1:(n + 1) * c, :])
            states[h] = states[h] * g_last + _dot(k_dec[rs].T.astype(BF16), vb)

    for h in heads:
        state[h] = states[h]
        o = jnp.concatenate(outs[h], axis=0)
        o = o * lax.rsqrt(jnp.mean(o * o, axis=-1, keepdims=True) + NORM_EPS) * gn_ref[...]
        ya_ref[:, hs[h]] = (o * _silu(z_ref[:, hs[h]])).astype(ya_ref.dtype)


def gated_deltanet(u, bd, conv_w, a_log, dt_bias, norm_g):
    t = u.shape[0]
    sup = GDN_SUPER
    pad = jnp.zeros((N_HEADS,), F32)
    lane_row = lambda a: jnp.concatenate([pad, a, jnp.zeros((HEAD_DIM - 2 * N_HEADS,), F32)]).reshape(1, HEAD_DIM)
    vec = pl.BlockSpec((1, HEAD_DIM), lambda s: (0, 0))
    return pl.pallas_call(
        _gdn_body,
        grid=(t // sup,),
        in_specs=[pl.BlockSpec((sup, 3 * WIDTH), lambda s: (s, U_QKV_A // (3 * WIDTH))),
                  pl.BlockSpec((sup, WIDTH), lambda s: (s, U_Z_A // WIDTH)),
                  pl.BlockSpec((sup, HEAD_DIM), lambda s: (s, 0)),
                  pl.BlockSpec((GDN_CONV, 3 * WIDTH), lambda s: (0, 0)), vec, vec, vec],
        out_specs=pl.BlockSpec((sup, WIDTH), lambda s: (s, 0)),
        out_shape=jax.ShapeDtypeStruct((t, WIDTH), BF16),
        scratch_shapes=[pltpu.VMEM((sup + 8, 3 * WIDTH), F32), pltpu.VMEM((N_HEADS, HEAD_DIM, HEAD_DIM), F32)],
        compiler_params=_cparams(("arbitrary",)),
        name="gated_deltanet",
    )(u, u, bd, conv_w, lane_row(a_log), lane_row(dt_bias), norm_g.reshape(1, HEAD_DIM))


def _merge_body(h_ref, ya_ref, yb_ref, yc_ref, yd_ref, g0_ref, g1_ref, g2_ref, g3_ref,
                b0_ref, b1_ref, b2_ref, b3_ref, wo_ref, x_ref, n2_ref, wr_ref, br_ref,
                x1_ref, rl_ref, acc):
    j = pl.program_id(1)

    @pl.when(j == 0)
    def _():
        acc[...] = jnp.zeros(acc.shape, F32)

    hb = h_ref[...]
    merged = None
    for y_ref, g_ref, b_ref in ((ya_ref, g0_ref, b0_ref), (yb_ref, g1_ref, b1_ref),
                                (yc_ref, g2_ref, b2_ref), (yd_ref, g3_ref, b3_ref)):
        gate = _sigmoid(_dot(hb, g_ref[...]))
        term = gate * _dot(y_ref[...].astype(BF16), b_ref[...])
        merged = term if merged is None else merged + term
    acc[...] += _dot(merged.astype(BF16), wo_ref[...])

    @pl.when(j == pl.num_programs(1) - 1)
    def _():
        x1 = x_ref[...] + acc[...]
        x1_ref[...] = x1
        h2 =x1 * lax.rsqrt(jnp.mean(x1 * x1, axis=-1, keepdims=True) + NORM_EPS) * n2_ref[...]
        rl_ref[...] = jnp.dot(h2, wr_ref[...], preferred_element_type=F32,
                              precision=lax.Precision.HIGHEST) + br_ref[...]


def merge_project(h, ya, yb, yc, yd, w_all, w_branch, w_out, layer, x, norm2_g, w_router, b_router, tm=512, tn=256):
    t = h.shape[0]
    d = D_MODEL
    nj = d // tn
    g0 = U_WIDTH // tn
    row = pl.BlockSpec((tm, d), lambda i, j: (i, 0))
    ysp = pl.BlockSpec((tm, WIDTH), lambda i, j: (i, 0))
    gate_spec = lambda k: pl.BlockSpec((None, d, tn), lambda i, j: (layer, 0, g0 + k * nj + j))
    br_spec = lambda k: pl.BlockSpec((None, None, WIDTH, tn), lambda i, j: (layer, k, 0, j))
    lanes = pl.BlockSpec((tm, HEAD_DIM), lambda i, j: (i, 0))
    w_gate = w_all
    return pl.pallas_call(
        _merge_body,
        grid=(t // tm, nj),
        in_specs=[row, ysp, ysp, ysp, ysp] + [gate_spec(k) for k in range(4)] + [br_spec(k) for k in range(4)]
        + [pl.BlockSpec((None, tn, d), lambda i, j: (layer, j, 0)), row, pl.BlockSpec((1, d), lambda i, j: (0, 0)),
           pl.BlockSpec((d, HEAD_DIM), lambda i, j: (0, 0)), pl.BlockSpec((1, HEAD_DIM), lambda i, j: (0, 0))],
        out_specs=[row, lanes],
        out_shape=[jax.ShapeDtypeStruct((t, d), F32), jax.ShapeDtypeStruct((t, HEAD_DIM), F32)],
        scratch_shapes=[pltpu.VMEM((tm, d), F32)],
        compiler_params=_cparams(("parallel", "arbitrary")),
        name="merge_project",
    )(h, ya, yb, yc, yd, w_gate, w_gate, w_gate, w_gate, w_branch, w_branch, w_branch, w_branch,
      w_out, x, norm2_g.reshape(1, d), w_router, b_router)


def _router_body(rl_ref, eid_ref, wts_ref):
    rl = rl_ref[...]
    lane_i = lax.broadcasted_iota(jnp.int32, rl.shape, 1)
    lane = lane_i.astype(F32)
    neg = -jnp.inf
    big = 1e9
    gl = jnp.where(lane < N_GROUPS, rl, neg)
    gmax = jnp.max(gl, axis=-1, keepdims=True)
    gi = jnp.min(jnp.where(gl == gmax, lane, big), axis=-1, keepdims=True)
    gp = 1.0 / jnp.sum(jnp.exp(gl - gmax), axis=-1, keepdims=True)
    lo = N_GROUPS + gi * PER_GROUP
    el = jnp.where((lane >= lo) & (lane < lo + PER_GROUP), rl, neg)
    e1 = jnp.max(el, axis=-1, keepdims=True)
    i1 = jnp.min(jnp.where(el == e1, lane, big), axis=-1, keepdims=True)
    el2 = jnp.where(lane == i1, neg, el)
    e2 = jnp.max(el2, axis=-1, keepdims=True)
    i2 = jnp.min(jnp.where(el2 == e2, lane, big), axis=-1, keepdims=True)
    esum = jnp.sum(jnp.exp(el - e1), axis=-1, keepdims=True)
    p1 = 1.0 / esum
    p2 = jnp.exp(e2 - e1) / esum
    w1 = gp * p1 / (p1 + p2)
    w2 = gp * p2 / (p1 + p2)
    eid = jnp.where(lane_i == 0, i1 - N_GROUPS, jnp.where(lane_i == 1, i2 - N_GROUPS, 0.0))
    eid_ref[...] = eid.astype(jnp.int32)
    wts_ref[...] = jnp.where(lane_i == 0, w1, jnp.where(lane_i == 1, w2, 0.0))


def route(rl, tm=512):
    t = rl.shape[0]
    spec = pl.BlockSpec((tm, HEAD_DIM), lambda i: (i, 0))
    return pl.pallas_call(
        _router_body,
        grid=(t // tm,),
        in_specs=[spec],
        out_specs=[spec, spec],
        out_shape=[jax.ShapeDtypeStruct((t, HEAD_DIM), jnp.int32), jax.ShapeDtypeStruct((t, HEAD_DIM), F32)],
        compiler_params=_cparams(("parallel",)),
        name="router",
    )(rl)


def _row_copy(src, src_row, dst, dst_row, sem):
    return pltpu.make_async_copy(src.at[pl.ds(src_row, 1), :], dst.at[pl.ds(dst_row, 1), :], sem)


def _zero_copy(zbuf, zrow_ref, xs_hbm, zsem, e):
    start = pl.multiple_of(jnp.maximum(zrow_ref[e], 0), MOE_BLOCK)
    return pltpu.make_async_copy(zbuf, xs_hbm.at[pl.ds(start, MOE_BLOCK), :], zsem)


def _zero_block(zbuf, xs_hbm, zsem, blk):
    start = pl.multiple_of(blk * MOE_BLOCK, MOE_BLOCK)
    return pltpu.make_async_copy(zbuf, xs_hbm.at[pl.ds(start, MOE_BLOCK), :], zsem)


def _sort_rows_body(dest_ref, zrow_ref, nused_ref, x_ref, xs_hbm, zbuf, sem, zsem, *, tm):
    i = pl.program_id(0)
    sem, zsem = sem.at[0], zsem.at[0]

    @pl.when(i == 0)
    def _():
        zbuf[...] = jnp.zeros(zbuf.shape, F32)
        n_blocks = xs_hbm.shape[0] // MOE_BLOCK
        for e in range(N_EXPERTS):
            @pl.when(zrow_ref[e] >= 0)
            def _():
                _zero_copy(zbuf, zrow_ref, xs_hbm, zsem, e).start()

        def z_start(blk, c):
            _zero_block(zbuf, xs_hbm, zsem, blk).start()
            return c

        def z_wait(blk, c):
            _zero_block(zbuf, xs_hbm, zsem, blk).wait()
            return c

        lax.fori_loop(nused_ref[0], n_blocks, z_start, 0)
        for e in range(N_EXPERTS):
            @pl.when(zrow_ref[e] >= 0)
            def _():
                _zero_copy(zbuf, zrow_ref, xs_hbm, zsem, e).wait()
        lax.fori_loop(nused_ref[0], n_blocks, z_wait, 0)

    for r in range(tm):
        for k in range(TOP_K):
            _row_copy(x_ref, r, xs_hbm, dest_ref[(i * tm + r) * TOP_K + k], sem).start()
    for _ in range(tm * TOP_K):
        _row_copy(x_ref, 0, xs_hbm, 0, sem).wait()


def sort_rows(x1, dest, zrow, n_used, n_slots, tm=256):
    t, d = x1.shape
    grid_spec = pltpu.PrefetchScalarGridSpec(
        num_scalar_prefetch=3,
        grid=(t // tm,),
        in_specs=[pl.BlockSpec((tm, d), lambda i, dest, zrow, nu: (i, 0))],
        out_specs=pl.BlockSpec(memory_space=pl.ANY),
        scratch_shapes=[pltpu.VMEM((MOE_BLOCK, d), F32),
                        pltpu.SemaphoreType.DMA((1,)), pltpu.SemaphoreType.DMA((1,))],
    )
    return pl.pallas_call(
        functools.partial(_sort_rows_body, tm=tm),
        grid_spec=grid_spec,
        out_shape=jax.ShapeDtypeStruct((n_slots, d), F32),
        compiler_params=_cparams(("arbitrary",)),
        name="moe_sort_rows",
    )(dest, zrow, n_used, x1)


def _weight_copies(wup_hbm, wdn_hbm, wup_f, wdn_f, wsem, layer, e, slot):
    return (pltpu.make_async_copy(wup_hbm.at[layer, e], wup_f.at[slot], wsem.at[0, slot]),
            pltpu.make_async_copy(wdn_hbm.at[layer, e], wdn_f.at[slot], wsem.at[1, slot]))


def _expert_body(be_ref, nused_ref, nxt1_ref, nxt2_ref, par_ref, xs_ref, g_ref, wup_hbm, wdn_hbm, ys_ref,
                 wup_f, wdn_f, wup_b, wdn_b, wsem, *, layer):
    b = pl.program_id(0)
    copies = functools.partial(_weight_copies, wup_hbm, wdn_hbm, wup_f, wdn_f, wsem, layer)

    def start_if_any(e, slot):
        @pl.when(e >= 0)
        def _():
            for cp in copies(e, slot):
                cp.start()

    @pl.when(b == 0)
    def _():
        start_if_any(be_ref[0], par_ref[0])
        start_if_any(nxt1_ref[0], (par_ref[0] + 1) % EXPERT_WEIGHT_SLOTS)

    @pl.when(b >= nused_ref[0])
    def _():
        ys_ref[...] = jnp.zeros(ys_ref.shape, F32)

    @pl.when(b < nused_ref[0])
    def _():
        changed = jnp.logical_or(b == 0, be_ref[b] != be_ref[jnp.maximum(b - 1, 0)])

        @pl.when(changed)
        def _():
            slot = par_ref[b]
            for cp in copies(be_ref[b], slot):
                cp.wait()
            start_if_any(nxt2_ref[b], (slot + 2) % EXPERT_WEIGHT_SLOTS)

            wup_b[...] = wup_f[slot].astype(BF16)
            wdn_b[...] = wdn_f[slot].astype(BF16)

        xr = xs_ref[...]
        xn = xr * lax.rsqrt(jnp.mean(xr * xr, axis=-1, keepdims=True) + NORM_EPS) * g_ref[...]
        gu = _dot(xn.astype(BF16), wup_b[...])
        act = (_silu(gu[:, :EXPERT_HIDDEN]) * gu[:, EXPERT_HIDDEN:]).astype(BF16)
        ys_ref[...] = _dot(act, wdn_b[...])


def experts(xs, norm_g, w_up, w_down, layer, block_e, n_used, next1, next2, slot):
    n_blocks = block_e.shape[0]
    d = D_MODEL
    ns = EXPERT_WEIGHT_SLOTS
    rows = lambda b, be, nu, n1, n2, sl: (jnp.minimum(b, nu[0] - 1), 0)
    grid_spec = pltpu.PrefetchScalarGridSpec(
        num_scalar_prefetch=5,
        grid=(n_blocks,),
        in_specs=[pl.BlockSpec((MOE_BLOCK, d), rows),
                  pl.BlockSpec((1, d), lambda b, be, nu, n1, n2, sl: (0, 0)),
                  pl.BlockSpec(memory_space=pl.ANY), pl.BlockSpec(memory_space=pl.ANY)],
        out_specs=pl.BlockSpec((MOE_BLOCK, d), lambda b, be, nu, n1, n2, sl: (b, 0)),
        scratch_shapes=[pltpu.VMEM((ns, d, 2 * EXPERT_HIDDEN), F32), pltpu.VMEM((ns, EXPERT_HIDDEN, d), F32),
                        pltpu.VMEM((d, 2 * EXPERT_HIDDEN), BF16), pltpu.VMEM((EXPERT_HIDDEN, d), BF16),
                        pltpu.SemaphoreType.DMA((2, ns))],
    )
    return pl.pallas_call(
        functools.partial(_expert_body, layer=layer),
        grid_spec=grid_spec,
        out_shape=jax.ShapeDtypeStruct(xs.shape, F32),
        compiler_params=_cparams(("arbitrary",)),
        name="experts",
    )(block_e, n_used, next1, next2, slot, xs, norm_g.reshape(1, d), w_up, w_down)


def dispatch_tables(eid, t):
    n_assign = t * TOP_K
    flat_e = eid.reshape(n_assign)
    onehot = (flat_e[:, None] == jnp.arange(N_EXPERTS, dtype=jnp.int32)[None, :]).astype(F32)
    grp = onehot.reshape(n_assign // MOE_BLOCK, MOE_BLOCK, N_EXPERTS)
    tril = jnp.tril(jnp.ones((MOE_BLOCK, MOE_BLOCK), F32))
    inside = jnp.einsum("ij,gjk->gik", tril, grp)
    totals = inside[:, -1, :]
    csum = (inside + (jnp.cumsum(totals, axis=0) - totals)[:, None, :]).reshape(n_assign, N_EXPERTS)
    counts = csum[-1].astype(jnp.int32)
    rank = jnp.sum(csum * onehot, axis=1).astype(jnp.int32) - 1
    padded = (counts + MOE_BLOCK - 1) // MOE_BLOCK * MOE_BLOCK
    pad_end = jnp.cumsum(padded)
    pad_start = pad_end - padded
    dest = pad_start[flat_e] + rank
    n_blocks = -(-(n_assign + N_EXPERTS * (MOE_BLOCK - 1)) // MOE_BLOCK)
    n_slots = n_blocks * MOE_BLOCK
    zrow = jnp.where(padded > 0, pad_end - MOE_BLOCK, -1).astype(jnp.int32)
    n_used = pad_end[-1] // MOE_BLOCK
    starts = jnp.arange(n_blocks, dtype=jnp.int32) * MOE_BLOCK
    block_e = jnp.sum((pad_end[None, :] <= starts[:, None]).astype(jnp.int32), axis=1)
    block_e = jnp.minimum(block_e, N_EXPERTS - 1)
    last_e = block_e[jnp.maximum(n_used - 1, 0)]
    block_e = jnp.where(jnp.arange(n_blocks) < n_used, block_e, last_e)
    ids = jnp.arange(N_EXPERTS, dtype=jnp.int32)
    later = (ids[None, :] > ids[:, None]) & (counts[None, :] > 0)
    next_nonempty = jnp.min(jnp.where(later, ids[None, :], N_EXPERTS), axis=1)
    next_nonempty = jnp.where(next_nonempty == N_EXPERTS, -1, next_nonempty).astype(jnp.int32)
    rank_nonempty = jnp.cumsum((counts > 0).astype(jnp.int32)) - 1
    next1 = next_nonempty[block_e]
    next2 = jnp.where(next1 >= 0, next_nonempty[jnp.maximum(next1, 0)], -1)
    slot = (rank_nonempty[block_e] % EXPERT_WEIGHT_SLOTS).astype(jnp.int32)
    return (block_e, n_used.reshape(1).astype(jnp.int32), next1, next2, slot, dest.astype(jnp.int32), zrow,
            n_slots)


def _combine_rows(dest_ref, ys_hbm, ybuf, sem, tile, slot, tm, wait):
    for r in range(tm):
        for k in range(TOP_K):
            if wait:
                _row_copy(ys_hbm, 0, ybuf.at[slot, k], 0, sem.at[slot]).wait()
            else:
                _row_copy(ys_hbm, dest_ref[(tile * tm + r) * TOP_K + k], ybuf.at[slot, k], r, sem.at[slot]).start()


def _combine_body(dest_ref, x_ref, w_ref, g_ref, ys_hbm, x2_ref, hn_ref, ybuf, sem, *, tm):
    i = pl.program_id(0)
    slot = i % 2

    @pl.when(i == 0)
    def _():
        _combine_rows(dest_ref, ys_hbm, ybuf, sem, i, slot, tm, wait=False)

    _combine_rows(dest_ref, ys_hbm, ybuf, sem, i, slot, tm, wait=True)

    @pl.when(i + 1 < pl.num_programs(0))
    def _():
        _combine_rows(dest_ref, ys_hbm, ybuf, sem, i + 1, 1 - slot, tm, wait=False)

    w = w_ref[...]
    x2 = x_ref[...] + (w[:, 0:1] * ybuf[slot, 0] + w[:, 1:2] * ybuf[slot, 1])
    x2_ref[...] = x2
    hn = x2 * lax.rsqrt(jnp.mean(x2 * x2, axis=-1, keepdims=True) + NORM_EPS) * g_ref[...]
    hn_ref[...] = hn.astype(hn_ref.dtype)


def combine(x1, ys, dest, wts, next_g, next_dtype, tm=128):
    t, d = x1.shape
    row = pl.BlockSpec((tm, d), lambda i, dest: (i, 0))
    grid_spec = pltpu.PrefetchScalarGridSpec(
        num_scalar_prefetch=1,
        grid=(t // tm,),
        in_specs=[row, pl.BlockSpec((tm, HEAD_DIM), lambda i, dest: (i, 0)),
                  pl.BlockSpec((1, d), lambda i, dest: (0, 0)),
                  pl.BlockSpec(memory_space=pl.ANY)],
        out_specs=[row, row],
        scratch_shapes=[pltpu.VMEM((2, TOP_K, tm, d), F32), pltpu.SemaphoreType.DMA((2,))],
    )
    return pl.pallas_call(
        functools.partial(_combine_body, tm=tm),
        grid_spec=grid_spec,
        out_shape=[jax.ShapeDtypeStruct((t, d), F32), jax.ShapeDtypeStruct((t, d), next_dtype)],
        compiler_params=_cparams(("arbitrary",)),
        name="moe_combine",
    )(dest, x1, wts, next_g.reshape(1, d), ys)


def _layer(x, h, cos_t, sin_t, layer, w_bd, w_all, conv_a_w, a_log, dt_bias, gdn_norm_g, conv_b_w, conv_b_b,
           ln_b_g, ln_b_b, pool_w, pool_scale, w_branch, w_out, norm2_g, wg, bg, we, be, w_up, w_down,
           next_g, next_dtype):
    t = x.shape[0]
    pool_bd = jnp.zeros((WIDTH, WIDTH), F32)
    for gi in range(len(POOL_WINDOWS)):
        pool_bd = lax.dynamic_update_slice(pool_bd, pool_w[gi], (gi * POOL_GROUP, gi * POOL_GROUP))
    w_router = jnp.pad(jnp.concatenate([wg, we], axis=1), ((0, 0), (0, HEAD_DIM - N_GROUPS - N_EXPERTS)))
    b_router = jnp.pad(jnp.concatenate([bg, be]), (0, HEAD_DIM - N_GROUPS - N_EXPERTS)).reshape(1, HEAD_DIM)

    u = project_mixers(h, w_all, layer, cos_t, sin_t)
    bd = project(h, w_bd, layer, HEAD_DIM)
    ya = gated_deltanet(u, bd, conv_a_w, a_log, dt_bias, gdn_norm_g)
    yb, yd = local_mixers(u, conv_b_w, conv_b_b, ln_b_g, ln_b_b, pool_bd.astype(BF16), pool_scale)
    yc = lse = None
    for gi, (_, dilation) in enumerate(ATT_PATTERNS):
        yc, lse = attention_group(u, gi, dilation, yc, lse)
    x1, rl = merge_project(h, ya, yb, yc, yd, w_all, w_branch, w_out, layer, x, norm2_g, w_router, b_router)
    eid, wts = route(rl)
    block_e, n_used, next1, next2, slot, dest, zrow, n_slots = dispatch_tables(eid[:, :TOP_K], t)
    xs = sort_rows(x1, dest, zrow, n_used, n_slots)
    ys = experts(xs, norm2_g, w_up, w_down, layer, block_e, n_used, next1, next2, slot)
    return combine(x1, ys, dest, wts, next_g, next_dtype)


def kernel(x, positions, norm1_g, w_in, conv_a_w, a_log, dt_bias, gdn_norm_g, conv_b_w, conv_b_b, ln_b_g, ln_b_b,
           pool_w, pool_scale, w_branch, w_out, norm2_g, router_group_w, router_group_b, router_expert_w,
           router_expert_b, w_up, w_down, final_norm_g):
    b_, s_, d = x.shape
    depth = w_in.shape[0]
    outs = []
    w_all = prepare_w_in(w_in)
    w_bd = beta_decay_weights(w_in)
    w_branch_b = w_branch.astype(BF16)
    w_out_b = w_out.astype(BF16)
    for bi in range(b_):
        xb = x[bi]
        cos_t, sin_t = rope_tables(positions[bi])
        h = rmsnorm(xb, norm1_g[0], BF16)
        for layer in range(depth):
            last = layer == depth - 1
            next_g = final_norm_g if last else norm1_g[layer + 1]
            xb, h = _layer(xb, h, cos_t, sin_t, layer, w_bd, w_all, conv_a_w[layer], a_log[layer], dt_bias[layer],
                           gdn_norm_g[layer], conv_b_w[layer], conv_b_b[layer], ln_b_g[layer], ln_b_b[layer],
                           pool_w[layer], pool_scale[layer], w_branch_b, w_out_b, norm2_g[layer],
                           router_group_w[layer], router_group_b[layer], router_expert_w[layer],
                           router_expert_b[layer], w_up, w_down, next_g, F32 if last else BF16)
        outs.append(h)
    return jnp.stack(outs, axis=0)
```

```python
import functools

import jax
import jax.numpy as jnp
from jax import lax
from jax.experimental import pallas as pl
from jax.experimental.pallas import tpu as pltpu

F32 = jnp.float32
BF16 = jnp.bfloat16

D_MODEL = 2048
N_HEADS = 6
HEAD_DIM = 128
WIDTH = N_HEADS * HEAD_DIM
GDN_CONV = 4
GDN_CHUNK = 64
GDN_SUPER = 256
CONF_CONV = 31
ATT_PATTERNS = ((128, 1), (512, 4), (2048, 16))
ATT_BLOCK = 128
ATT_UNITS_IN_FLIGHT = 6
ATT_HEADS_PER_STEP = {1: 6, 4: 3, 16: 2}
ROPE_THETA = 10000.0
ROPE_ROWS = 256
POOL_WINDOWS = (2, 4, 8, 16)
POOL_GROUP = 192
N_BRANCHES = 4
N_GROUPS = 8
PER_GROUP = 8
N_EXPERTS = 64
TOP_K = 2
EXPERT_HIDDEN = 512
MOE_BLOCK = 128
EXPERT_WEIGHT_SLOTS = 3
LOCAL_ROWS = 32
NORM_EPS = 1e-6
LN_EPS = 1e-5

U_QKV_A = 0
U_Z_A = 2304
U_B = 3072
U_C = 4608
U_D = 11520
U_WIDTH = 12288
W_BETA = 3072
W_UB = 3084
W_GATE = 12300

VMEM_LIMIT = 56 * 1024 * 1024


def _cparams(semantics, vmem=VMEM_LIMIT):
    return pltpu.CompilerParams(dimension_semantics=semantics, vmem_limit_bytes=vmem)


def _sigmoid(x):
    return 1.0 / (1.0 + jnp.exp(-x))


def _silu(x):
    return x * _sigmoid(x)


def _dot(a, b):
    return jnp.dot(a, b, preferred_element_type=F32)


def _dot_nt(a, b):
    return lax.dot_general(a, b, (((1,), (1,)), ((), ())), preferred_element_type=F32)


def _dot_tn(a, b):
    return lax.dot_general(a, b, (((0,), (0,)), ((), ())), preferred_element_type=F32)


def _rmsnorm_body(x_ref, g_ref, o_ref):
    x = x_ref[...]
    y = x * lax.rsqrt(jnp.mean(x * x, axis=-1, keepdims=True) + NORM_EPS) * g_ref[...]
    o_ref[...] = y.astype(o_ref.dtype)


def rmsnorm(x, g, out_dtype, tm=512):
    t, d = x.shape
    return pl.pallas_call(
        _rmsnorm_body,
        grid=(t // tm,),
        in_specs=[pl.BlockSpec((tm, d), lambda i: (i, 0)), pl.BlockSpec((1, d), lambda i: (0, 0))],
        out_specs=pl.BlockSpec((tm, d), lambda i: (i, 0)),
        out_shape=jax.ShapeDtypeStruct((t, d), out_dtype),
        compiler_params=_cparams(("parallel",)),
        name="rmsnorm",
    )(x, g.reshape(1, d))


def _wprep_copy(wt_hbm, buf, sem, j, slot, *, tn, n_plain, shift):
    row0 = j * tn + jnp.where(j >= n_plain, shift, 0)
    return pltpu.make_async_copy(wt_hbm.at[pl.ds(row0, tn)], buf.at[slot], sem.at[slot])


def _wprep_body(wt_hbm, o_ref, buf, sem, *, tn, n_plain, shift):
    j = pl.program_id(0)
    slot = j % 2
    cp = functools.partial(_wprep_copy, wt_hbm, buf, sem, tn=tn, n_plain=n_plain, shift=shift)

    @pl.when(j == 0)
    def _():
        cp(j, slot).start()

    cp(j, slot).wait()

    @pl.when(j + 1 < pl.num_programs(0))
    def _():
        cp(j + 1, 1 - slot).start()

    for l in range(o_ref.shape[0]):
        o_ref[l] = buf[slot, :, l, :].T.astype(BF16)


def prepare_w_in(w_in, tn=512):
    depth, k, n = w_in.shape
    w_t = jnp.transpose(w_in, (2, 0, 1))
    n_out = W_BETA + (n - W_UB)
    return pl.pallas_call(
        functools.partial(_wprep_body, tn=tn, n_plain=W_BETA // tn, shift=W_UB - W_BETA),
        grid=(n_out // tn,),
        in_specs=[pl.BlockSpec(memory_space=pl.ANY)],
        out_specs=pl.BlockSpec((depth, k, tn), lambda j: (0, 0, j)),
        out_shape=jax.ShapeDtypeStruct((depth, k, n_out), BF16),
        scratch_shapes=[pltpu.VMEM((2, tn, depth, k), F32), pltpu.SemaphoreType.DMA((2,))],
        compiler_params=_cparams(("arbitrary",)),
        name="prepare_w_in",
    )(w_t)


def _bd_weight_body(w_ref, o_ref):
    for l in range(o_ref.shape[0]):
        o_ref[l] = w_ref[:, l, :].T.astype(BF16)


def beta_decay_weights(w_in):
    depth, k, _ = w_in.shape
    w_t = jnp.transpose(w_in, (2, 0, 1))
    return pl.pallas_call(
        _bd_weight_body,
        grid=(1,),
        in_specs=[pl.BlockSpec((HEAD_DIM, depth, k), lambda i: (W_BETA // HEAD_DIM, 0, 0))],
        out_specs=pl.BlockSpec((depth, k, HEAD_DIM), lambda i: (0, 0, 0)),
        out_shape=jax.ShapeDtypeStruct((depth, k, HEAD_DIM), BF16),
        compiler_params=_cparams(("arbitrary",)),
        name="beta_decay_weights",
    )(w_t)


def _matmul_body(h_ref, w_ref, o_ref):
    o_ref[...] = _dot(h_ref[...], w_ref[...].astype(BF16)).astype(o_ref.dtype)


def project(h, w, layer, n, col0=0, tm=1024, tn=768):
    t, k = h.shape
    tn = min(tn, n)
    return pl.pallas_call(
        _matmul_body,
        grid=(n // tn, t // tm),
        in_specs=[pl.BlockSpec((tm, k), lambda j, i: (i, 0)),
                  pl.BlockSpec((None, k, tn), lambda j, i: (layer, 0, col0 // tn + j))],
        out_specs=pl.BlockSpec((tm, tn), lambda j, i: (i, j)),
        out_shape=jax.ShapeDtypeStruct((t, n), F32),
        compiler_params=_cparams(("parallel", "parallel")),
        name="in_proj",
    )(h, w)


def _rope(x, cos, sin):
    return x * cos + pltpu.roll(x, HEAD_DIM // 2, axis=1) * sin


def _matmul_rope_body(h_ref, w_ref, cos_ref, sin_ref, o_ref, *, q_tiles, k_tiles):
    j = pl.program_id(0)
    is_q = (j >= q_tiles[0]) & (j < q_tiles[1])
    is_k = (j >= k_tiles[0]) & (j < k_tiles[1])

    @pl.when(jnp.logical_not(is_q | is_k))
    def _():
        o_ref[...] = _dot(h_ref[...], w_ref[...])

    def rotated(scale):
        for r0 in range(0, o_ref.shape[0], ROPE_ROWS):
            rs = slice(r0, r0 + ROPE_ROWS)
            acc = _dot(h_ref[rs, :], w_ref[...])
            cos, sin = cos_ref[rs, :], sin_ref[rs, :]
            for c in range(o_ref.shape[1] // HEAD_DIM):
                sl = slice(c * HEAD_DIM, (c + 1) * HEAD_DIM)
                o_ref[rs, sl] = _rope(acc[:, sl], cos, sin) * scale

    @pl.when(is_q)
    def _():
        rotated(HEAD_DIM ** -0.5)

    @pl.when(is_k)
    def _():
        rotated(1.0)


def project_mixers(h, w, layer, cos_t, sin_t, tm=1024, tn=WIDTH):
    t, k = h.shape
    q0 = U_C // tn
    nq = 3 * WIDTH // tn
    tab = pl.BlockSpec((tm, HEAD_DIM), lambda j, i: (i, 0))
    return pl.pallas_call(
        functools.partial(_matmul_rope_body, q_tiles=(q0, q0 + nq), k_tiles=(q0 + nq, q0 + 2 * nq)),
        grid=(U_WIDTH // tn, t // tm),
        in_specs=[pl.BlockSpec((tm, k), lambda j, i: (i, 0)),
                  pl.BlockSpec((None, k, tn), lambda j, i: (layer, 0, j)), tab, tab],
        out_specs=pl.BlockSpec((tm, tn), lambda j, i: (i, j)),
        out_shape=jax.ShapeDtypeStruct((t, U_WIDTH), F32),
        compiler_params=_cparams(("parallel", "parallel")),
        name="in_proj",
    )(h, w, cos_t, sin_t)


def _rope_body(pos_ref, inv_ref, cos_ref, sin_ref):
    ang = pos_ref[...].astype(F32) * inv_ref[...]
    lane = lax.broadcasted_iota(jnp.int32, ang.shape, 1)
    cos_ref[...] = jnp.cos(ang)
    s = jnp.sin(ang)
    sin_ref[...] = jnp.where(lane < HEAD_DIM // 2, -s, s)


def rope_tables(pos, tm=1024):
    t = pos.shape[0]
    half = HEAD_DIM // 2
    inv = ROPE_THETA ** (-jnp.arange(half, dtype=F32) * 2.0 / HEAD_DIM)
    inv = jnp.concatenate([inv, inv]).reshape(1, HEAD_DIM)
    spec = pl.BlockSpec((tm, HEAD_DIM), lambda i: (i, 0))
    return pl.pallas_call(
        _rope_body,
        grid=(t // tm,),
        in_specs=[pl.BlockSpec((tm, 1), lambda i: (i, 0)), pl.BlockSpec((1, HEAD_DIM), lambda i: (0, 0))],
        out_specs=[spec, spec],
        out_shape=[jax.ShapeDtypeStruct((t, HEAD_DIM), F32)] * 2,
        compiler_params=_cparams(("parallel",)),
        name="rope_tables",
    )(pos.reshape(t, 1), inv)


def _attn_body(*refs, first, d, hw):
    per = 1 if d == 1 else hw
    n_in = 5 if first else 7
    ops = [refs[i * per:(i + 1) * per] for i in range(n_in)]
    o_out, lse_out = refs[n_in * per], refs[n_in * per + 1]
    o_scr, lse_scr = (None, None) if d == 1 else refs[n_in * per + 2:]

    def head(i, h):
        return (ops[i][0], slice(h * HEAD_DIM, (h + 1) * HEAD_DIM)) if d == 1 else (ops[i][h], slice(None))

    def load(i, h, rows):
        ref, sl = head(i, h)
        return ref[rows, sl]

    n = pl.program_id(0)
    blk = ATT_BLOCK
    row = lax.broadcasted_iota(jnp.int32, (blk, blk), 0)
    col = lax.broadcasted_iota(jnp.int32, (blk, blk), 1)
    in_cur = col <= row
    has_prev = n > 0
    neg = -jnp.inf
    units = [(pl.ds(r, blk, stride=d) if d > 1 else pl.ds(0, blk), h) for h in range(hw) for r in range(d)]
    for g0 in range(0, len(units), ATT_UNITS_IN_FLIGHT):
        group = units[g0:g0 + ATT_UNITS_IN_FLIGHT]
        scores = []
        for rows, h in group:
            q = load(0, h, rows)
            kp = load(2, h, rows)
            qb = q.astype(BF16)
            s_c = _dot_nt(qb, load(1, h, rows).astype(BF16))
            s_p = _dot_nt(qb, kp.astype(BF16))
            s_d = jnp.where(has_prev, jnp.sum(q * kp, axis=-1, keepdims=True), neg)
            scores.append((jnp.where(in_cur, s_c, jnp.where(has_prev, s_p, neg)), s_d))
        probs = []
        for (rows, h), (s_w, s_d) in zip(group, scores):
            m = jnp.maximum(jnp.max(s_w, axis=-1, keepdims=True), s_d)
            m_old = None
            if not first:
                m_old = load(6, h, rows)[:, 0:1]
                m = jnp.maximum(m, m_old)
            p_w = jnp.exp(s_w - m)
            p_d = jnp.exp(s_d - m)
            probs.append((m, m_old, p_w, p_d, jnp.sum(p_w, axis=-1, keepdims=True) + p_d))
        for (rows, h), (m, m_old, p_w, p_d, l) in zip(group, probs):
            vp = load(4, h, rows)
            acc = (_dot(jnp.where(in_cur, p_w, 0.0).astype(BF16), load(3, h, rows).astype(BF16))
                   + _dot(jnp.where(in_cur, 0.0, p_w).astype(BF16), vp.astype(BF16)) + p_d * vp)
            if not first:
                w_old = jnp.exp(m_old - m)
                l = l + w_old
                acc = acc + w_old * load(5, h, rows)
            lse = jnp.broadcast_to(m + jnp.log(l), (blk, HEAD_DIM))
            if d == 1:
                sl = slice(h * HEAD_DIM, (h + 1) * HEAD_DIM)
                o_out[rows, sl] = acc / l
                lse_out[rows, sl] = lse
            else:
                o_scr.at[h][rows, :] = acc / l
                lse_scr.at[h][rows, :] = lse
    if d > 1:
        for h in range(hw):
            sl = slice(h * HEAD_DIM, (h + 1) * HEAD_DIM)
            o_out[:, sl] = o_scr[h]
            lse_out[:, sl] = lse_scr[h]


def attention_group(u, group, dilation, o_prev, lse_prev):
    t = u.shape[0]
    d = dilation
    span = ATT_BLOCK * d
    hw = ATT_HEADS_PER_STEP[d]
    per = 1 if d == 1 else hw
    cw = hw * HEAD_DIM // per
    first = o_prev is None
    offs = [(U_C + part * 3 * WIDTH + group * WIDTH) // cw for part in range(3)]

    def block_index(n, g, *, off, back, p):
        return jnp.maximum(n - back, 0), off + g * per + p

    def spec(off, back):
        return [pl.BlockSpec((span, cw), functools.partial(block_index, off=off, back=back, p=p))
                for p in range(per)]

    in_specs = spec(offs[0], 0) + spec(offs[1], 0) + spec(offs[1], 1) + spec(offs[2], 0) + spec(offs[2], 1)
    args = [u] * (5 * per)
    if not first:
        in_specs += spec(0, 0) + spec(0, 0)
        args += [o_prev] * per + [lse_prev] * per
    o_spec = pl.BlockSpec((span, hw * HEAD_DIM), lambda n, g: (n, g))
    scratch = [] if d == 1 else [pltpu.VMEM((hw, span, HEAD_DIM), F32)] * 2
    return pl.pallas_call(
        functools.partial(_attn_body, first=first, d=d, hw=hw),
        grid=(t // span, N_HEADS // hw),
        in_specs=in_specs,
        out_specs=[o_spec, o_spec],
        out_shape=[jax.ShapeDtypeStruct((t, WIDTH), F32), jax.ShapeDtypeStruct((t, WIDTH), F32)],
        scratch_shapes=scratch,
        compiler_params=_cparams(("parallel", "parallel")),
        name=f"attn_d{d}",
    )(*args)


def _local_body(ub_ref, ud_ref, cw_ref, cb_ref, lg_ref, lb_ref, pw_ref, ps_ref, yb_ref, yd_ref, extb, extd, pool_buf, *, tm):
    i = pl.program_id(0)
    hb, hd = 32, 16

    @pl.when(i == 0)
    def _():
        extb[0:hb, :] = jnp.zeros((hb, WIDTH), F32)
        extd[0:hd, :] = jnp.zeros((hd, WIDTH), F32)

    @pl.when(i > 0)
    def _():
        extb[0:hb, :] = extb[tm:tm + hb, :]
        extd[0:hd, :] = extd[tm:tm + hd, :]

    ub = ub_ref[...]
    extb[hb:hb + tm, :] = ub[:, :WIDTH] * _sigmoid(ub[:, WIDTH:])
    extd[hd:hd + tm, :] = ud_ref[...]

    rc = LOCAL_ROWS
    ch = lax.broadcasted_iota(jnp.int32, (rc, WIDTH), 1)
    rows = lax.broadcasted_iota(jnp.int32, (rc, WIDTH), 0)
    for c0 in range(0, tm, rc):
        acc = None
        for s in range(8):
            win = rc + (8 if s else 0)
            part = None
            for j in range(CONF_CONV):
                off = hb - (CONF_CONV - 1) + j
                if off % 8 != s:
                    continue
                base = c0 + off - s
                term = cw_ref[j:j + 1, :] * extb[base:base + win, :]
                part = term if part is None else part + term
            if part is None:
                continue
            part = part[s:s + rc, :]
            acc = part if acc is None else acc + part
        cb = acc + cb_ref[...]
        mu = jnp.mean(cb, axis=-1, keepdims=True)
        cen = cb - mu
        var = jnp.mean(cen * cen, axis=-1, keepdims=True)
        yb = _silu(cen * lax.rsqrt(var + LN_EPS) * lg_ref[...] + lb_ref[...])
        yb_ref[c0:c0 + rc, :] = yb.astype(yb_ref.dtype)

        xd = extd[c0 + hd:c0 + hd + rc, :]
        tpos = i * tm + c0 + rows + 1
        run = xd
        pooled = None
        shift = 1
        for gi, w in enumerate(POOL_WINDOWS):
            while shift < w:
                run = run + extd[c0 + hd - shift:c0 + hd - shift + rc, :]
                shift += 1
            val = run / jnp.minimum(tpos, w).astype(F32) - xd
            pooled = val if pooled is None else jnp.where(ch >= gi * POOL_GROUP, val, pooled)
        pool_buf[c0:c0 + rc, :] = pooled.astype(pool_buf.dtype)
    yd = _dot(pool_buf[...], pw_ref[...]) * ps_ref[...]
    yd_ref[...] = yd.astype(yd_ref.dtype)


def local_mixers(u, conv_w, conv_b, ln_g, ln_b, pool_w_bd, pool_scale, tm=256):
    t = u.shape[0]
    row = lambda a: a.reshape(1, WIDTH)
    vec = pl.BlockSpec((1, WIDTH), lambda i: (0, 0))
    out = pl.BlockSpec((tm, WIDTH), lambda i: (i, 0))
    return pl.pallas_call(
        functools.partial(_local_body, tm=tm),
        grid=(t // tm,),
        in_specs=[pl.BlockSpec((tm, 2 * WIDTH), lambda i: (i, U_B // (2 * WIDTH))),
                  pl.BlockSpec((tm, WIDTH), lambda i: (i, U_D // WIDTH)),
                  pl.BlockSpec((CONF_CONV, WIDTH), lambda i: (0, 0)), vec, vec, vec,
                  pl.BlockSpec((WIDTH, WIDTH), lambda i: (0, 0)), vec],
        out_specs=[out, out],
        out_shape=[jax.ShapeDtypeStruct((t, WIDTH), BF16)] * 2,
        scratch_shapes=[pltpu.VMEM((tm + 32, WIDTH), F32), pltpu.VMEM((tm + 16, WIDTH), F32),
                        pltpu.VMEM((tm, WIDTH), BF16)],
        compiler_params=_cparams(("arbitrary",)),
        name="conv_pool",
    )(u, u, conv_w, row(conv_b), row(ln_g), row(ln_b), pool_w_bd, row(pool_scale))


def _gdn_body(qkv_ref, z_ref, bd_ref, cw_ref, alog_ref, dtb_ref, gn_ref, ya_ref, ext, state):
    s = pl.program_id(0)
    sup, c = GDN_SUPER, GDN_CHUNK
    halo = 8

    @pl.when(s == 0)
    def _():
        ext[0:halo, :] = jnp.zeros((halo, 3 * WIDTH), F32)
        state[...] = jnp.zeros(state.shape, F32)

    @pl.when(s > 0)
    def _():
        ext[0:halo, :] = ext[sup:sup + halo, :]

    ext[halo:halo + sup, :] = qkv_ref[...]
    acc = jnp.zeros((sup, 3 * WIDTH), F32)
    for j in range(GDN_CONV):
        off = halo - (GDN_CONV - 1) + j
        acc = acc + cw_ref[j:j + 1, :] * ext[off:off + sup, :]
    qkv = _silu(acc)

    bd = bd_ref[...]
    beta_all = _sigmoid(bd)
    xs = bd + dtb_ref[...]
    softplus = jnp.maximum(xs, 0.0) + jnp.log(1.0 + jnp.exp(-jnp.abs(xs)))
    g_all = -jnp.exp(alog_ref[...]) * softplus
    rowi = lax.broadcasted_iota(jnp.int32, (sup, HEAD_DIM), 0)
    in_chunk = rowi & (c - 1)
    gc = g_all
    sh = 1
    while sh < c:
        gc = gc + jnp.where(in_chunk >= sh, pltpu.roll(gc, sh, axis=0), 0.0)
        sh *= 2
    gct = gc.T

    ri = lax.broadcasted_iota(jnp.int32, (sup, sup), 0)
    ci = lax.broadcasted_iota(jnp.int32, (sup, sup), 1)
    same = (ri & -c) == (ci & -c)
    incl = same & (ri >= ci)
    strict = same & (ri > ci)
    eye = (ri == ci).astype(F32)

    heads = range(N_HEADS)
    hs = [slice(h * HEAD_DIM, (h + 1) * HEAD_DIM) for h in heads]
    prep = []
    for h in heads:
        q = qkv[:, hs[h]]
        k = qkv[:, WIDTH + h * HEAD_DIM:WIDTH + (h + 1) * HEAD_DIM]
        v = qkv[:, 2 * WIDTH + h * HEAD_DIM:2 * WIDTH + (h + 1) * HEAD_DIM]
        q = q * lax.rsqrt(jnp.sum(q * q, axis=-1, keepdims=True) + 1e-6) * (HEAD_DIM ** -0.5)
        k = k * lax.rsqrt(jnp.sum(k * k, axis=-1, keepdims=True) + 1e-6)
        b_col = beta_all[:, h:h + 1]
        gc_col = gc[:, N_HEADS + h:N_HEADS + h + 1]
        gc_row = gct[N_HEADS + h:N_HEADS + h + 1, :]
        dm = jnp.exp(jnp.where(incl, gc_col - gc_row, -jnp.inf))
        kb = k.astype(BF16)
        nmat = jnp.where(strict, -(b_col * _dot_nt(kb, kb) * dm), 0.0)
        prep.append((q, k, v, b_col, gc_col, dm, kb, nmat))

    pmats = [eye + p[7] for p in prep]
    mbs = [p[7].astype(BF16) for p in prep]
    step = 2
    while step < c:
        m2s = [_dot(mb, mb).astype(BF16) for mb in mbs]
        pmats = [pm + _dot(pm.astype(BF16), m2) for pm, m2 in zip(pmats, m2s)]
        mbs = m2s
        step *= 2

    chunked = []
    for h in heads:
        q, k, v, b_col, gc_col, dm, kb, _ = prep[h]
        eg = jnp.exp(gc_col)
        rhs = jnp.concatenate([v * b_col, k * (b_col * eg)], axis=1).astype(BF16)
        sol = _dot(pmats[h].astype(BF16), rhs)
        qk_b = (_dot_nt(q.astype(BF16), kb) * dm).astype(BF16)
        q_dec = (q * eg).astype(BF16)
        gl_col = jnp.concatenate(
            [jnp.broadcast_to(gc_col[(n + 1) * c - 1:(n + 1) * c, :], (c, 1)) for n in range(sup // c)], axis=0)
        k_dec = k * jnp.exp(gl_col - gc_col)
        chunked.append((sol[:, :HEAD_DIM], sol[:, HEAD_DIM:].astype(BF16), qk_b, q_dec, k_dec, gc_col))

    states = [state[h] for h in heads]
    outs = [[] for _ in heads]
    for n in range(sup // c):
        rs = slice(n * c, (n + 1) * c)
        for h in heads:
            u_c, w_b, qk_b, q_dec, k_dec, gc_col = chunked[h]
            sb = states[h].astype(BF16)
            v_new = u_c[rs] - _dot(w_b[rs], sb)
            vb = v_new.astype(BF16)
            outs[h].append(_dot(q_dec[rs], sb) + _dot(qk_b[rs, rs], vb))
            g_last = jnp.exp(gc_col[(n + 1) * c - 1:(n + 1) * c, :])
            states[h] = states[h] * g_last + _dot(k_dec[rs].T.astype(BF16), vb)

    for h in heads:
        state[h] = states[h]
        o = jnp.concatenate(outs[h], axis=0)
        o = o * lax.rsqrt(jnp.mean(o * o, axis=-1, keepdims=True) + NORM_EPS) * gn_ref[...]
        ya_ref[:, hs[h]] = (o * _silu(z_ref[:, hs[h]])).astype(ya_ref.dtype)


def gated_deltanet(u, bd, conv_w, a_log, dt_bias, norm_g):
    t = u.shape[0]
    sup = GDN_SUPER
    pad = jnp.zeros((N_HEADS,), F32)
    lane_row = lambda a: jnp.concatenate([pad, a, jnp.zeros((HEAD_DIM - 2 * N_HEADS,), F32)]).reshape(1, HEAD_DIM)
    vec = pl.BlockSpec((1, HEAD_DIM), lambda s: (0, 0))
    return pl.pallas_call(
        _gdn_body,
        grid=(t // sup,),
        in_specs=[pl.BlockSpec((sup, 3 * WIDTH), lambda s: (s, U_QKV_A // (3 * WIDTH))),
                  pl.BlockSpec((sup, WIDTH), lambda s: (s, U_Z_A // WIDTH)),
                  pl.BlockSpec((sup, HEAD_DIM), lambda s: (s, 0)),
                  pl.BlockSpec((GDN_CONV, 3 * WIDTH), lambda s: (0, 0)), vec, vec, vec],
        out_specs=pl.BlockSpec((sup, WIDTH), lambda s: (s, 0)),
        out_shape=jax.ShapeDtypeStruct((t, WIDTH), BF16),
        scratch_shapes=[pltpu.VMEM((sup + 8, 3 * WIDTH), F32), pltpu.VMEM((N_HEADS, HEAD_DIM, HEAD_DIM), F32)],
        compiler_params=_cparams(("arbitrary",)),
        name="gated_deltanet",
    )(u, u, bd, conv_w, lane_row(a_log), lane_row(dt_bias), norm_g.reshape(1, HEAD_DIM))


def _merge_body(h_ref, ya_ref, yb_ref, yc_ref, yd_ref, g0_ref, g1_ref, g2_ref, g3_ref,
                b0_ref, b1_ref, b2_ref, b3_ref, wo_ref, x_ref, n2_ref, wr_ref, br_ref,
                x1_ref, rl_ref, acc):
    j = pl.program_id(1)

    @pl.when(j == 0)
    def _():
        acc[...] = jnp.zeros(acc.shape, F32)

    hb = h_ref[...]
    merged = None
    for y_ref, g_ref, b_ref in ((ya_ref, g0_ref, b0_ref), (yb_ref, g1_ref, b1_ref),
                                (yc_ref, g2_ref, b2_ref), (yd_ref, g3_ref, b3_ref)):
        gate = _sigmoid(_dot(hb, g_ref[...]))
        term = gate * _dot(y_ref[...].astype(BF16), b_ref[...])
        merged = term if merged is None else merged + term
    acc[...] += _dot(merged.astype(BF16), wo_ref[...])

    @pl.when(j == pl.num_programs(1) - 1)
    def _():
        x1 = x_ref[...] + acc[...]
        x1_ref[...] = x1
        h2 =x1 * lax.rsqrt(jnp.mean(x1 * x1, axis=-1, keepdims=True) + NORM_EPS) * n2_ref[...]
        rl_ref[...] = jnp.dot(h2, wr_ref[...], preferred_element_type=F32,
                              precision=lax.Precision.HIGHEST) + br_ref[...]


def merge_project(h, ya, yb, yc, yd, w_all, w_branch, w_out, layer, x, norm2_g, w_router, b_router, tm=512, tn=256):
    t = h.shape[0]
    d = D_MODEL
    nj = d // tn
    g0 = U_WIDTH // tn
    row = pl.BlockSpec((tm, d), lambda i, j: (i, 0))
    ysp = pl.BlockSpec((tm, WIDTH), lambda i, j: (i, 0))
    gate_spec = lambda k: pl.BlockSpec((None, d, tn), lambda i, j: (layer, 0, g0 + k * nj + j))
    br_spec = lambda k: pl.BlockSpec((None, None, WIDTH, tn), lambda i, j: (layer, k, 0, j))
    lanes = pl.BlockSpec((tm, HEAD_DIM), lambda i, j: (i, 0))
    w_gate = w_all
    return pl.pallas_call(
        _merge_body,
        grid=(t // tm, nj),
        in_specs=[row, ysp, ysp, ysp, ysp] + [gate_spec(k) for k in range(4)] + [br_spec(k) for k in range(4)]
        + [pl.BlockSpec((None, tn, d), lambda i, j: (layer, j, 0)), row, pl.BlockSpec((1, d), lambda i, j: (0, 0)),
           pl.BlockSpec((d, HEAD_DIM), lambda i, j: (0, 0)), pl.BlockSpec((1, HEAD_DIM), lambda i, j: (0, 0))],
        out_specs=[row, lanes],
        out_shape=[jax.ShapeDtypeStruct((t, d), F32), jax.ShapeDtypeStruct((t, HEAD_DIM), F32)],
        scratch_shapes=[pltpu.VMEM((tm, d), F32)],
        compiler_params=_cparams(("parallel", "arbitrary")),
        name="merge_project",
    )(h, ya, yb, yc, yd, w_gate, w_gate, w_gate, w_gate, w_branch, w_branch, w_branch, w_branch,
      w_out, x, norm2_g.reshape(1, d), w_router, b_router)


def _router_body(rl_ref, eid_ref, wts_ref):
    rl = rl_ref[...]
    lane_i = lax.broadcasted_iota(jnp.int32, rl.shape, 1)
    lane = lane_i.astype(F32)
    neg = -jnp.inf
    big = 1e9
    gl = jnp.where(lane < N_GROUPS, rl, neg)
    gmax = jnp.max(gl, axis=-1, keepdims=True)
    gi = jnp.min(jnp.where(gl == gmax, lane, big), axis=-1, keepdims=True)
    gp = 1.0 / jnp.sum(jnp.exp(gl - gmax), axis=-1, keepdims=True)
    lo = N_GROUPS + gi * PER_GROUP
    el = jnp.where((lane >= lo) & (lane < lo + PER_GROUP), rl, neg)
    e1 = jnp.max(el, axis=-1, keepdims=True)
    i1 = jnp.min(jnp.where(el == e1, lane, big), axis=-1, keepdims=True)
    el2 = jnp.where(lane == i1, neg, el)
    e2 = jnp.max(el2, axis=-1, keepdims=True)
    i2 = jnp.min(jnp.where(el2 == e2, lane, big), axis=-1, keepdims=True)
    esum = jnp.sum(jnp.exp(el - e1), axis=-1, keepdims=True)
    p1 = 1.0 / esum
    p2 = jnp.exp(e2 - e1) / esum
    w1 = gp * p1 / (p1 + p2)
    w2 = gp * p2 / (p1 + p2)
    eid = jnp.where(lane_i == 0, i1 - N_GROUPS, jnp.where(lane_i == 1, i2 - N_GROUPS, 0.0))
    eid_ref[...] = eid.astype(jnp.int32)
    wts_ref[...] = jnp.where(lane_i == 0, w1, jnp.where(lane_i == 1, w2, 0.0))


def route(rl, tm=512):
    t = rl.shape[0]
    spec = pl.BlockSpec((tm, HEAD_DIM), lambda i: (i, 0))
    return pl.pallas_call(
        _router_body,
        grid=(t // tm,),
        in_specs=[spec],
        out_specs=[spec, spec],
        out_shape=[jax.ShapeDtypeStruct((t, HEAD_DIM), jnp.int32), jax.ShapeDtypeStruct((t, HEAD_DIM), F32)],
        compiler_params=_cparams(("parallel",)),
        name="router",
    )(rl)


def _row_copy(src, src_row, dst, dst_row, sem):
    return pltpu.make_async_copy(src.at[pl.ds(src_row, 1), :], dst.at[pl.ds(dst_row, 1), :], sem)


def _zero_copy(zbuf, zrow_ref, xs_hbm, zsem, e):
    start = pl.multiple_of(jnp.maximum(zrow_ref[e], 0), MOE_BLOCK)
    return pltpu.make_async_copy(zbuf, xs_hbm.at[pl.ds(start, MOE_BLOCK), :], zsem)


def _zero_block(zbuf, xs_hbm, zsem, blk):
    start = pl.multiple_of(blk * MOE_BLOCK, MOE_BLOCK)
    return pltpu.make_async_copy(zbuf, xs_hbm.at[pl.ds(start, MOE_BLOCK), :], zsem)


def _sort_rows_body(dest_ref, zrow_ref, nused_ref, x_ref, xs_hbm, zbuf, sem, zsem, *, tm):
    i = pl.program_id(0)
    sem, zsem = sem.at[0], zsem.at[0]

    @pl.when(i == 0)
    def _():
        zbuf[...] = jnp.zeros(zbuf.shape, F32)
        n_blocks = xs_hbm.shape[0] // MOE_BLOCK
        for e in range(N_EXPERTS):
            @pl.when(zrow_ref[e] >= 0)
            def _():
                _zero_copy(zbuf, zrow_ref, xs_hbm, zsem, e).start()

        def z_start(blk, c):
            _zero_block(zbuf, xs_hbm, zsem, blk).start()
            return c

        def z_wait(blk, c):
            _zero_block(zbuf, xs_hbm, zsem, blk).wait()
            return c

        lax.fori_loop(nused_ref[0], n_blocks, z_start, 0)
        for e in range(N_EXPERTS):
            @pl.when(zrow_ref[e] >= 0)
            def _():
                _zero_copy(zbuf, zrow_ref, xs_hbm, zsem, e).wait()
        lax.fori_loop(nused_ref[0], n_blocks, z_wait, 0)

    for r in range(tm):
        for k in range(TOP_K):
            _row_copy(x_ref, r, xs_hbm, dest_ref[(i * tm + r) * TOP_K + k], sem).start()
    for _ in range(tm * TOP_K):
        _row_copy(x_ref, 0, xs_hbm, 0, sem).wait()


def sort_rows(x1, dest, zrow, n_used, n_slots, tm=256):
    t, d = x1.shape
    grid_spec = pltpu.PrefetchScalarGridSpec(
        num_scalar_prefetch=3,
        grid=(t // tm,),
        in_specs=[pl.BlockSpec((tm, d), lambda i, dest, zrow, nu: (i, 0))],
        out_specs=pl.BlockSpec(memory_space=pl.ANY),
        scratch_shapes=[pltpu.VMEM((MOE_BLOCK, d), F32),
                        pltpu.SemaphoreType.DMA((1,)), pltpu.SemaphoreType.DMA((1,))],
    )
    return pl.pallas_call(
        functools.partial(_sort_rows_body, tm=tm),
        grid_spec=grid_spec,
        out_shape=jax.ShapeDtypeStruct((n_slots, d), F32),
        compiler_params=_cparams(("arbitrary",)),
        name="moe_sort_rows",
    )(dest, zrow, n_used, x1)


def _weight_copies(wup_hbm, wdn_hbm, wup_f, wdn_f, wsem, layer, e, slot):
    return (pltpu.make_async_copy(wup_hbm.at[layer, e], wup_f.at[slot], wsem.at[0, slot]),
            pltpu.make_async_copy(wdn_hbm.at[layer, e], wdn_f.at[slot], wsem.at[1, slot]))


def _expert_body(be_ref, nused_ref, nxt1_ref, nxt2_ref, par_ref, xs_ref, g_ref, wup_hbm, wdn_hbm, ys_ref,
                 wup_f, wdn_f, wup_b, wdn_b, wsem, *, layer):
    b = pl.program_id(0)
    copies = functools.partial(_weight_copies, wup_hbm, wdn_hbm, wup_f, wdn_f, wsem, layer)

    def start_if_any(e, slot):
        @pl.when(e >= 0)
        def _():
            for cp in copies(e, slot):
                cp.start()

    @pl.when(b == 0)
    def _():
        start_if_any(be_ref[0], par_ref[0])
        start_if_any(nxt1_ref[0], (par_ref[0] + 1) % EXPERT_WEIGHT_SLOTS)

    @pl.when(b >= nused_ref[0])
    def _():
        ys_ref[...] = jnp.zeros(ys_ref.shape, F32)

    @pl.when(b < nused_ref[0])
    def _():
        changed = jnp.logical_or(b == 0, be_ref[b] != be_ref[jnp.maximum(b - 1, 0)])

        @pl.when(changed)
        def _():
            slot = par_ref[b]
            for cp in copies(be_ref[b], slot):
                cp.wait()
            start_if_any(nxt2_ref[b], (slot + 2) % EXPERT_WEIGHT_SLOTS)

            wup_b[...] = wup_f[slot].astype(BF16)
            wdn_b[...] = wdn_f[slot].astype(BF16)

        xr = xs_ref[...]
        xn = xr * lax.rsqrt(jnp.mean(xr * xr, axis=-1, keepdims=True) + NORM_EPS) * g_ref[...]
        gu = _dot(xn.astype(BF16), wup_b[...])
        act = (_silu(gu[:, :EXPERT_HIDDEN]) * gu[:, EXPERT_HIDDEN:]).astype(BF16)
        ys_ref[...] = _dot(act, wdn_b[...])


def experts(xs, norm_g, w_up, w_down, layer, block_e, n_used, next1, next2, slot):
    n_blocks = block_e.shape[0]
    d = D_MODEL
    ns = EXPERT_WEIGHT_SLOTS
    rows = lambda b, be, nu, n1, n2, sl: (jnp.minimum(b, nu[0] - 1), 0)
    grid_spec = pltpu.PrefetchScalarGridSpec(
        num_scalar_prefetch=5,
        grid=(n_blocks,),
        in_specs=[pl.BlockSpec((MOE_BLOCK, d), rows),
                  pl.BlockSpec((1, d), lambda b, be, nu, n1, n2, sl: (0, 0)),
                  pl.BlockSpec(memory_space=pl.ANY), pl.BlockSpec(memory_space=pl.ANY)],
        out_specs=pl.BlockSpec((MOE_BLOCK, d), lambda b, be, nu, n1, n2, sl: (b, 0)),
        scratch_shapes=[pltpu.VMEM((ns, d, 2 * EXPERT_HIDDEN), F32), pltpu.VMEM((ns, EXPERT_HIDDEN, d), F32),
                        pltpu.VMEM((d, 2 * EXPERT_HIDDEN), BF16), pltpu.VMEM((EXPERT_HIDDEN, d), BF16),
                        pltpu.SemaphoreType.DMA((2, ns))],
    )
    return pl.pallas_call(
        functools.partial(_expert_body, layer=layer),
        grid_spec=grid_spec,
        out_shape=jax.ShapeDtypeStruct(xs.shape, F32),
        compiler_params=_cparams(("arbitrary",)),
        name="experts",
    )(block_e, n_used, next1, next2, slot, xs, norm_g.reshape(1, d), w_up, w_down)


def dispatch_tables(eid, t):
    n_assign = t * TOP_K
    flat_e = eid.reshape(n_assign)
    onehot = (jnp.arange(N_EXPERTS, dtype=jnp.int32)[:, None] == flat_e[None, :]).astype(F32)
    grp = onehot.reshape(N_EXPERTS, n_assign // MOE_BLOCK, MOE_BLOCK)
    tril = jnp.tril(jnp.ones((MOE_BLOCK, MOE_BLOCK), F32))
    inside = jnp.einsum("egj,ij->egi", grp, tril)
    totals = inside[:, :, -1]
    csum = (inside + (jnp.cumsum(totals, axis=1) - totals)[:, :, None]).reshape(N_EXPERTS, n_assign)
    counts = jnp.sum(totals, axis=1).astype(jnp.int32)
    rank = jnp.sum(csum * onehot, axis=0).astype(jnp.int32) - 1
    padded = (counts + MOE_BLOCK - 1) // MOE_BLOCK * MOE_BLOCK
    pad_end = jnp.cumsum(padded)
    pad_start = pad_end - padded
    dest = pad_start[flat_e] + rank
    n_blocks = -(-(n_assign + N_EXPERTS * (MOE_BLOCK - 1)) // MOE_BLOCK)
    n_slots = n_blocks * MOE_BLOCK
    zrow = jnp.where(padded > 0, pad_end - MOE_BLOCK, -1).astype(jnp.int32)
    n_used = pad_end[-1] // MOE_BLOCK
    starts = jnp.arange(n_blocks, dtype=jnp.int32) * MOE_BLOCK
    block_e = jnp.sum((pad_end[None, :] <= starts[:, None]).astype(jnp.int32), axis=1)
    block_e = jnp.minimum(block_e, N_EXPERTS - 1)
    last_e = block_e[jnp.maximum(n_used - 1, 0)]
    block_e = jnp.where(jnp.arange(n_blocks) < n_used, block_e, last_e)
    ids = jnp.arange(N_EXPERTS, dtype=jnp.int32)
    later = (ids[None, :] > ids[:, None]) & (counts[None, :] > 0)
    next_nonempty = jnp.min(jnp.where(later, ids[None, :], N_EXPERTS), axis=1)
    next_nonempty = jnp.where(next_nonempty == N_EXPERTS, -1, next_nonempty).astype(jnp.int32)
    rank_nonempty = jnp.cumsum((counts > 0).astype(jnp.int32)) - 1
    next1 = next_nonempty[block_e]
    next2 = jnp.where(next1 >= 0, next_nonempty[jnp.maximum(next1, 0)], -1)
    slot = (rank_nonempty[block_e] % EXPERT_WEIGHT_SLOTS).astype(jnp.int32)
    return (block_e, n_used.reshape(1).astype(jnp.int32), next1, next2, slot, dest.astype(jnp.int32), zrow,
            n_slots)


def _combine_rows(dest_ref, ys_hbm, ybuf, sem, tile, slot, tm, wait):
    for r in range(tm):
        for k in range(TOP_K):
            if wait:
                _row_copy(ys_hbm, 0, ybuf.at[slot, k], 0, sem.at[slot]).wait()
            else:
                _row_copy(ys_hbm, dest_ref[(tile * tm + r) * TOP_K + k], ybuf.at[slot, k], r, sem.at[slot]).start()


def _combine_body(dest_ref, x_ref, w_ref, g_ref, ys_hbm, x2_ref, hn_ref, ybuf, sem, *, tm):
    i = pl.program_id(0)
    slot = i % 2

    @pl.when(i == 0)
    def _():
        _combine_rows(dest_ref, ys_hbm, ybuf, sem, i, slot, tm, wait=False)

    _combine_rows(dest_ref, ys_hbm, ybuf, sem, i, slot, tm, wait=True)

    @pl.when(i + 1 < pl.num_programs(0))
    def _():
        _combine_rows(dest_ref, ys_hbm, ybuf, sem, i + 1, 1 - slot, tm, wait=False)

    w = w_ref[...]
    x2 = x_ref[...] + (w[:, 0:1] * ybuf[slot, 0] + w[:, 1:2] * ybuf[slot, 1])
    x2_ref[...] = x2
    hn = x2 * lax.rsqrt(jnp.mean(x2 * x2, axis=-1, keepdims=True) + NORM_EPS) * g_ref[...]
    hn_ref[...] = hn.astype(hn_ref.dtype)


def combine(x1, ys, dest, wts, next_g, next_dtype, tm=128):
    t, d = x1.shape
    row = pl.BlockSpec((tm, d), lambda i, dest: (i, 0))
    grid_spec = pltpu.PrefetchScalarGridSpec(
        num_scalar_prefetch=1,
        grid=(t // tm,),
        in_specs=[row, pl.BlockSpec((tm, HEAD_DIM), lambda i, dest: (i, 0)),
                  pl.BlockSpec((1, d), lambda i, dest: (0, 0)),
                  pl.BlockSpec(memory_space=pl.ANY)],
        out_specs=[row, row],
        scratch_shapes=[pltpu.VMEM((2, TOP_K, tm, d), F32), pltpu.SemaphoreType.DMA((2,))],
    )
    return pl.pallas_call(
        functools.partial(_combine_body, tm=tm),
        grid_spec=grid_spec,
        out_shape=[jax.ShapeDtypeStruct((t, d), F32), jax.ShapeDtypeStruct((t, d), next_dtype)],
        compiler_params=_cparams(("arbitrary",)),
        name="moe_combine",
    )(dest, x1, wts, next_g.reshape(1, d), ys)


def _layer(x, h, cos_t, sin_t, layer, w_bd, w_all, conv_a_w, a_log, dt_bias, gdn_norm_g, conv_b_w, conv_b_b,
           ln_b_g, ln_b_b, pool_w, pool_scale, w_branch, w_out, norm2_g, wg, bg, we, be, w_up, w_down,
           next_g, next_dtype):
    t = x.shape[0]
    pool_bd = jnp.zeros((WIDTH, WIDTH), F32)
    for gi in range(len(POOL_WINDOWS)):
        pool_bd = lax.dynamic_update_slice(pool_bd, pool_w[gi], (gi * POOL_GROUP, gi * POOL_GROUP))
    w_router = jnp.pad(jnp.concatenate([wg, we], axis=1), ((0, 0), (0, HEAD_DIM - N_GROUPS - N_EXPERTS)))
    b_router = jnp.pad(jnp.concatenate([bg, be]), (0, HEAD_DIM - N_GROUPS - N_EXPERTS)).reshape(1, HEAD_DIM)

    u = project_mixers(h, w_all, layer, cos_t, sin_t)
    bd = project(h, w_bd, layer, HEAD_DIM)
    ya = gated_deltanet(u, bd, conv_a_w, a_log, dt_bias, gdn_norm_g)
    yb, yd = local_mixers(u, conv_b_w, conv_b_b, ln_b_g, ln_b_b, pool_bd.astype(BF16), pool_scale)
    yc = lse = None
    for gi, (_, dilation) in enumerate(ATT_PATTERNS):
        yc, lse = attention_group(u, gi, dilation, yc, lse)
    x1, rl = merge_project(h, ya, yb, yc, yd, w_all, w_branch, w_out, layer, x, norm2_g, w_router, b_router)
    eid, wts = route(rl)
    block_e, n_used, next1, next2, slot, dest, zrow, n_slots = dispatch_tables(eid[:, :TOP_K], t)
    xs = sort_rows(x1, dest, zrow, n_used, n_slots)
    ys = experts(xs, norm2_g, w_up, w_down, layer, block_e, n_used, next1, next2, slot)
    return combine(x1, ys, dest, wts, next_g, next_dtype)


def kernel(x, positions, norm1_g, w_in, conv_a_w, a_log, dt_bias, gdn_norm_g, conv_b_w, conv_b_b, ln_b_g, ln_b_b,
           pool_w, pool_scale, w_branch, w_out, norm2_g, router_group_w, router_group_b, router_expert_w,
           router_expert_b, w_up, w_down, final_norm_g):
    b_, s_, d = x.shape
    depth = w_in.shape[0]
    outs = []
    w_all = prepare_w_in(w_in)
    w_bd = beta_decay_weights(w_in)
    w_branch_b = w_branch.astype(BF16)
    w_out_b = w_out.astype(BF16)
    for bi in range(b_):
        xb = x[bi]
        cos_t, sin_t = rope_tables(positions[bi])
        h = rmsnorm(xb, norm1_g[0], BF16)
        for layer in range(depth):
            last = layer == depth - 1
            next_g = final_norm_g if last else norm1_g[layer + 1]
            xb, h = _layer(xb, h, cos_t, sin_t, layer, w_bd, w_all, conv_a_w[layer], a_log[layer], dt_bias[layer],
                           gdn_norm_g[layer], conv_b_w[layer], conv_b_b[layer], ln_b_g[layer], ln_b_b[layer],
                           pool_w[layer], pool_scale[layer], w_branch_b, w_out_b, norm2_g[layer],
                           router_group_w[layer], router_group_b[layer], router_expert_w[layer],
                           router_expert_b[layer], w_up, w_down, next_g, F32 if last else BF16)
        outs.append(h)
    return jnp.stack(outs, axis=0)
```

```python
import functools

import jax
import jax.numpy as jnp
from jax import lax
from jax.experimental import pallas as pl
from jax.experimental.pallas import tpu as pltpu

F32 = jnp.float32
BF16 = jnp.bfloat16

D_MODEL = 2048
N_HEADS = 6
HEAD_DIM = 128
WIDTH = N_HEADS * HEAD_DIM
GDN_CONV = 4
GDN_CHUNK = 64
GDN_SUPER = 256
CONF_CONV = 31
ATT_PATTERNS = ((128, 1), (512, 4), (2048, 16))
ATT_BLOCK = 128
ATT_UNITS_IN_FLIGHT = 6
ATT_HEADS_PER_STEP = {1: 6, 4: 3, 16: 2}
ROPE_THETA = 10000.0
ROPE_ROWS = 256
POOL_WINDOWS = (2, 4, 8, 16)
POOL_GROUP = 192
N_BRANCHES = 4
N_GROUPS = 8
PER_GROUP = 8
N_EXPERTS = 64
TOP_K = 2
EXPERT_HIDDEN = 512
MOE_BLOCK = 128
EXPERT_WEIGHT_SLOTS = 3
LOCAL_ROWS = 32
NORM_EPS = 1e-6
LN_EPS = 1e-5

U_QKV_A = 0
U_Z_A = 2304
U_B = 3072
U_C = 4608
U_D = 11520
U_WIDTH = 12288
W_BETA = 3072
W_UB = 3084
W_GATE = 12300

VMEM_LIMIT = 56 * 1024 * 1024


def _cparams(semantics, vmem=VMEM_LIMIT):
    return pltpu.CompilerParams(dimension_semantics=semantics, vmem_limit_bytes=vmem)


def _sigmoid(x):
    return 1.0 / (1.0 + jnp.exp(-x))


def _silu(x):
    return x * _sigmoid(x)


def _dot(a, b):
    return jnp.dot(a, b, preferred_element_type=F32)


def _dot_nt(a, b):
    return lax.dot_general(a, b, (((1,), (1,)), ((), ())), preferred_element_type=F32)


def _dot_tn(a, b):
    return lax.dot_general(a, b, (((0,), (0,)), ((), ())), preferred_element_type=F32)


def _rmsnorm_body(x_ref, g_ref, o_ref):
    x = x_ref[...]
    y = x * lax.rsqrt(jnp.mean(x * x, axis=-1, keepdims=True) + NORM_EPS) * g_ref[...]
    o_ref[...] = y.astype(o_ref.dtype)


def rmsnorm(x, g, out_dtype, tm=512):
    t, d = x.shape
    return pl.pallas_call(
        _rmsnorm_body,
        grid=(t // tm,),
        in_specs=[pl.BlockSpec((tm, d), lambda i: (i, 0)), pl.BlockSpec((1, d), lambda i: (0, 0))],
        out_specs=pl.BlockSpec((tm, d), lambda i: (i, 0)),
        out_shape=jax.ShapeDtypeStruct((t, d), out_dtype),
        compiler_params=_cparams(("parallel",)),
        name="rmsnorm",
    )(x, g.reshape(1, d))


def _wprep_copy(wt_hbm, buf, sem, j, slot, *, tn, n_plain, shift):
    row0 = j * tn + jnp.where(j >= n_plain, shift, 0)
    return pltpu.make_async_copy(wt_hbm.at[pl.ds(row0, tn)], buf.at[slot], sem.at[slot])


def _wprep_body(wt_hbm, o_ref, buf, sem, *, tn, n_plain, shift):
    j = pl.program_id(0)
    slot = j % 2
    cp = functools.partial(_wprep_copy, wt_hbm, buf, sem, tn=tn, n_plain=n_plain, shift=shift)

    @pl.when(j == 0)
    def _():
        cp(j, slot).start()

    cp(j, slot).wait()

    @pl.when(j + 1 < pl.num_programs(0))
    def _():
        cp(j + 1, 1 - slot).start()

    for l in range(o_ref.shape[0]):
        o_ref[l] = buf[slot, :, l, :].T.astype(BF16)


def prepare_w_in(w_in, tn=512):
    depth, k, n = w_in.shape
    w_t = jnp.transpose(w_in, (2, 0, 1))
    n_out = W_BETA + (n - W_UB)
    return pl.pallas_call(
        functools.partial(_wprep_body, tn=tn, n_plain=W_BETA // tn, shift=W_UB - W_BETA),
        grid=(n_out // tn,),
        in_specs=[pl.BlockSpec(memory_space=pl.ANY)],
        out_specs=pl.BlockSpec((depth, k, tn), lambda j: (0, 0, j)),
        out_shape=jax.ShapeDtypeStruct((depth, k, n_out), BF16),
        scratch_shapes=[pltpu.VMEM((2, tn, depth, k), F32), pltpu.SemaphoreType.DMA((2,))],
        compiler_params=_cparams(("arbitrary",)),
        name="prepare_w_in",
    )(w_t)


def _bd_weight_body(w_ref, o_ref):
    for l in range(o_ref.shape[0]):
        o_ref[l] = w_ref[:, l, :].T.astype(BF16)


def beta_decay_weights(w_in):
    depth, k, _ = w_in.shape
    w_t = jnp.transpose(w_in, (2, 0, 1))
    return pl.pallas_call(
        _bd_weight_body,
        grid=(1,),
        in_specs=[pl.BlockSpec((HEAD_DIM, depth, k), lambda i: (W_BETA // HEAD_DIM, 0, 0))],
        out_specs=pl.BlockSpec((depth, k, HEAD_DIM), lambda i: (0, 0, 0)),
        out_shape=jax.ShapeDtypeStruct((depth, k, HEAD_DIM), BF16),
        compiler_params=_cparams(("arbitrary",)),
        name="beta_decay_weights",
    )(w_t)


def _matmul_body(h_ref, w_ref, o_ref):
    o_ref[...] = _dot(h_ref[...], w_ref[...].astype(BF16)).astype(o_ref.dtype)


def project(h, w, layer, n, col0=0, tm=1024, tn=768):
    t, k = h.shape
    tn = min(tn, n)
    return pl.pallas_call(
        _matmul_body,
        grid=(n // tn, t // tm),
        in_specs=[pl.BlockSpec((tm, k), lambda j, i: (i, 0)),
                  pl.BlockSpec((None, k, tn), lambda j, i: (layer, 0, col0 // tn + j))],
        out_specs=pl.BlockSpec((tm, tn), lambda j, i: (i, j)),
        out_shape=jax.ShapeDtypeStruct((t, n), F32),
        compiler_params=_cparams(("parallel", "parallel")),
        name="in_proj",
    )(h, w)


def _rope(x, cos, sin):
    return x * cos + pltpu.roll(x, HEAD_DIM // 2, axis=1) * sin


def _matmul_rope_body(h_ref, w_ref, cos_ref, sin_ref, o_ref, *, q_tiles, k_tiles):
    j = pl.program_id(0)
    is_q = (j >= q_tiles[0]) & (j < q_tiles[1])
    is_k = (j >= k_tiles[0]) & (j < k_tiles[1])

    @pl.when(jnp.logical_not(is_q | is_k))
    def _():
        o_ref[...] = _dot(h_ref[...], w_ref[...])

    def rotated(scale):
        for r0 in range(0, o_ref.shape[0], ROPE_ROWS):
            rs = slice(r0, r0 + ROPE_ROWS)
            acc = _dot(h_ref[rs, :], w_ref[...])
            cos, sin = cos_ref[rs, :], sin_ref[rs, :]
            for c in range(o_ref.shape[1] // HEAD_DIM):
                sl = slice(c * HEAD_DIM, (c + 1) * HEAD_DIM)
                o_ref[rs, sl] = _rope(acc[:, sl], cos, sin) * scale

    @pl.when(is_q)
    def _():
        rotated(HEAD_DIM ** -0.5)

    @pl.when(is_k)
    def _():
        rotated(1.0)


def project_mixers(h, w, layer, cos_t, sin_t, tm=1024, tn=WIDTH):
    t, k = h.shape
    q0 = U_C // tn
    nq = 3 * WIDTH // tn
    tab = pl.BlockSpec((tm, HEAD_DIM), lambda j, i: (i, 0))
    return pl.pallas_call(
        functools.partial(_matmul_rope_body, q_tiles=(q0, q0 + nq), k_tiles=(q0 + nq, q0 + 2 * nq)),
        grid=(U_WIDTH // tn, t // tm),
        in_specs=[pl.BlockSpec((tm, k), lambda j, i: (i, 0)),
                  pl.BlockSpec((None, k, tn), lambda j, i: (layer, 0, j)), tab, tab],
        out_specs=pl.BlockSpec((tm, tn), lambda j, i: (i, j)),
        out_shape=jax.ShapeDtypeStruct((t, U_WIDTH), F32),
        compiler_params=_cparams(("parallel", "parallel")),
        name="in_proj",
    )(h, w, cos_t, sin_t)


def _rope_body(pos_ref, inv_ref, cos_ref, sin_ref):
    ang = pos_ref[...].astype(F32) * inv_ref[...]
    lane = lax.broadcasted_iota(jnp.int32, ang.shape, 1)
    cos_ref[...] = jnp.cos(ang)
    s = jnp.sin(ang)
    sin_ref[...] = jnp.where(lane < HEAD_DIM // 2, -s, s)


def rope_tables(pos, tm=1024):
    t = pos.shape[0]
    half = HEAD_DIM // 2
    inv = ROPE_THETA ** (-jnp.arange(half, dtype=F32) * 2.0 / HEAD_DIM)
    inv = jnp.concatenate([inv, inv]).reshape(1, HEAD_DIM)
    spec = pl.BlockSpec((tm, HEAD_DIM), lambda i: (i, 0))
    return pl.pallas_call(
        _rope_body,
        grid=(t // tm,),
        in_specs=[pl.BlockSpec((tm, 1), lambda i: (i, 0)), pl.BlockSpec((1, HEAD_DIM), lambda i: (0, 0))],
        out_specs=[spec, spec],
        out_shape=[jax.ShapeDtypeStruct((t, HEAD_DIM), F32)] * 2,
        compiler_params=_cparams(("parallel",)),
        name="rope_tables",
    )(pos.reshape(t, 1), inv)


def _attn_body(*refs, first, d, hw):
    per = 1 if d == 1 else hw
    n_in = 5 if first else 7
    ops = [refs[i * per:(i + 1) * per] for i in range(n_in)]
    o_out, lse_out = refs[n_in * per], refs[n_in * per + 1]
    o_scr, lse_scr = (None, None) if d == 1 else refs[n_in * per + 2:]

    def head(i, h):
        return (ops[i][0], slice(h * HEAD_DIM, (h + 1) * HEAD_DIM)) if d == 1 else (ops[i][h], slice(None))

    def load(i, h, rows):
        ref, sl = head(i, h)
        return ref[rows, sl]

    n = pl.program_id(0)
    blk = ATT_BLOCK
    row = lax.broadcasted_iota(jnp.int32, (blk, blk), 0)
    col = lax.broadcasted_iota(jnp.int32, (blk, blk), 1)
    in_cur = col <= row
    has_prev = n > 0
    neg = -jnp.inf
    units = [(pl.ds(r, blk, stride=d) if d > 1 else pl.ds(0, blk), h) for h in range(hw) for r in range(d)]
    for g0 in range(0, len(units), ATT_UNITS_IN_FLIGHT):
        group = units[g0:g0 + ATT_UNITS_IN_FLIGHT]
        scores = []
        for rows, h in group:
            q = load(0, h, rows)
            kp = load(2, h, rows)
            qb = q.astype(BF16)
            s_c = _dot_nt(qb, load(1, h, rows).astype(BF16))
            s_p = _dot_nt(qb, kp.astype(BF16))
            s_d = jnp.where(has_prev, jnp.sum(q * kp, axis=-1, keepdims=True), neg)
            scores.append((jnp.where(in_cur, s_c, jnp.where(has_prev, s_p, neg)), s_d))
        probs = []
        for (rows, h), (s_w, s_d) in zip(group, scores):
            m = jnp.maximum(jnp.max(s_w, axis=-1, keepdims=True), s_d)
            m_old = None
            if not first:
                m_old = load(6, h, rows)[:, 0:1]
                m = jnp.maximum(m, m_old)
            p_w = jnp.exp(s_w - m)
            p_d = jnp.exp(s_d - m)
            probs.append((m, m_old, p_w, p_d, jnp.sum(p_w, axis=-1, keepdims=True) + p_d))
        for (rows, h), (m, m_old, p_w, p_d, l) in zip(group, probs):
            vp = load(4, h, rows)
            acc = (_dot(jnp.where(in_cur, p_w, 0.0).astype(BF16), load(3, h, rows).astype(BF16))
                   + _dot(jnp.where(in_cur, 0.0, p_w).astype(BF16), vp.astype(BF16)) + p_d * vp)
            if not first:
                w_old = jnp.exp(m_old - m)
                l = l + w_old
                acc = acc + w_old * load(5, h, rows)
            lse = jnp.broadcast_to(m + jnp.log(l), (blk, HEAD_DIM))
            if d == 1:
                sl = slice(h * HEAD_DIM, (h + 1) * HEAD_DIM)
                o_out[rows, sl] = acc / l
                lse_out[rows, sl] = lse
            else:
                o_scr.at[h][rows, :] = acc / l
                lse_scr.at[h][rows, :] = lse
    if d > 1:
        for h in range(hw):
            sl = slice(h * HEAD_DIM, (h + 1) * HEAD_DIM)
            o_out[:, sl] = o_scr[h]
            lse_out[:, sl] = lse_scr[h]


def attention_group(u, group, dilation, o_prev, lse_prev):
    t = u.shape[0]
    d = dilation
    span = ATT_BLOCK * d
    hw = ATT_HEADS_PER_STEP[d]
    per = 1 if d == 1 else hw
    cw = hw * HEAD_DIM // per
    first = o_prev is None
    offs = [(U_C + part * 3 * WIDTH + group * WIDTH) // cw for part in range(3)]

    def block_index(n, g, *, off, back, p):
        return jnp.maximum(n - back, 0), off + g * per + p

    def spec(off, back):
        return [pl.BlockSpec((span, cw), functools.partial(block_index, off=off, back=back, p=p))
                for p in range(per)]

    in_specs = spec(offs[0], 0) + spec(offs[1], 0) + spec(offs[1], 1) + spec(offs[2], 0) + spec(offs[2], 1)
    args = [u] * (5 * per)
    if not first:
        in_specs += spec(0, 0) + spec(0, 0)
        args += [o_prev] * per + [lse_prev] * per
    o_spec = pl.BlockSpec((span, hw * HEAD_DIM), lambda n, g: (n, g))
    scratch = [] if d == 1 else [pltpu.VMEM((hw, span, HEAD_DIM), F32)] * 2
    return pl.pallas_call(
        functools.partial(_attn_body, first=first, d=d, hw=hw),
        grid=(t // span, N_HEADS // hw),
        in_specs=in_specs,
        out_specs=[o_spec, o_spec],
        out_shape=[jax.ShapeDtypeStruct((t, WIDTH), F32), jax.ShapeDtypeStruct((t, WIDTH), F32)],
        scratch_shapes=scratch,
        compiler_params=_cparams(("parallel", "parallel")),
        name=f"attn_d{d}",
    )(*args)


def _local_body(ub_ref, ud_ref, cw_ref, cb_ref, lg_ref, lb_ref, pw_ref, ps_ref, yb_ref, yd_ref, extb, extd, pool_buf, *, tm):
    i = pl.program_id(0)
    hb, hd = 32, 16

    @pl.when(i == 0)
    def _():
        extb[0:hb, :] = jnp.zeros((hb, WIDTH), F32)
        extd[0:hd, :] = jnp.zeros((hd, WIDTH), F32)

    @pl.when(i > 0)
    def _():
        extb[0:hb, :] = extb[tm:tm + hb, :]
        extd[0:hd, :] = extd[tm:tm + hd, :]

    ub = ub_ref[...]
    extb[hb:hb + tm, :] = ub[:, :WIDTH] * _sigmoid(ub[:, WIDTH:])
    extd[hd:hd + tm, :] = ud_ref[...]

    rc = LOCAL_ROWS
    ch = lax.broadcasted_iota(jnp.int32, (rc, WIDTH), 1)
    rows = lax.broadcasted_iota(jnp.int32, (rc, WIDTH), 0)
    for c0 in range(0, tm, rc):
        acc = None
        for s in range(8):
            win = rc + (8 if s else 0)
            part = None
            for j in range(CONF_CONV):
                off = hb - (CONF_CONV - 1) + j
                if off % 8 != s:
                    continue
                base = c0 + off - s
                term = cw_ref[j:j + 1, :] * extb[base:base + win, :]
                part = term if part is None else part + term
            if part is None:
                continue
            part = part[s:s + rc, :]
            acc = part if acc is None else acc + part
        cb = acc + cb_ref[...]
        mu = jnp.mean(cb, axis=-1, keepdims=True)
        cen = cb - mu
        var = jnp.mean(cen * cen, axis=-1, keepdims=True)
        yb = _silu(cen * lax.rsqrt(var + LN_EPS) * lg_ref[...] + lb_ref[...])
        yb_ref[c0:c0 + rc, :] = yb.astype(yb_ref.dtype)

        xd = extd[c0 + hd:c0 + hd + rc, :]
        tpos = i * tm + c0 + rows + 1
        run = xd
        pooled = None
        shift = 1
        for gi, w in enumerate(POOL_WINDOWS):
            while shift < w:
                run = run + extd[c0 + hd - shift:c0 + hd - shift + rc, :]
                shift += 1
            val = run / jnp.minimum(tpos, w).astype(F32) - xd
            pooled = val if pooled is None else jnp.where(ch >= gi * POOL_GROUP, val, pooled)
        pool_buf[c0:c0 + rc, :] = pooled.astype(pool_buf.dtype)
    yd = _dot(pool_buf[...], pw_ref[...]) * ps_ref[...]
    yd_ref[...] = yd.astype(yd_ref.dtype)


def local_mixers(u, conv_w, conv_b, ln_g, ln_b, pool_w_bd, pool_scale, tm=256):
    t = u.shape[0]
    row = lambda a: a.reshape(1, WIDTH)
    vec = pl.BlockSpec((1, WIDTH), lambda i: (0, 0))
    out = pl.BlockSpec((tm, WIDTH), lambda i: (i, 0))
    return pl.pallas_call(
        functools.partial(_local_body, tm=tm),
        grid=(t // tm,),
        in_specs=[pl.BlockSpec((tm, 2 * WIDTH), lambda i: (i, U_B // (2 * WIDTH))),
                  pl.BlockSpec((tm, WIDTH), lambda i: (i, U_D // WIDTH)),
                  pl.BlockSpec((CONF_CONV, WIDTH), lambda i: (0, 0)), vec, vec, vec,
                  pl.BlockSpec((WIDTH, WIDTH), lambda i: (0, 0)), vec],
        out_specs=[out, out],
        out_shape=[jax.ShapeDtypeStruct((t, WIDTH), BF16)] * 2,
        scratch_shapes=[pltpu.VMEM((tm + 32, WIDTH), F32), pltpu.VMEM((tm + 16, WIDTH), F32),
                        pltpu.VMEM((tm, WIDTH), BF16)],
        compiler_params=_cparams(("arbitrary",)),
        name="conv_pool",
    )(u, u, conv_w, row(conv_b), row(ln_g), row(ln_b), pool_w_bd, row(pool_scale))


def _gdn_body(qkv_ref, z_ref, bd_ref, cw_ref, alog_ref, dtb_ref, gn_ref, ya_ref, ext, state):
    s = pl.program_id(0)
    sup, c = GDN_SUPER, GDN_CHUNK
    halo = 8

    @pl.when(s == 0)
    def _():
        ext[0:halo, :] = jnp.zeros((halo, 3 * WIDTH), F32)
        state[...] = jnp.zeros(state.shape, F32)

    @pl.when(s > 0)
    def _():
        ext[0:halo, :] = ext[sup:sup + halo, :]

    ext[halo:halo + sup, :] = qkv_ref[...]
    acc = jnp.zeros((sup, 3 * WIDTH), F32)
    for j in range(GDN_CONV):
        off = halo - (GDN_CONV - 1) + j
        acc = acc + cw_ref[j:j + 1, :] * ext[off:off + sup, :]
    qkv = _silu(acc)

    bd = bd_ref[...]
    beta_all = _sigmoid(bd)
    xs = bd + dtb_ref[...]
    softplus = jnp.maximum(xs, 0.0) + jnp.log(1.0 + jnp.exp(-jnp.abs(xs)))
    g_all = -jnp.exp(alog_ref[...]) * softplus
    rowi = lax.broadcasted_iota(jnp.int32, (sup, HEAD_DIM), 0)
    in_chunk = rowi & (c - 1)
    gc = g_all
    sh = 1
    while sh < c:
        gc = gc + jnp.where(in_chunk >= sh, pltpu.roll(gc, sh, axis=0), 0.0)
        sh *= 2
    gct = gc.T

    ri = lax.broadcasted_iota(jnp.int32, (sup, sup), 0)
    ci = lax.broadcasted_iota(jnp.int32, (sup, sup), 1)
    same = (ri & -c) == (ci & -c)
    incl = same & (ri >= ci)
    strict = same & (ri > ci)
    eye = (ri == ci).astype(F32)

    heads = range(N_HEADS)
    hs = [slice(h * HEAD_DIM, (h + 1) * HEAD_DIM) for h in heads]
    prep = []
    for h in heads:
        q = qkv[:, hs[h]]
        k = qkv[:, WIDTH + h * HEAD_DIM:WIDTH + (h + 1) * HEAD_DIM]
        v = qkv[:, 2 * WIDTH + h * HEAD_DIM:2 * WIDTH + (h + 1) * HEAD_DIM]
        q = q * lax.rsqrt(jnp.sum(q * q, axis=-1, keepdims=True) + 1e-6) * (HEAD_DIM ** -0.5)
        k = k * lax.rsqrt(jnp.sum(k * k, axis=-1, keepdims=True) + 1e-6)
        b_col = beta_all[:, h:h + 1]
        gc_col = gc[:, N_HEADS + h:N_HEADS + h + 1]
        gc_row = gct[N_HEADS + h:N_HEADS + h + 1, :]
        dm = jnp.exp(jnp.where(incl, gc_col - gc_row, -jnp.inf))
        kb = k.astype(BF16)
        nmat = jnp.where(strict, -(b_col * _dot_nt(kb, kb) * dm), 0.0)
        prep.append((q, k, v, b_col, gc_col, dm, kb, nmat))

    pmats = [eye + p[7] for p in prep]
    mbs = [p[7].astype(BF16) for p in prep]
    step = 2
    while step < c:
        m2s = [_dot(mb, mb).astype(BF16) for mb in mbs]
        pmats = [pm + _dot(pm.astype(BF16), m2) for pm, m2 in zip(pmats, m2s)]
        mbs = m2s
        step *= 2

    chunked = []
    for h in heads:
        q, k, v, b_col, gc_col, dm, kb, _ = prep[h]
        eg = jnp.exp(gc_col)
        rhs = jnp.concatenate([v * b_col, k * (b_col * eg)], axis=1).astype(BF16)
        sol = _dot(pmats[h].astype(BF16), rhs)
        qk_b = (_dot_nt(q.astype(BF16), kb) * dm).astype(BF16)
        q_dec = (q * eg).astype(BF16)
        gl_col = jnp.concatenate(
            [jnp.broadcast_to(gc_col[(n + 1) * c - 1:(n + 1) * c, :], (c, 1)) for n in range(sup // c)], axis=0)
        k_dec = k * jnp.exp(gl_col - gc_col)
        chunked.append((sol[:, :HEAD_DIM], sol[:, HEAD_DIM:].astype(BF16), qk_b, q_dec, k_dec, gc_col))

    states = [state[h] for h in heads]
    outs = [[] for _ in heads]
    for n in range(sup // c):
        rs = slice(n * c, (n + 1) * c)
        for h in heads:
            u_c, w_b, qk_b, q_dec, k_dec, gc_col = chunked[h]
            sb = states[h].astype(BF16)
            v_new = u_c[rs] - _dot(w_b[rs], sb)
            vb = v_new.astype(BF16)
            outs[h].append(_dot(q_dec[rs], sb) + _dot(qk_b[rs, rs], vb))
            g_last = jnp.exp(gc_col[(n + 1) * c - 1:(n + 1) * c, :])
            states[h] = states[h] * g_last + _dot(k_dec[rs].T.astype(BF16), vb)

    for h in heads:
        state[h] = states[h]
        o = jnp.concatenate(outs[h], axis=0)
        o = o * lax.rsqrt(jnp.mean(o * o, axis=-1, keepdims=True) + NORM_EPS) * gn_ref[...]
        ya_ref[:, hs[h]] = (o * _silu(z_ref[:, hs[h]])).astype(ya_ref.dtype)


def gated_deltanet(u, bd, conv_w, a_log, dt_bias, norm_g):
    t = u.shape[0]
    sup = GDN_SUPER
    pad = jnp.zeros((N_HEADS,), F32)
    lane_row = lambda a: jnp.concatenate([pad, a, jnp.zeros((HEAD_DIM - 2 * N_HEADS,), F32)]).reshape(1, HEAD_DIM)
    vec = pl.BlockSpec((1, HEAD_DIM), lambda s: (0, 0))
    return pl.pallas_call(
        _gdn_body,
        grid=(t // sup,),
        in_specs=[pl.BlockSpec((sup, 3 * WIDTH), lambda s: (s, U_QKV_A // (3 * WIDTH))),
                  pl.BlockSpec((sup, WIDTH), lambda s: (s, U_Z_A // WIDTH)),
                  pl.BlockSpec((sup, HEAD_DIM), lambda s: (s, 0)),
                  pl.BlockSpec((GDN_CONV, 3 * WIDTH), lambda s: (0, 0)), vec, vec, vec],
        out_specs=pl.BlockSpec((sup, WIDTH), lambda s: (s, 0)),
        out_shape=jax.ShapeDtypeStruct((t, WIDTH), BF16),
        scratch_shapes=[pltpu.VMEM((sup + 8, 3 * WIDTH), F32), pltpu.VMEM((N_HEADS, HEAD_DIM, HEAD_DIM), F32)],
        compiler_params=_cparams(("arbitrary",)),
        name="gated_deltanet",
    )(u, u, bd, conv_w, lane_row(a_log), lane_row(dt_bias), norm_g.reshape(1, HEAD_DIM))


def _merge_body(h_ref, ya_ref, yb_ref, yc_ref, yd_ref, g0_ref, g1_ref, g2_ref, g3_ref,
                b0_ref, b1_ref, b2_ref, b3_ref, wo_ref, x_ref, n2_ref, wr_ref, br_ref,
                x1_ref, rl_ref, acc):
    j = pl.program_id(1)

    @pl.when(j == 0)
    def _():
        acc[...] = jnp.zeros(acc.shape, F32)

    hb = h_ref[...]
    merged = None
    for y_ref, g_ref, b_ref in ((ya_ref, g0_ref, b0_ref), (yb_ref, g1_ref, b1_ref),
                                (yc_ref, g2_ref, b2_ref), (yd_ref, g3_ref, b3_ref)):
        gate = _sigmoid(_dot(hb, g_ref[...]))
        term = gate * _dot(y_ref[...].astype(BF16), b_ref[...])
        merged = term if merged is None else merged + term
    acc[...] += _dot(merged.astype(BF16), wo_ref[...])

    @pl.when(j == pl.num_programs(1) - 1)
    def _():
        x1 = x_ref[...] + acc[...]
        x1_ref[...] = x1
        h2 =x1 * lax.rsqrt(jnp.mean(x1 * x1, axis=-1, keepdims=True) + NORM_EPS) * n2_ref[...]
        rl_ref[...] = jnp.dot(h2, wr_ref[...], preferred_element_type=F32,
                              precision=lax.Precision.HIGHEST) + br_ref[...]


def merge_project(h, ya, yb, yc, yd, w_all, w_branch, w_out, layer, x, norm2_g, w_router, b_router, tm=512, tn=256):
    t = h.shape[0]
    d = D_MODEL
    nj = d // tn
    g0 = U_WIDTH // tn
    row = pl.BlockSpec((tm, d), lambda i, j: (i, 0))
    ysp = pl.BlockSpec((tm, WIDTH), lambda i, j: (i, 0))
    gate_spec = lambda k: pl.BlockSpec((None, d, tn), lambda i, j: (layer, 0, g0 + k * nj + j))
    br_spec = lambda k: pl.BlockSpec((None, None, WIDTH, tn), lambda i, j: (layer, k, 0, j))
    lanes = pl.BlockSpec((tm, HEAD_DIM), lambda i, j: (i, 0))
    w_gate = w_all
    return pl.pallas_call(
        _merge_body,
        grid=(t // tm, nj),
        in_specs=[row, ysp, ysp, ysp, ysp] + [gate_spec(k) for k in range(4)] + [br_spec(k) for k in range(4)]
        + [pl.BlockSpec((None, tn, d), lambda i, j: (layer, j, 0)), row, pl.BlockSpec((1, d), lambda i, j: (0, 0)),
           pl.BlockSpec((d, HEAD_DIM), lambda i, j: (0, 0)), pl.BlockSpec((1, HEAD_DIM), lambda i, j: (0, 0))],
        out_specs=[row, lanes],
        out_shape=[jax.ShapeDtypeStruct((t, d), F32), jax.ShapeDtypeStruct((t, HEAD_DIM), F32)],
        scratch_shapes=[pltpu.VMEM((tm, d), F32)],
        compiler_params=_cparams(("parallel", "arbitrary")),
        name="merge_project",
    )(h, ya, yb, yc, yd, w_gate, w_gate, w_gate, w_gate, w_branch, w_branch, w_branch, w_branch,
      w_out, x, norm2_g.reshape(1, d), w_router, b_router)


def _router_body(rl_ref, eid_ref, wts_ref):
    rl = rl_ref[...]
    lane_i = lax.broadcasted_iota(jnp.int32, rl.shape, 1)
    lane = lane_i.astype(F32)
    neg = -jnp.inf
    big = 1e9
    gl = jnp.where(lane < N_GROUPS, rl, neg)
    gmax = jnp.max(gl, axis=-1, keepdims=True)
    gi = jnp.min(jnp.where(gl == gmax, lane, big), axis=-1, keepdims=True)
    gp = 1.0 / jnp.sum(jnp.exp(gl - gmax), axis=-1, keepdims=True)
    lo = N_GROUPS + gi * PER_GROUP
    el = jnp.where((lane >= lo) & (lane < lo + PER_GROUP), rl, neg)
    e1 = jnp.max(el, axis=-1, keepdims=True)
    i1 = jnp.min(jnp.where(el == e1, lane, big), axis=-1, keepdims=True)
    el2 = jnp.where(lane == i1, neg, el)
    e2 = jnp.max(el2, axis=-1, keepdims=True)
    i2 = jnp.min(jnp.where(el2 == e2, lane, big), axis=-1, keepdims=True)
    esum = jnp.sum(jnp.exp(el - e1), axis=-1, keepdims=True)
    p1 = 1.0 / esum
    p2 = jnp.exp(e2 - e1) / esum
    w1 = gp * p1 / (p1 + p2)
    w2 = gp * p2 / (p1 + p2)
    eid = jnp.where(lane_i == 0, i1 - N_GROUPS, jnp.where(lane_i == 1, i2 - N_GROUPS, 0.0))
    eid_ref[...] = eid.astype(jnp.int32)
    wts_ref[...] = jnp.where(lane_i == 0, w1, jnp.where(lane_i == 1, w2, 0.0))


def route(rl, tm=512):
    t = rl.shape[0]
    spec = pl.BlockSpec((tm, HEAD_DIM), lambda i: (i, 0))
    return pl.pallas_call(
        _router_body,
        grid=(t // tm,),
        in_specs=[spec],
        out_specs=[spec, spec],
        out_shape=[jax.ShapeDtypeStruct((t, HEAD_DIM), jnp.int32), jax.ShapeDtypeStruct((t, HEAD_DIM), F32)],
        compiler_params=_cparams(("parallel",)),
        name="router",
    )(rl)


def _row_copy(src, src_row, dst, dst_row, sem):
    return pltpu.make_async_copy(src.at[pl.ds(src_row, 1), :], dst.at[pl.ds(dst_row, 1), :], sem)


def _zero_copy(zbuf, zrow_ref, xs_hbm, zsem, e):
    start = pl.multiple_of(jnp.maximum(zrow_ref[e], 0), MOE_BLOCK)
    return pltpu.make_async_copy(zbuf, xs_hbm.at[pl.ds(start, MOE_BLOCK), :], zsem)


def _zero_block(zbuf, xs_hbm, zsem, blk):
    start = pl.multiple_of(blk * MOE_BLOCK, MOE_BLOCK)
    return pltpu.make_async_copy(zbuf, xs_hbm.at[pl.ds(start, MOE_BLOCK), :], zsem)


def _sort_rows_body(dest_ref, zrow_ref, nused_ref, x_ref, xs_hbm, zbuf, sem, zsem, *, tm):
    i = pl.program_id(0)
    sem, zsem = sem.at[0], zsem.at[0]

    @pl.when(i == 0)
    def _():
        zbuf[...] = jnp.zeros(zbuf.shape, F32)
        n_blocks = xs_hbm.shape[0] // MOE_BLOCK
        for e in range(N_EXPERTS):
            @pl.when(zrow_ref[e] >= 0)
            def _():
                _zero_copy(zbuf, zrow_ref, xs_hbm, zsem, e).start()

        def z_start(blk, c):
            _zero_block(zbuf, xs_hbm, zsem, blk).start()
            return c

        def z_wait(blk, c):
            _zero_block(zbuf, xs_hbm, zsem, blk).wait()
            return c

        lax.fori_loop(nused_ref[0], n_blocks, z_start, 0)
        for e in range(N_EXPERTS):
            @pl.when(zrow_ref[e] >= 0)
            def _():
                _zero_copy(zbuf, zrow_ref, xs_hbm, zsem, e).wait()
        lax.fori_loop(nused_ref[0], n_blocks, z_wait, 0)

    for r in range(tm):
        for k in range(TOP_K):
            _row_copy(x_ref, r, xs_hbm, dest_ref[(i * tm + r) * TOP_K + k], sem).start(priority=k % 2)
    for _ in range(tm * TOP_K):
        _row_copy(x_ref, 0, xs_hbm, 0, sem).wait()


def sort_rows(x1, dest, zrow, n_used, n_slots, tm=256):
    t, d = x1.shape
    grid_spec = pltpu.PrefetchScalarGridSpec(
        num_scalar_prefetch=3,
        grid=(t // tm,),
        in_specs=[pl.BlockSpec((tm, d), lambda i, dest, zrow, nu: (i, 0))],
        out_specs=pl.BlockSpec(memory_space=pl.ANY),
        scratch_shapes=[pltpu.VMEM((MOE_BLOCK, d), F32),
                        pltpu.SemaphoreType.DMA((1,)), pltpu.SemaphoreType.DMA((1,))],
    )
    return pl.pallas_call(
        functools.partial(_sort_rows_body, tm=tm),
        grid_spec=grid_spec,
        out_shape=jax.ShapeDtypeStruct((n_slots, d), F32),
        compiler_params=_cparams(("arbitrary",)),
        name="moe_sort_rows",
    )(dest, zrow, n_used, x1)


def _weight_copies(wup_hbm, wdn_hbm, wup_f, wdn_f, wsem, layer, e, slot):
    return (pltpu.make_async_copy(wup_hbm.at[layer, e], wup_f.at[slot], wsem.at[0, slot]),
            pltpu.make_async_copy(wdn_hbm.at[layer, e], wdn_f.at[slot], wsem.at[1, slot]))


def _expert_body(be_ref, nused_ref, nxt1_ref, nxt2_ref, par_ref, xs_ref, g_ref, wup_hbm, wdn_hbm, ys_ref,
                 wup_f, wdn_f, wup_b, wdn_b, wsem, *, layer):
    b = pl.program_id(0)
    copies = functools.partial(_weight_copies, wup_hbm, wdn_hbm, wup_f, wdn_f, wsem, layer)

    def start_if_any(e, slot):
        @pl.when(e >= 0)
        def _():
            for cp in copies(e, slot):
                cp.start()

    @pl.when(b == 0)
    def _():
        start_if_any(be_ref[0], par_ref[0])
        start_if_any(nxt1_ref[0], (par_ref[0] + 1) % EXPERT_WEIGHT_SLOTS)

    @pl.when(b >= nused_ref[0])
    def _():
        ys_ref[...] = jnp.zeros(ys_ref.shape, F32)

    @pl.when(b < nused_ref[0])
    def _():
        changed = jnp.logical_or(b == 0, be_ref[b] != be_ref[jnp.maximum(b - 1, 0)])

        @pl.when(changed)
        def _():
            slot = par_ref[b]
            for cp in copies(be_ref[b], slot):
                cp.wait()
            start_if_any(nxt2_ref[b], (slot + 2) % EXPERT_WEIGHT_SLOTS)

            wup_b[...] = wup_f[slot].astype(BF16)
            wdn_b[...] = wdn_f[slot].astype(BF16)

        xr = xs_ref[...]
        xn = xr * lax.rsqrt(jnp.mean(xr * xr, axis=-1, keepdims=True) + NORM_EPS) * g_ref[...]
        gu = _dot(xn.astype(BF16), wup_b[...])
        act = (_silu(gu[:, :EXPERT_HIDDEN]) * gu[:, EXPERT_HIDDEN:]).astype(BF16)
        ys_ref[...] = _dot(act, wdn_b[...])


def experts(xs, norm_g, w_up, w_down, layer, block_e, n_used, next1, next2, slot):
    n_blocks = block_e.shape[0]
    d = D_MODEL
    ns = EXPERT_WEIGHT_SLOTS
    rows = lambda b, be, nu, n1, n2, sl: (jnp.minimum(b, nu[0] - 1), 0)
    grid_spec = pltpu.PrefetchScalarGridSpec(
        num_scalar_prefetch=5,
        grid=(n_blocks,),
        in_specs=[pl.BlockSpec((MOE_BLOCK, d), rows),
                  pl.BlockSpec((1, d), lambda b, be, nu, n1, n2, sl: (0, 0)),
                  pl.BlockSpec(memory_space=pl.ANY), pl.BlockSpec(memory_space=pl.ANY)],
        out_specs=pl.BlockSpec((MOE_BLOCK, d), lambda b, be, nu, n1, n2, sl: (b, 0)),
        scratch_shapes=[pltpu.VMEM((ns, d, 2 * EXPERT_HIDDEN), F32), pltpu.VMEM((ns, EXPERT_HIDDEN, d), F32),
                        pltpu.VMEM((d, 2 * EXPERT_HIDDEN), BF16), pltpu.VMEM((EXPERT_HIDDEN, d), BF16),
                        pltpu.SemaphoreType.DMA((2, ns))],
    )
    return pl.pallas_call(
        functools.partial(_expert_body, layer=layer),
        grid_spec=grid_spec,
        out_shape=jax.ShapeDtypeStruct(xs.shape, F32),
        compiler_params=_cparams(("arbitrary",)),
        name="experts",
    )(block_e, n_used, next1, next2, slot, xs, norm_g.reshape(1, d), w_up, w_down)


def dispatch_tables(eid, t):
    n_assign = t * TOP_K
    flat_e = eid.reshape(n_assign)
    onehot = (jnp.arange(N_EXPERTS, dtype=jnp.int32)[:, None] == flat_e[None, :]).astype(F32)
    grp = onehot.reshape(N_EXPERTS, n_assign // MOE_BLOCK, MOE_BLOCK)
    tril = jnp.tril(jnp.ones((MOE_BLOCK, MOE_BLOCK), F32))
    inside = jnp.einsum("egj,ij->egi", grp, tril)
    totals = inside[:, :, -1]
    csum = (inside + (jnp.cumsum(totals, axis=1) - totals)[:, :, None]).reshape(N_EXPERTS, n_assign)
    counts = jnp.sum(totals, axis=1).astype(jnp.int32)
    rank = jnp.sum(csum * onehot, axis=0).astype(jnp.int32) - 1
    padded = (counts + MOE_BLOCK - 1) // MOE_BLOCK * MOE_BLOCK
    pad_end = jnp.cumsum(padded)
    pad_start = pad_end - padded
    dest = pad_start[flat_e] + rank
    n_blocks = -(-(n_assign + N_EXPERTS * (MOE_BLOCK - 1)) // MOE_BLOCK)
    n_slots = n_blocks * MOE_BLOCK
    zrow = jnp.where(padded > 0, pad_end - MOE_BLOCK, -1).astype(jnp.int32)
    n_used = pad_end[-1] // MOE_BLOCK
    starts = jnp.arange(n_blocks, dtype=jnp.int32) * MOE_BLOCK
    block_e = jnp.sum((pad_end[None, :] <= starts[:, None]).astype(jnp.int32), axis=1)
    block_e = jnp.minimum(block_e, N_EXPERTS - 1)
    last_e = block_e[jnp.maximum(n_used - 1, 0)]
    block_e = jnp.where(jnp.arange(n_blocks) < n_used, block_e, last_e)
    ids = jnp.arange(N_EXPERTS, dtype=jnp.int32)
    later = (ids[None, :] > ids[:, None]) & (counts[None, :] > 0)
    next_nonempty = jnp.min(jnp.where(later, ids[None, :], N_EXPERTS), axis=1)
    next_nonempty = jnp.where(next_nonempty == N_EXPERTS, -1, next_nonempty).astype(jnp.int32)
    rank_nonempty = jnp.cumsum((counts > 0).astype(jnp.int32)) - 1
    next1 = next_nonempty[block_e]
    next2 = jnp.where(next1 >= 0, next_nonempty[jnp.maximum(next1, 0)], -1)
    slot = (rank_nonempty[block_e] % EXPERT_WEIGHT_SLOTS).astype(jnp.int32)
    return (block_e, n_used.reshape(1).astype(jnp.int32), next1, next2, slot, dest.astype(jnp.int32), zrow,
            n_slots)


def _combine_rows(dest_ref, ys_hbm, ybuf, sem, tile, slot, tm, wait):
    for r in range(tm):
        for k in range(TOP_K):
            if wait:
                _row_copy(ys_hbm, 0, ybuf.at[slot, k], 0, sem.at[slot]).wait()
            else:
                _row_copy(ys_hbm, dest_ref[(tile * tm + r) * TOP_K + k], ybuf.at[slot, k], r,
                          sem.at[slot]).start(priority=k % 2)


def _combine_body(dest_ref, x_ref, w_ref, g_ref, ys_hbm, x2_ref, hn_ref, ybuf, sem, *, tm):
    i = pl.program_id(0)
    slot = i % 2

    @pl.when(i == 0)
    def _():
        _combine_rows(dest_ref, ys_hbm, ybuf, sem, i, slot, tm, wait=False)

    _combine_rows(dest_ref, ys_hbm, ybuf, sem, i, slot, tm, wait=True)

    @pl.when(i + 1 < pl.num_programs(0))
    def _():
        _combine_rows(dest_ref, ys_hbm, ybuf, sem, i + 1, 1 - slot, tm, wait=False)

    w = w_ref[...]
    x2 = x_ref[...] + (w[:, 0:1] * ybuf[slot, 0] + w[:, 1:2] * ybuf[slot, 1])
    x2_ref[...] = x2
    hn = x2 * lax.rsqrt(jnp.mean(x2 * x2, axis=-1, keepdims=True) + NORM_EPS) * g_ref[...]
    hn_ref[...] = hn.astype(hn_ref.dtype)


def combine(x1, ys, dest, wts, next_g, next_dtype, tm=128):
    t, d = x1.shape
    row = pl.BlockSpec((tm, d), lambda i, dest: (i, 0))
    grid_spec = pltpu.PrefetchScalarGridSpec(
        num_scalar_prefetch=1,
        grid=(t // tm,),
        in_specs=[row, pl.BlockSpec((tm, HEAD_DIM), lambda i, dest: (i, 0)),
                  pl.BlockSpec((1, d), lambda i, dest: (0, 0)),
                  pl.BlockSpec(memory_space=pl.ANY)],
        out_specs=[row, row],
        scratch_shapes=[pltpu.VMEM((2, TOP_K, tm, d), F32), pltpu.SemaphoreType.DMA((2,))],
    )
    return pl.pallas_call(
        functools.partial(_combine_body, tm=tm),
        grid_spec=grid_spec,
        out_shape=[jax.ShapeDtypeStruct((t, d), F32), jax.ShapeDtypeStruct((t, d), next_dtype)],
        compiler_params=_cparams(("arbitrary",)),
        name="moe_combine",
    )(dest, x1, wts, next_g.reshape(1, d), ys)


def _layer(x, h, cos_t, sin_t, layer, w_bd, w_all, conv_a_w, a_log, dt_bias, gdn_norm_g, conv_b_w, conv_b_b,
           ln_b_g, ln_b_b, pool_w, pool_scale, w_branch, w_out, norm2_g, wg, bg, we, be, w_up, w_down,
           next_g, next_dtype):
    t = x.shape[0]
    pool_bd = jnp.zeros((WIDTH, WIDTH), F32)
    for gi in range(len(POOL_WINDOWS)):
        pool_bd = lax.dynamic_update_slice(pool_bd, pool_w[gi], (gi * POOL_GROUP, gi * POOL_GROUP))
    w_router = jnp.pad(jnp.concatenate([wg, we], axis=1), ((0, 0), (0, HEAD_DIM - N_GROUPS - N_EXPERTS)))
    b_router = jnp.pad(jnp.concatenate([bg, be]), (0, HEAD_DIM - N_GROUPS - N_EXPERTS)).reshape(1, HEAD_DIM)

    u = project_mixers(h, w_all, layer, cos_t, sin_t)
    bd = project(h, w_bd, layer, HEAD_DIM)
    ya = gated_deltanet(u, bd, conv_a_w, a_log, dt_bias, gdn_norm_g)
    yb, yd = local_mixers(u, conv_b_w, conv_b_b, ln_b_g, ln_b_b, pool_bd.astype(BF16), pool_scale)
    yc = lse = None
    for gi, (_, dilation) in enumerate(ATT_PATTERNS):
        yc, lse = attention_group(u, gi, dilation, yc, lse)
    x1, rl = merge_project(h, ya, yb, yc, yd, w_all, w_branch, w_out, layer, x, norm2_g, w_router, b_router)
    eid, wts = route(rl)
    block_e, n_used, next1, next2, slot, dest, zrow, n_slots = dispatch_tables(eid[:, :TOP_K], t)
    xs = sort_rows(x1, dest, zrow, n_used, n_slots)
    ys = experts(xs, norm2_g, w_up, w_down, layer, block_e, n_used, next1, next2, slot)
    return combine(x1, ys, dest, wts, next_g, next_dtype)


def kernel(x, positions, norm1_g, w_in, conv_a_w, a_log, dt_bias, gdn_norm_g, conv_b_w, conv_b_b, ln_b_g, ln_b_b,
           pool_w, pool_scale, w_branch, w_out, norm2_g, router_group_w, router_group_b, router_expert_w,
           router_expert_b, w_up, w_down, final_norm_g):
    b_, s_, d = x.shape
    depth = w_in.shape[0]
    outs = []
    w_all = prepare_w_in(w_in)
    w_bd = beta_decay_weights(w_in)
    w_branch_b = w_branch.astype(BF16)
    w_out_b = w_out.astype(BF16)
    for bi in range(b_):
        xb = x[bi]
        cos_t, sin_t = rope_tables(positions[bi])
        h = rmsnorm(xb, norm1_g[0], BF16)
        for layer in range(depth):
            last = layer == depth - 1
            next_g = final_norm_g if last else norm1_g[layer + 1]
            xb, h = _layer(xb, h, cos_t, sin_t, layer, w_bd, w_all, conv_a_w[layer], a_log[layer], dt_bias[layer],
                           gdn_norm_g[layer], conv_b_w[layer], conv_b_b[layer], ln_b_g[layer], ln_b_b[layer],
                           pool_w[layer], pool_scale[layer], w_branch_b, w_out_b, norm2_g[layer],
                           router_group_w[layer], router_group_b[layer], router_expert_w[layer],
                           router_expert_b[layer], w_up, w_down, next_g, F32 if last else BF16)
        outs.append(h)
    return jnp.stack(outs, axis=0)
```
